```python
import math, functools
import jax, jax.numpy as jnp
from jax import lax
import numpy as np

D_MODEL = 2048
BATCH = 4
SEQ = 2048
DEPTH = 2

GRID_W = 64
CTX_LEN = 256
N_MOD = 9
D_FF = 5632
CONV_W = 4
EPS = 1e-6
D_MIX = D_MODEL

SSD_WIDTH = D_MIX // 4
SSD_HEADDIM = 64
SSD_HEADS = SSD_WIDTH // SSD_HEADDIM
SSD_GROUPS = 2
SSD_STATE = 64
SSD_CHUNK = 128
SSD_XBC = SSD_WIDTH + 2 * SSD_GROUPS * SSD_STATE
LRU_WIDTH = D_MIX // 4
LRU_BLOCKS = 8
LRU_BLOCK_DIM = LRU_WIDTH // LRU_BLOCKS
LRU_C = 8.0
HGRN_WIDTH = D_MIX // 4
HGRN_HEADDIM = 128
HGRN_HEADS = HGRN_WIDTH // HGRN_HEADDIM
HGRN_CHUNK = 16
RET_WIDTH = D_MIX // 4
RET_V_DIM = 128
RET_HEADS = RET_WIDTH // RET_V_DIM
RET_QK_DIM = RET_V_DIM // 2
RET_CHUNK = 128
ROPE_BASE = 10000.0

SSD_COLS = SSD_WIDTH + SSD_XBC + 2 * SSD_HEADS
LRU_COLS = 2 * LRU_WIDTH
HGRN_COLS = 5 * HGRN_WIDTH
RET_COLS = 2 * RET_HEADS * RET_QK_DIM + 2 * RET_WIDTH
IN_COLS = SSD_COLS + LRU_COLS + HGRN_COLS + RET_COLS

F32 = jnp.float32

kernel_name = 'hybrid_parallel_ssd_rglru_hgrn2_retention_dit'


def rmsnorm(x, g=None):
    xf = x.astype(F32)
    y = xf * lax.rsqrt(jnp.mean(xf * xf, axis=-1, keepdims=True) + EPS)
    if g is not None:
        y = y * g.astype(F32)
    return y.astype(x.dtype)


def dwconv(x, w, b):
    y = lax.conv_general_dilated(
        x, w.astype(x.dtype)[:, None, :], window_strides=(1,),
        padding=[((CONV_W - 1) // 2, CONV_W // 2)],
        dimension_numbers=('NWC', 'WIO', 'NWC'), feature_group_count=x.shape[-1])
    return y + b.astype(x.dtype)


def rope_1d(x, pos):
    half = x.shape[-1] // 2
    inv = ROPE_BASE ** (-jnp.arange(half, dtype=F32) / half)
    ang = pos.astype(F32)[:, None] * inv[None, :]
    cos, sin = jnp.cos(ang)[:, None, :], jnp.sin(ang)[:, None, :]
    x1, x2 = x[..., :half], x[..., half:]
    return jnp.concatenate([x1 * cos - x2 * sin, x1 * sin + x2 * cos], axis=-1)


def rope_2d(x, rows, cols):
    h = x.shape[-1] // 2
    return jnp.concatenate([rope_1d(x[..., :h], rows), rope_1d(x[..., h:], cols)], axis=-1)


def linear_scan(a, b, h0):
    def comb(l, r):
        return (l[0] * r[0], r[0] * l[1] + r[1])
    a_cum, b_cum = lax.associative_scan(comb, (a, b), axis=1)
    h = a_cum * h0[:, None, :] + b_cum
    return h, h[:, -1]


def scalar_decay_chunks(q, k, v, log_a, s0, chunk):
    Bn, T, H, N = q.shape
    P = v.shape[-1]
    nc = T // chunk
    qc = q.reshape(Bn, nc, chunk, H, N)
    kc = k.reshape(Bn, nc, chunk, H, N)
    vc = v.reshape(Bn, nc, chunk, H, P)
    cum = jnp.cumsum(log_a.astype(F32).reshape(Bn, nc, chunk, H), axis=2)
    tri = jnp.tril(jnp.ones((chunk, chunk), bool))[:, :, None]
    diff = cum[:, :, :, None, :] - cum[:, :, None, :, :]
    decay = jnp.where(tri, jnp.exp(jnp.where(tri, diff, 0.0)), 0.0)
    scores = jnp.einsum('bcthn,bcshn->bctsh', qc, kc) * decay
    y = jnp.einsum('bctsh,bcshp->bcthp', scores, vc)
    last = cum[:, :, -1]
    u = jnp.einsum('bcshn,bcshp->bchnp', kc * jnp.exp(last[:, :, None] - cum)[..., None], vc)

    def step(s, inp):
        g, uc = inp
        return g[..., None, None] * s + uc, s

    s_fin, s_in = lax.scan(step, s0, (jnp.moveaxis(jnp.exp(last), 1, 0), jnp.moveaxis(u, 1, 0)))
    y = y + jnp.einsum('bcthn,bchnp->bcthp', qc * jnp.exp(cum)[..., None], jnp.moveaxis(s_in, 0, 1))
    return y.reshape(Bn, T, H, P), s_fin


def vector_decay_chunks(q, k, v, log_f, s0, chunk):
    Bn, T, H, K = q.shape
    V = v.shape[-1]
    nc = T // chunk
    qc = q.reshape(Bn, nc, chunk, H, K)
    kc = k.reshape(Bn, nc, chunk, H, K)
    vc = v.reshape(Bn, nc, chunk, H, V)
    cum = jnp.cumsum(log_f.astype(F32).reshape(Bn, nc, chunk, H, K), axis=2)
    tri = jnp.tril(jnp.ones((chunk, chunk), bool))[:, :, None, None]
    diff = cum[:, :, :, None] - cum[:, :, None, :]
    decay = jnp.where(tri, jnp.exp(jnp.where(tri, diff, 0.0)), 0.0)
    scores = jnp.einsum('bcthk,bcshk,bctshk->bctsh', qc, kc, decay)
    y = jnp.einsum('bctsh,bcshv->bcthv', scores, vc)
    last = cum[:, :, -1]
    u = jnp.einsum('bcshk,bcshv->bchkv', kc * jnp.exp(last[:, :, None] - cum), vc)

    def step(s, inp):
        g, uc = inp
        return g[..., None] * s + uc, s

    s_fin, s_in = lax.scan(step, s0, (jnp.moveaxis(jnp.exp(last), 1, 0), jnp.moveaxis(u, 1, 0)))
    y = y + jnp.einsum('bcthk,bchkv->bcthv', qc * jnp.exp(cum), jnp.moveaxis(s_in, 0, 1))
    return y.reshape(Bn, T, H, V), s_fin


def bidir_scan(scan, ctx_args, lat_args, s0, reverse):
    if reverse:
        ctx_args = tuple(jnp.flip(t, axis=1) for t in ctx_args)
        lat_args = tuple(jnp.flip(t, axis=1) for t in lat_args)
    y_c, s_c = scan(*ctx_args, s0)
    y_l, _ = scan(*lat_args, s_c)
    if reverse:
        y_c, y_l = jnp.flip(y_c, axis=1), jnp.flip(y_l, axis=1)
    return y_c, y_l


def ssd_mixer(u_c, u_l, conv_w, conv_b, dt_bias, a_log, d_skip, norm_w):
    def prep(u):
        Bn, T = u.shape[:2]
        z, xbc, dt = jnp.split(u, [SSD_WIDTH, SSD_WIDTH + SSD_XBC], axis=-1)
        xbc = jax.nn.silu(dwconv(xbc, conv_w, conv_b))
        xs, bm, cm = jnp.split(xbc, [SSD_WIDTH, SSD_WIDTH + SSD_GROUPS * SSD_STATE], axis=-1)
        rep = SSD_HEADS // SSD_GROUPS
        bm = jnp.repeat(bm.reshape(Bn, T, SSD_GROUPS, SSD_STATE), rep, axis=2)
        cm = jnp.repeat(cm.reshape(Bn, T, SSD_GROUPS, SSD_STATE), rep, axis=2)
        return z, xs.reshape(Bn, T, SSD_HEADS, SSD_HEADDIM), bm, cm, dt

    def dir_args(stream, d):
        _, xs, bm, cm, dt = stream
        delta = jax.nn.softplus(dt[..., d * SSD_HEADS:(d + 1) * SSD_HEADS] + dt_bias[d])
        return (cm, bm * delta[..., None], xs, -jnp.exp(a_log[d]) * delta)

    sc, sl = prep(u_c), prep(u_l)
    scan = functools.partial(scalar_decay_chunks, chunk=SSD_CHUNK)
    s0 = jnp.zeros((u_c.shape[0], SSD_HEADS, SSD_STATE, SSD_HEADDIM), F32)
    y_c = sc[1] * d_skip[:, None]
    y_l = sl[1] * d_skip[:, None]
    for d in range(2):
        yc_d, yl_d = bidir_scan(scan, dir_args(sc, d), dir_args(sl, d), s0, d == 1)
        y_c = y_c + yc_d
        y_l = y_l + yl_d

    def out(y, z):
        return rmsnorm(y.reshape(z.shape) * jax.nn.silu(z), norm_w)

    return out(y_c, sc[0]), out(y_l, sl[0])


def rglru_mixer(u_c, u_l, conv_w, conv_b, wa, ba, wx, bx, lam):
    def prep(u):
        xb, gb = jnp.split(u, 2, axis=-1)
        return dwconv(xb, conv_w, conv_b), gb

    def dir_args(xb, d):
        Bn, T = xb.shape[:2]
        xr = xb.reshape(Bn, T, LRU_BLOCKS, LRU_BLOCK_DIM)
        r = jax.nn.sigmoid(jnp.einsum('bthi,hij->bthj', xr, wa[d]) + ba[d])
        i = jax.nn.sigmoid(jnp.einsum('bthi,hij->bthj', xr, wx[d]) + bx[d])
        log_a = -LRU_C * r * jax.nn.softplus(-lam[d]).reshape(LRU_BLOCKS, LRU_BLOCK_DIM)
        b = jnp.sqrt(jnp.maximum(-jnp.expm1(2.0 * log_a), 1e-12)) * (i * xr)
        return (jnp.exp(log_a).reshape(Bn, T, LRU_WIDTH), b.reshape(Bn, T, LRU_WIDTH))

    (xc, gc), (xl, gl) = prep(u_c), prep(u_l)
    h0 = jnp.zeros((u_c.shape[0], LRU_WIDTH), F32)
    h_c = jnp.zeros_like(xc)
    h_l = jnp.zeros_like(xl)
    for d in range(2):
        hc_d, hl_d = bidir_scan(linear_scan, dir_args(xc, d), dir_args(xl, d), h0, d == 1)
        h_c = h_c + hc_d
        h_l = h_l + hl_d
    return h_c * jax.nn.gelu(gc), h_l * jax.nn.gelu(gl)


def hgrn2_mixer(u_c, u_l, lb, norm_w):
    def prep(u):
        Bn, T = u.shape[:2]
        hs = lambda t: t.reshape(Bn, T, HGRN_HEADS, HGRN_HEADDIM)
        q, ff, fb, i, g = jnp.split(u, 5, axis=-1)
        return hs(jax.nn.silu(q) * HGRN_HEADDIM ** -0.5), (hs(ff), hs(fb)), hs(i), g

    def dir_args(stream, d):
        q, fr, i, _ = stream
        lbd = lb[d].reshape(HGRN_HEADS, HGRN_HEADDIM)
        f = lbd + (1.0 - lbd) * jax.nn.sigmoid(fr[d])
        return (q, 1.0 - f, i, jnp.log(f))

    sc, sl = prep(u_c), prep(u_l)
    scan = functools.partial(vector_decay_chunks, chunk=HGRN_CHUNK)
    s0 = jnp.zeros((u_c.shape[0], HGRN_HEADS, HGRN_HEADDIM, HGRN_HEADDIM), F32)
    o_c = jnp.zeros_like(sc[2])
    o_l = jnp.zeros_like(sl[2])
    for d in range(2):
        oc_d, ol_d = bidir_scan(scan, dir_args(sc, d), dir_args(sl, d), s0, d == 1)
        o_c = o_c + oc_d
        o_l = o_l + ol_d

    def out(o, g):
        return rmsnorm(o, norm_w).reshape(g.shape) * jax.nn.silu(g)

    return out(o_c, sc[3]), out(o_l, sl[3])


def retention_mixer(u_c, u_l, rows, cols, log_decay):
    qk = RET_HEADS * RET_QK_DIM

    def prep(u, rotate):
        Bn, T = u.shape[:2]
        q, k, v, g = jnp.split(u, [qk, 2 * qk, 2 * qk + RET_WIDTH], axis=-1)
        q = q.reshape(Bn, T, RET_HEADS, RET_QK_DIM)
        k = k.reshape(Bn, T, RET_HEADS, RET_QK_DIM)
        if rotate:
            q, k = rope_2d(q, rows, cols), rope_2d(k, rows, cols)
        la = jnp.broadcast_to(log_decay, (Bn, T, RET_HEADS))
        return (q, k * RET_QK_DIM ** -0.5, v.reshape(Bn, T, RET_HEADS, RET_V_DIM), la), g

    (ac, gc), (al, gl) = prep(u_c, False), prep(u_l, True)
    scan = functools.partial(scalar_decay_chunks, chunk=RET_CHUNK)
    s0 = jnp.zeros((u_c.shape[0], RET_HEADS, RET_QK_DIM, RET_V_DIM), F32)
    o_c = jnp.zeros_like(ac[2])
    o_l = jnp.zeros_like(al[2])
    for d in range(2):
        oc_d, ol_d = bidir_scan(scan, ac, al, s0, d == 1)
        o_c = o_c + oc_d
        o_l = o_l + ol_d

    def out(o, g):
        return jax.nn.silu(g) * rmsnorm(o).reshape(g.shape)

    return out(o_c, gc), out(o_l, gl)


def swiglu_half_step(x, m, j, g_pre, g_post, w1, w3, w2):
    h = rmsnorm(x, g_pre) * (1.0 + m[:, j + 1]) + m[:, j]
    y = (jax.nn.silu(h @ w1) * (h @ w3)) @ w2
    return x + 0.5 * m[:, j + 2] * rmsnorm(y, g_post)


def setup_inputs(seed: int = 0) -> dict:
    key = jax.random.key(seed)
    ks = iter(jax.random.split(key, 40))

    def nrm(shape, s):
        return jax.random.normal(next(ks), shape, F32) * s

    dt = jnp.exp(jax.random.uniform(next(ks), (DEPTH, 2, SSD_HEADS), F32,
                                    minval=math.log(1e-3), maxval=math.log(1e-1)))
    a_c = jax.random.uniform(next(ks), (DEPTH, 2, LRU_WIDTH), F32, minval=0.9, maxval=0.999)
    a_lru = a_c ** (1.0 / LRU_C)
    return {
        'x': nrm((BATCH, SEQ, D_MODEL), 1.0),
        'c': nrm((BATCH, D_MODEL), 1.0),
        'ctx': nrm((BATCH, CTX_LEN, D_MODEL), 1.0),
        'c_ctx': nrm((D_MODEL,), 1.0),
        'w_mod': nrm((DEPTH, D_MODEL, N_MOD * D_MODEL), 0.5 * D_MODEL ** -0.5),
        'b_mod': nrm((DEPTH, N_MOD * D_MODEL), 0.02),
        'norm_pre': 1.0 + nrm((DEPTH, 3, D_MODEL), 0.02),
        'norm_post': 1.0 + nrm((DEPTH, 3, D_MODEL), 0.02),
        'ffn_w1': nrm((DEPTH, 2, D_MODEL, D_FF), D_MODEL ** -0.5),
        'ffn_w3': nrm((DEPTH, 2, D_MODEL, D_FF), D_MODEL ** -0.5),
        'ffn_w2': nrm((DEPTH, 2, D_FF, D_MODEL), D_FF ** -0.5),
        'w_in': nrm((DEPTH, D_MODEL, IN_COLS), D_MODEL ** -0.5),
        'w_out': nrm((DEPTH, D_MIX, D_MODEL), D_MIX ** -0.5),
        'ssd_conv_w': nrm((DEPTH, CONV_W, SSD_XBC), CONV_W ** -0.5),
        'ssd_conv_b': nrm((DEPTH, SSD_XBC), 0.02),
        'ssd_dt_bias': dt + jnp.log(-jnp.expm1(-dt)),
        'ssd_a_log': jnp.log(jax.random.uniform(next(ks), (DEPTH, 2, SSD_HEADS), F32, minval=1.0, maxval=16.0)),
        'ssd_d': 1.0 + nrm((DEPTH, SSD_HEADS), 0.1),
        'ssd_norm_w': 1.0 + nrm((DEPTH, SSD_WIDTH), 0.02),
        'lru_conv_w': nrm((DEPTH, CONV_W, LRU_WIDTH), CONV_W ** -0.5),
        'lru_conv_b': nrm((DEPTH, LRU_WIDTH), 0.02),
        'lru_wa': nrm((DEPTH, 2, LRU_BLOCKS, LRU_BLOCK_DIM, LRU_BLOCK_DIM), LRU_BLOCK_DIM ** -0.5),
        'lru_ba': nrm((DEPTH, 2, LRU_BLOCKS, LRU_BLOCK_DIM), 0.02),
        'lru_wx': nrm((DEPTH, 2, LRU_BLOCKS, LRU_BLOCK_DIM, LRU_BLOCK_DIM), LRU_BLOCK_DIM ** -0.5),
        'lru_bx': nrm((DEPTH, 2, LRU_BLOCKS, LRU_BLOCK_DIM), 0.02),
        'lru_lambda': jnp.log(a_lru) - jnp.log1p(-a_lru),
        'hgrn_lb_logits': nrm((2, DEPTH, HGRN_WIDTH), 1.0),
        'hgrn_norm_w': 1.0 + nrm((DEPTH, HGRN_HEADDIM), 0.02),
    }


def reference(x, c, ctx, c_ctx, w_mod, b_mod, norm_pre, norm_post, ffn_w1, ffn_w3, ffn_w2,
              w_in, w_out, ssd_conv_w, ssd_conv_b, ssd_dt_bias, ssd_a_log, ssd_d, ssd_norm_w,
              lru_conv_w, lru_conv_b, lru_wa, lru_ba, lru_wx, lru_bx, lru_lambda,
              hgrn_lb_logits, hgrn_norm_w):
    Bn, L, D = x.shape
    ROWS = L // GRID_W
    rr, cc = jnp.meshgrid(jnp.arange(ROWS), jnp.arange(GRID_W), indexing='ij')
    rows, cols = rr.reshape(-1), cc.reshape(-1)
    sm = jax.nn.softmax(hgrn_lb_logits.astype(F32), axis=1)
    lb_all = jnp.cumsum(sm, axis=1) - sm[:, :1]
    ret_log_decay = jnp.log1p(-jnp.exp2(-5.0 - jnp.arange(RET_HEADS, dtype=F32)))
    splits = [SSD_COLS, SSD_COLS + LRU_COLS, SSD_COLS + LRU_COLS + HGRN_COLS]

    xl, xc = x, ctx
    for l in range(DEPTH):
        last = l == DEPTH - 1
        m_l = (jax.nn.silu(c) @ w_mod[l] + b_mod[l]).reshape(Bn, N_MOD, 1, D)
        m_c = (jax.nn.silu(c_ctx) @ w_mod[l] + b_mod[l]).reshape(1, N_MOD, 1, D)

        xl = swiglu_half_step(xl, m_l, 0, norm_pre[l, 0], norm_post[l, 0], ffn_w1[l, 0], ffn_w3[l, 0], ffn_w2[l, 0])
        xc = swiglu_half_step(xc, m_c, 0, norm_pre[l, 0], norm_post[l, 0], ffn_w1[l, 0], ffn_w3[l, 0], ffn_w2[l, 0])

        h_l = rmsnorm(xl, norm_pre[l, 1]) * (1.0 + m_l[:, 4]) + m_l[:, 3]
        h_c = rmsnorm(xc, norm_pre[l, 1]) * (1.0 + m_c[:, 4]) + m_c[:, 3]
        ua_l, ub_l, uc_l, ud_l = jnp.split((h_l @ w_in[l]).astype(F32), splits, axis=-1)
        ua_c, ub_c, uc_c, ud_c = jnp.split((h_c @ w_in[l]).astype(F32), splits, axis=-1)
        ya = ssd_mixer(ua_c, ua_l, ssd_conv_w[l], ssd_conv_b[l], ssd_dt_bias[l], ssd_a_log[l], ssd_d[l], ssd_norm_w[l])
        yb = rglru_mixer(ub_c, ub_l, lru_conv_w[l], lru_conv_b[l], lru_wa[l], lru_ba[l], lru_wx[l], lru_bx[l], lru_lambda[l])
        yc = hgrn2_mixer(uc_c, uc_l, lb_all[:, l], hgrn_norm_w[l])
        yd = retention_mixer(ud_c, ud_l, rows, cols, ret_log_decay)
        y_l = jnp.concatenate([ya[1], yb[1], yc[1], yd[1]], axis=-1).astype(xl.dtype) @ w_out[l]
        xl = xl + m_l[:, 5] * rmsnorm(y_l, norm_post[l, 1])
        if not last:
            y_c = jnp.concatenate([ya[0], yb[0], yc[0], yd[0]], axis=-1).astype(xc.dtype) @ w_out[l]
            xc = xc + m_c[:, 5] * rmsnorm(y_c, norm_post[l, 1])

        xl = swiglu_half_step(xl, m_l, 6, norm_pre[l, 2], norm_post[l, 2], ffn_w1[l, 1], ffn_w3[l, 1], ffn_w2[l, 1])
        if not last:
            xc = swiglu_half_step(xc, m_c, 6, norm_pre[l, 2], norm_post[l, 2], ffn_w1[l, 1], ffn_w3[l, 1], ffn_w2[l, 1])
    return xl
```

```python
import functools
import math

import numpy as np
import jax
import jax.numpy as jnp
from jax import lax
from jax.experimental import pallas as pl
from jax.experimental.pallas import tpu as pltpu

F32 = jnp.float32
BF16 = jnp.bfloat16

D_MODEL = 2048
N_MOD = 9
D_FF = 5632
CONV_W = 4
EPS = 1e-6
GRID_W = 64

SSD_WIDTH = 512
SSD_HEADDIM = 64
SSD_HEADS = 8
SSD_GROUPS = 2
SSD_STATE = 64
SSD_XBC = SSD_WIDTH + 2 * SSD_GROUPS * SSD_STATE
SSD_COLS = SSD_WIDTH + SSD_XBC + 2 * SSD_HEADS
SSD_PAD = 1408
LRU_WIDTH = 512
LRU_BLOCKS = 8
LRU_BLOCK_DIM = 64
LRU_C = 8.0
LRU_COLS = 2 * LRU_WIDTH
HGRN_WIDTH = 512
HGRN_HEADDIM = 128
HGRN_HEADS = 4
HGRN_COLS = 5 * HGRN_WIDTH
RET_WIDTH = 512
RET_V_DIM = 128
RET_HEADS = 4
RET_QK_DIM = 64
RET_COLS = 2 * RET_HEADS * RET_QK_DIM + 2 * RET_WIDTH
ROPE_BASE = 10000.0

LANES = 128
CHUNK = 128
SUB = 16
VMEM_LIMIT = 56 * 1024 * 1024


def _cparams(n_axes):
    return pltpu.CompilerParams(dimension_semantics=("arbitrary",) * n_axes,
                                vmem_limit_bytes=VMEM_LIMIT)


def _bdot(a, b):
    return jnp.dot(a.astype(BF16), b.astype(BF16), preferred_element_type=F32)


def _bdot_nt(a, b):
    return lax.dot_general(a.astype(BF16), b.astype(BF16), (((1,), (1,)), ((), ())),
                           preferred_element_type=F32)


def _split3(x):
    hi = x.astype(BF16)
    r1 = x - hi.astype(F32)
    mid = r1.astype(BF16)
    lo = (r1 - mid.astype(F32)).astype(BF16)
    return hi, mid, lo


def _dot_sel(sel, parts):
    return (jnp.dot(sel, parts[0], preferred_element_type=F32)
            + jnp.dot(sel, parts[1], preferred_element_type=F32)
            + jnp.dot(sel, parts[2], preferred_element_type=F32))


def _sigmoid(x):
    return 1.0 / (1.0 + jnp.exp(-x))


def _silu(x):
    return x * _sigmoid(x)


def _softplus(x):
    return jnp.maximum(x, 0.0) + jnp.log(1.0 + jnp.exp(-jnp.abs(x)))


def _iota(shape, dim):
    return lax.broadcasted_iota(jnp.int32, shape, dim)


def _mod_kernel(c_ref, w_ref, b_ref, o_ref):
    c = c_ref[...]
    o_ref[0] = _bdot(_silu(c), w_ref[0]) + b_ref[0]


def _modulation(c_all, w_mod, b_mod):
    depth, d, n = w_mod.shape
    tn = 1024
    return pl.pallas_call(
        _mod_kernel,
        grid=(depth, n // tn),
        in_specs=[pl.BlockSpec((8, d), lambda l, j: (0, 0)),
                  pl.BlockSpec((1, d, tn), lambda l, j: (l, 0, j)),
                  pl.BlockSpec((1, 1, tn), lambda l, j: (l, 0, j))],
        out_specs=pl.BlockSpec((1, 8, tn), lambda l, j: (l, 0, j)),
        out_shape=jax.ShapeDtypeStruct((depth, 8, n), F32),
        compiler_params=_cparams(2),
        name="modulation",
    )(c_all, w_mod, b_mod.reshape(depth, 1, n))


def _ffn_kernel(x_ref, mod_ref, gpre_ref, gpost_ref, w1_ref, w3_ref, w2_ref, o_ref, h_scr, acc_scr, *, j, n_f):
    f = pl.program_id(1)

    @pl.when(f == 0)
    def _():
        x = x_ref[...]
        y = x * lax.rsqrt(jnp.mean(x * x, axis=-1, keepdims=True) + EPS) * gpre_ref[...]
        h = y * (1.0 + mod_ref[0, j + 1:j + 2, :]) + mod_ref[0, j:j + 1, :]
        h_scr[...] = h.astype(BF16)
        acc_scr[...] = jnp.zeros_like(acc_scr)

    h = h_scr[...]
    a = jnp.dot(h, w1_ref[...], preferred_element_type=F32)
    b = jnp.dot(h, w3_ref[...], preferred_element_type=F32)
    g = (_silu(a) * b).astype(BF16)
    acc_scr[...] += jnp.dot(g, w2_ref[...], preferred_element_type=F32)

    @pl.when(f == n_f - 1)
    def _():
        y = acc_scr[...]
        yn = y * lax.rsqrt(jnp.mean(y * y, axis=-1, keepdims=True) + EPS) * gpost_ref[...]
        o_ref[...] = x_ref[...] + 0.5 * mod_ref[0, j + 2:j + 3, :] * yn


def _ffn(x, mod, gpre, gpost, w1, w3, w2, *, j, n_rows, lat_rows, seq, n_batch):
    d = x.shape[1]
    dff = w1.shape[1]
    tm, tf = 512, 512
    n_f = dff // tf

    def mod_idx(i, f):
        return (jnp.where(i * tm < lat_rows, (i * tm) // seq, n_batch), 0, 0)

    return pl.pallas_call(
        functools.partial(_ffn_kernel, j=j, n_f=n_f),
        grid=(n_rows // tm, n_f),
        in_specs=[pl.BlockSpec((tm, d), lambda i, f: (i, 0)),
                  pl.BlockSpec((1, N_MOD, d), mod_idx),
                  pl.BlockSpec((1, d), lambda i, f: (0, 0)),
                  pl.BlockSpec((1, d), lambda i, f: (0, 0)),
                  pl.BlockSpec((d, tf), lambda i, f: (0, f)),
                  pl.BlockSpec((d, tf), lambda i, f: (0, f)),
                  pl.BlockSpec((tf, d), lambda i, f: (f, 0))],
        out_specs=pl.BlockSpec((tm, d), lambda i, f: (i, 0)),
        out_shape=jax.ShapeDtypeStruct((n_rows, d), F32),
        scratch_shapes=[pltpu.VMEM((tm, d), BF16), pltpu.VMEM((tm, d), F32)],
        compiler_params=_cparams(2),
        name="ffn",
    )(x, mod, gpre, gpost, w1, w3, w2)


def _inproj_kernel(x_ref, mod_ref, gpre_ref, w_ref, o_ref):
    x = x_ref[...]
    y = x * lax.rsqrt(jnp.mean(x * x, axis=-1, keepdims=True) + EPS) * gpre_ref[...]
    h = y * (1.0 + mod_ref[0, 4:5, :]) + mod_ref[0, 3:4, :]
    o_ref[...] = jnp.dot(h.astype(BF16), w_ref[...], preferred_element_type=F32)


def _x_to_b_block(i, tm, lat_rows, seq, ctx_len):
    per_b = (seq + ctx_len) // tm
    lat_b = seq // tm
    ctx_b = ctx_len // tm
    n_lat = lat_rows // tm
    lat_idx = (i // lat_b) * per_b + ctx_b + i % lat_b
    ic = i - n_lat
    ctx_idx = (ic // ctx_b) * per_b + ic % ctx_b
    return jnp.where(i < n_lat, lat_idx, ctx_idx)


def _inproj(x, mod, gpre, w, *, lat_rows, seq, ctx_len, n_batch):
    rows, d = x.shape
    n = w.shape[1]
    tm = 256

    def mod_idx(i):
        return (jnp.where(i * tm < lat_rows, (i * tm) // seq, n_batch), 0, 0)

    return pl.pallas_call(
        _inproj_kernel,
        grid=(rows // tm,),
        in_specs=[pl.BlockSpec((tm, d), lambda i: (i, 0)),
                  pl.BlockSpec((1, N_MOD, d), mod_idx),
                  pl.BlockSpec((1, d), lambda i: (0, 0)),
                  pl.BlockSpec((d, n), lambda i: (0, 0))],
        out_specs=pl.BlockSpec((tm, n), lambda i: (_x_to_b_block(i, tm, lat_rows, seq, ctx_len), 0)),
        out_shape=jax.ShapeDtypeStruct((rows, n), F32),
        compiler_params=_cparams(1),
        name="inproj",
    )(x, mod, gpre, w)


def _outproj_kernel(x_ref, ya_ref, yb_ref, yc_ref, yd_ref, w_ref, mod_ref, gpost_ref, o_ref):
    w = LRU_WIDTH
    y = (jnp.dot(ya_ref[...], w_ref[0 * w:1 * w, :], preferred_element_type=F32)
         + jnp.dot(yb_ref[...], w_ref[1 * w:2 * w, :], preferred_element_type=F32)
         + jnp.dot(yc_ref[...], w_ref[2 * w:3 * w, :], preferred_element_type=F32)
         + jnp.dot(yd_ref[...], w_ref[3 * w:4 * w, :], preferred_element_type=F32))
    yn = y * lax.rsqrt(jnp.mean(y * y, axis=-1, keepdims=True) + EPS) * gpost_ref[...]
    o_ref[...] = x_ref[...] + mod_ref[0, 5:6, :] * yn


def _outproj(x, ys, w, mod, gpost, *, n_rows, lat_rows, seq, ctx_len, n_batch):
    d = x.shape[1]
    tm = 256
    wy = ys[0].shape[1]

    def mod_idx(i):
        return (jnp.where(i * tm < lat_rows, (i * tm) // seq, n_batch), 0, 0)

    def y_idx(i):
        return (_x_to_b_block(i, tm, lat_rows, seq, ctx_len), 0)

    return pl.pallas_call(
        _outproj_kernel,
        grid=(n_rows // tm,),
        in_specs=[pl.BlockSpec((tm, d), lambda i: (i, 0))]
        + [pl.BlockSpec((tm, wy), y_idx)] * 4
        + [pl.BlockSpec((4 * wy, d), lambda i: (0, 0)),
           pl.BlockSpec((1, N_MOD, d), mod_idx),
           pl.BlockSpec((1, d), lambda i: (0, 0))],
        out_specs=pl.BlockSpec((tm, d), lambda i: (i, 0)),
        out_shape=jax.ShapeDtypeStruct((n_rows, d), F32),
        compiler_params=_cparams(1),
        name="outproj",
    )(x, *ys, w, mod, gpost)


def _conv_chunk(u_ref, col0, ncols, base, first, last, t_tot, w_ref, b_ref):
    cols = slice(col0, col0 + ncols)
    prev = u_ref[pl.ds(pl.multiple_of(jnp.maximum(base - 8, 0), 8), 8), cols]
    nxt = u_ref[pl.ds(pl.multiple_of(jnp.minimum(base + CHUNK, t_tot - 8), 8), 8), cols]
    cur = u_ref[pl.ds(base, CHUNK), cols]
    prev = jnp.where(first, 0.0, prev)
    nxt = jnp.where(last, 0.0, nxt)
    win = jnp.concatenate([prev, cur, nxt], axis=0)
    n = CHUNK + 16
    xm1 = pltpu.roll(win, 1, axis=0)[8:8 + CHUNK]
    xp1 = pltpu.roll(win, n - 1, axis=0)[8:8 + CHUNK]
    xp2 = pltpu.roll(win, n - 2, axis=0)[8:8 + CHUNK]
    return (w_ref[0:1, :] * xm1 + w_ref[1:2, :] * cur + w_ref[2:3, :] * xp1 + w_ref[3:4, :] * xp2
            + b_ref[...])


def _chunk_order(i, nc_c, nc, reverse):
    if not reverse:
        return i
    return jnp.where(i < nc_c, nc_c - 1 - i, nc + nc_c - 1 - i)


def _stream_edges(c, nc_c, nc):
    first = jnp.logical_or(c == 0, c == nc_c)
    last = jnp.logical_or(c == nc_c - 1, c == nc - 1)
    return first, last


def _ssd_kernel(u_ref, cw_ref, cb_ref, dtb_ref, alog_ref, dskip_ref, nw_ref, o_ref,
                act_scr, dl_scr, y_scr, s_scr, *, nc_c, nc):
    t_tot = nc * CHUNK
    xbc0 = SSD_WIDTH
    dt0 = SSD_WIDTH + SSD_XBC

    def prep(c, _):
        base = pl.multiple_of(c * CHUNK, CHUNK)
        first, last = _stream_edges(c, nc_c, nc)
        conv = _conv_chunk(u_ref, xbc0, SSD_XBC, base, first, last, t_tot, cw_ref, cb_ref)
        act_scr[pl.ds(base, CHUNK), :] = _silu(conv)
        dt = u_ref[pl.ds(base, CHUNK), dt0:dt0 + LANES]
        dl_scr[pl.ds(base, CHUNK), :] = _softplus(dt + dtb_ref[...])
        return 0

    lax.fori_loop(0, nc, prep, 0)

    a_neg = -jnp.exp(alog_ref[...])
    row = _iota((CHUNK, CHUNK), 0)
    col = _iota((CHUNK, CHUNK), 1)
    lane_lo = _iota((CHUNK, LANES), 1) < SSD_HEADDIM
    grp_mask = [(_iota((CHUNK, LANES), 1) // SSD_STATE) == g for g in range(SSD_GROUPS)]

    for d in range(2):
        tri = (row >= col) if d == 0 else (row <= col)
        tri_b = tri.astype(BF16)
        edge = CHUNK - 1 if d == 0 else 0
        s_scr[...] = jnp.zeros_like(s_scr)

        def body(i, _, d=d, tri=tri, tri_b=tri_b, edge=edge):
            c = _chunk_order(i, nc_c, nc, d == 1)
            base = pl.multiple_of(c * CHUNK, CHUNK)
            act = act_scr[pl.ds(base, CHUNK), :]
            bm = act[:, SSD_WIDTH:SSD_WIDTH + LANES]
            cm = act[:, SSD_WIDTH + LANES:SSD_WIDTH + 2 * LANES]
            delta = dl_scr[pl.ds(base, CHUNK), :]
            la = delta * a_neg
            cum_col = _dot_sel(tri_b, _split3(la))
            cum_row = cum_col.T
            delta_row = delta.T
            bt = bm.T
            gmat = [_bdot(jnp.where(grp_mask[g], cm, 0.0), bt) for g in range(SSD_GROUPS)]
            for k in range(SSD_HEADS // 2):
                g = (2 * k) // (SSD_HEADS // SSD_GROUPS)
                x_tile = act[:, k * LANES:(k + 1) * LANES]
                cg = jnp.where(grp_mask[g], cm, 0.0)
                m_list, w_list, e_list, tot_list = [], [], [], []
                for h in (2 * k, 2 * k + 1):
                    j = d * SSD_HEADS + h
                    ccol = cum_col[:, j:j + 1]
                    crow = cum_row[j:j + 1, :]
                    drow = delta_row[j:j + 1, :]
                    dec = jnp.where(tri, jnp.exp(jnp.where(tri, ccol - crow, 0.0)), 0.0)
                    m_list.append(gmat[g] * dec * drow)
                    tot = cum_row[j:j + 1, edge:edge + 1]
                    w_list.append(bt * (drow * jnp.exp(tot - crow)))
                    e_list.append(jnp.exp(ccol))
                    tot_list.append(jnp.exp(tot))
                y2 = _bdot(jnp.concatenate(m_list, axis=0), x_tile)
                u2 = _bdot(jnp.concatenate(w_list, axis=0), x_tile)
                s_old = s_scr[:, k * LANES:(k + 1) * LANES]
                y_inter = _bdot(cg, s_old) * jnp.where(lane_lo, e_list[0], e_list[1])
                y_tile = jnp.where(lane_lo, y2[:CHUNK], y2[CHUNK:]) + y_inter
                s_scr[:, k * LANES:(k + 1) * LANES] = (
                    s_old * jnp.where(lane_lo, tot_list[0], tot_list[1])
                    + jnp.where(lane_lo, u2[:CHUNK], u2[CHUNK:]))
                if d == 0:
                    y_scr[pl.ds(base, CHUNK), k * LANES:(k + 1) * LANES] = (
                        y_tile + x_tile * dskip_ref[:, k * LANES:(k + 1) * LANES])
                else:
                    y_scr[pl.ds(base, CHUNK), k * LANES:(k + 1) * LANES] += y_tile
            return 0

        lax.fori_loop(0, nc, body, 0)

    def fin(c, _):
        base = pl.multiple_of(c * CHUNK, CHUNK)
        z = u_ref[pl.ds(base, CHUNK), 0:SSD_WIDTH]
        y = y_scr[pl.ds(base, CHUNK), :] * _silu(z)
        yn = y * lax.rsqrt(jnp.mean(y * y, axis=-1, keepdims=True) + EPS) * nw_ref[...]
        o_ref[pl.ds(base, CHUNK), :] = yn.astype(o_ref.dtype)
        return 0

    lax.fori_loop(0, nc, fin, 0)


def _ssd(u, conv_w, conv_b, dt_bias, a_log, d_skip, norm_w, *, n_batch, t_ctx, t_lat):
    t_tot = t_ctx + t_lat
    nc_c, nc = t_ctx // CHUNK, t_tot // CHUNK
    pad = lambda v: jnp.pad(v.reshape(1, -1), ((0, 0), (0, LANES - v.size)))
    const = lambda shape: pl.BlockSpec(shape, lambda b: (0,) * len(shape))
    return pl.pallas_call(
        functools.partial(_ssd_kernel, nc_c=nc_c, nc=nc),
        grid=(n_batch,),
        in_specs=[pl.BlockSpec((t_tot, SSD_PAD), lambda b: (b, 0)),
                  const((CONV_W, SSD_XBC)), const((1, SSD_XBC)), const((1, LANES)), const((1, LANES)),
                  const((1, SSD_WIDTH)), const((1, SSD_WIDTH))],
        out_specs=pl.BlockSpec((t_tot, SSD_WIDTH), lambda b: (b, 0)),
        out_shape=jax.ShapeDtypeStruct((n_batch * t_tot, SSD_WIDTH), BF16),
        scratch_shapes=[pltpu.VMEM((t_tot, SSD_XBC), F32), pltpu.VMEM((t_tot, LANES), F32),
                        pltpu.VMEM((t_tot, SSD_WIDTH), F32), pltpu.VMEM((CHUNK, SSD_WIDTH), F32)],
        compiler_params=_cparams(1),
        name="ssd",
    )(u, conv_w, conv_b.reshape(1, -1), pad(dt_bias), pad(a_log),
      jnp.repeat(d_skip, SSD_HEADDIM).reshape(1, -1), norm_w.reshape(1, -1))


def _lru_kernel(u_ref, cw_ref, cb_ref, wg_ref, bg_ref, lam_ref, o_ref, xc_scr, h_scr, *, nc_c, nc):
    t_tot = nc * CHUNK
    w = LRU_WIDTH

    def prep(c, _):
        base = pl.multiple_of(c * CHUNK, CHUNK)
        first, last = _stream_edges(c, nc_c, nc)
        xc_scr[pl.ds(base, CHUNK), :] = _conv_chunk(u_ref, 0, w, base, first, last, t_tot, cw_ref, cb_ref)
        return 0

    lax.fori_loop(0, nc, prep, 0)

    sub = _iota((CHUNK, w), 0) % 8
    n_tiles = CHUNK // 8

    for d in range(2):
        coef = -LRU_C * _softplus(-lam_ref[d:d + 1, :])

        def body(i, h_prev, d=d, coef=coef):
            c = _chunk_order(i, nc_c, nc, d == 1)
            base = pl.multiple_of(c * CHUNK, CHUNK)
            xc = xc_scr[pl.ds(base, CHUNK), :]
            gates = _bdot(xc, wg_ref[d]) + bg_ref[d:d + 1, :]
            r = _sigmoid(gates[:, :w])
            ig = _sigmoid(gates[:, w:])
            log_a = r * coef
            a = jnp.exp(log_a)
            b = jnp.sqrt(jnp.maximum(1.0 - jnp.exp(2.0 * log_a), 1e-12)) * (ig * xc)
            for k in (1, 2, 4):
                if d == 0:
                    keep = sub >= k
                    shift = k
                else:
                    keep = sub < 8 - k
                    shift = CHUNK - k
                a_s = jnp.where(keep, pltpu.roll(a, shift, axis=0), 1.0)
                b_s = jnp.where(keep, pltpu.roll(b, shift, axis=0), 0.0)
                b = a * b_s + b
                a = a * a_s
            tiles = range(n_tiles) if d == 0 else range(n_tiles - 1, -1, -1)
            outs = [None] * n_tiles
            for t in tiles:
                h_t = a[t * 8:(t + 1) * 8] * h_prev + b[t * 8:(t + 1) * 8]
                outs[t] = h_t
                h_prev = h_t[7:8] if d == 0 else h_t[0:1]
            h = jnp.concatenate(outs, axis=0)
            if d == 0:
                h_scr[pl.ds(base, CHUNK), :] = h
            else:
                gate = u_ref[pl.ds(base, CHUNK), w:2 * w]
                out = (h_scr[pl.ds(base, CHUNK), :] + h) * jax.nn.gelu(gate)
                o_ref[pl.ds(base, CHUNK), :] = out.astype(o_ref.dtype)
            return h_prev

        lax.fori_loop(0, nc, body, jnp.zeros((1, w), F32))


def _block_diag(wb):
    nb, bi, bj = wb.shape
    eye = jnp.eye(nb, dtype=wb.dtype)
    return jnp.einsum('hij,hg->higj', wb, eye).reshape(nb * bi, nb * bj)


def _lru(u, conv_w, conv_b, wa, ba, wx, bx, lam, *, n_batch, t_ctx, t_lat):
    t_tot = t_ctx + t_lat
    nc_c, nc = t_ctx // CHUNK, t_tot // CHUNK
    w = LRU_WIDTH
    wg = jnp.stack([jnp.concatenate([_block_diag(wa[d]), _block_diag(wx[d])], axis=1) for d in range(2)])
    bg = jnp.stack([jnp.concatenate([ba[d].reshape(-1), bx[d].reshape(-1)]) for d in range(2)])
    const = lambda shape: pl.BlockSpec(shape, lambda b: (0,) * len(shape))
    return pl.pallas_call(
        functools.partial(_lru_kernel, nc_c=nc_c, nc=nc),
        grid=(n_batch,),
        in_specs=[pl.BlockSpec((t_tot, LRU_COLS), lambda b: (b, 0)),
                  const((CONV_W, w)), const((1, w)), const((2, w, 2 * w)), const((2, 2 * w)), const((2, w))],
        out_specs=pl.BlockSpec((t_tot, w), lambda b: (b, 0)),
        out_shape=jax.ShapeDtypeStruct((n_batch * t_tot, w), BF16),
        scratch_shapes=[pltpu.VMEM((t_tot, w), F32), pltpu.VMEM((t_tot, w), F32)],
        compiler_params=_cparams(1),
        name="rglru",
    )(u, conv_w, conv_b.reshape(1, -1), wg.astype(BF16), bg, lam)


def _hgrn_kernel(q_ref, ff_ref, fb_ref, i_ref, g_ref, lbl_ref, nw_ref, o_ref, o_scr, st_scr, *, layer, nc_c, nc):
    n_sub = CHUNK // SUB
    blk_r = _iota((CHUNK, CHUNK), 0) // SUB
    blk_c = _iota((CHUNK, CHUNK), 1) // SUB
    row = _iota((CHUNK, CHUNK), 0)
    col = _iota((CHUNK, CHUNK), 1)
    same = blk_r == blk_c
    ones_blk = same.astype(BF16)
    tio = _iota((n_sub, SUB, LANES), 1)
    row_blk = _iota((CHUNK, LANES), 0) // SUB
    scale = HGRN_HEADDIM ** -0.5

    for d in range(2):
        logits = lbl_ref[d]
        ex = jnp.exp(logits - jnp.max(logits, axis=0, keepdims=True))
        sm = ex / jnp.sum(ex, axis=0, keepdims=True)
        lb = jnp.sum(sm[0:layer + 1], axis=0, keepdims=True) - sm[0:1]
        f_ref = ff_ref if d == 0 else fb_ref
        tri_blk = jnp.logical_and(same, (row >= col) if d == 0 else (row <= col)).astype(BF16)
        st_scr[...] = jnp.zeros_like(st_scr)

        def body(i, _, d=d, lb=lb, f_ref=f_ref, tri_blk=tri_blk):
            c = _chunk_order(i, nc_c, nc, d == 1)
            base = pl.multiple_of(c * CHUNK, CHUNK)
            qh = _silu(q_ref[pl.ds(base, CHUNK), :]) * scale
            f = lb + (1.0 - lb) * _sigmoid(f_ref[pl.ds(base, CHUNK), :])
            kk = 1.0 - f
            gl = jnp.log(f)
            v = i_ref[pl.ds(base, CHUNK), :]
            parts = _split3(gl)
            cum = _dot_sel(tri_blk, parts)
            tot = _dot_sel(ones_blk, parts)
            q3 = qh.reshape(n_sub, SUB, LANES)
            c3 = cum.reshape(n_sub, SUB, LANES)
            k3 = kk.reshape(n_sub, SUB, LANES)
            v3 = v.reshape(n_sub, SUB, LANES)
            y3 = jnp.zeros((n_sub, SUB, LANES), F32)
            for s in range(SUB):
                m = (tio >= s) if d == 0 else (tio <= s)
                e = jnp.exp(jnp.where(m, c3 - c3[:, s:s + 1, :], 0.0))
                p = jnp.where(m, q3 * e * k3[:, s:s + 1, :], 0.0)
                y3 = y3 + jnp.sum(p, axis=-1, keepdims=True) * v3[:, s:s + 1, :]
            q_in = qh * jnp.exp(cum)
            k_out = kk * jnp.exp(tot - cum)
            v_t = v.T
            st = st_scr[...]
            outs = [None] * n_sub
            order = range(n_sub) if d == 0 else range(n_sub - 1, -1, -1)
            for jb in order:
                r0 = jb * SUB
                outs[jb] = _bdot_nt(q_in[r0:r0 + SUB], st)
                k_blk = jnp.where(row_blk == jb, k_out, 0.0)
                st = st * jnp.exp(tot[r0:r0 + 1, :]) + _bdot(v_t, k_blk)
            st_scr[...] = st
            y = y3.reshape(CHUNK, LANES) + jnp.concatenate(outs, axis=0)
            if d == 0:
                o_scr[pl.ds(base, CHUNK), :] = y
            else:
                o = o_scr[pl.ds(base, CHUNK), :] + y
                on = o * lax.rsqrt(jnp.mean(o * o, axis=-1, keepdims=True) + EPS) * nw_ref[...]
                out = on * _silu(g_ref[pl.ds(base, CHUNK), :])
                o_ref[pl.ds(base, CHUNK), :] = out.astype(o_ref.dtype)
            return 0

        lax.fori_loop(0, nc, body, 0)


def _hgrn(u, lb_logits, norm_w, *, layer, n_batch, t_ctx, t_lat):
    t_tot = t_ctx + t_lat
    nc_c, nc = t_ctx // CHUNK, t_tot // CHUNK
    nh = HGRN_HEADS
    depth = lb_logits.shape[1]
    col = lambda part: pl.BlockSpec((t_tot, LANES), lambda b, h, part=part: (b, part * nh + h))
    return pl.pallas_call(
        functools.partial(_hgrn_kernel, layer=layer, nc_c=nc_c, nc=nc),
        grid=(n_batch, nh),
        in_specs=[col(0), col(1), col(2), col(3), col(4),
                  pl.BlockSpec((2, depth, LANES), lambda b, h: (0, 0, h)),
                  pl.BlockSpec((1, LANES), lambda b, h: (0, 0))],
        out_specs=pl.BlockSpec((t_tot, LANES), lambda b, h: (b, h)),
        out_shape=jax.ShapeDtypeStruct((n_batch * t_tot, HGRN_WIDTH), BF16),
        scratch_shapes=[pltpu.VMEM((t_tot, LANES), F32), pltpu.VMEM((HGRN_HEADDIM, HGRN_HEADDIM), F32)],
        compiler_params=_cparams(2),
        name="hgrn2",
    )(u, u, u, u, u, lb_logits, norm_w.reshape(1, -1))


def _ret_consts(t_ctx, t_lat):
    t_tot = t_ctx + t_lat
    gam = 1.0 - np.exp2(-5.0 - np.arange(RET_HEADS, dtype=np.float64))
    t = np.arange(CHUNK, dtype=np.float64)
    diff = np.abs(t[:, None] - t[None, :])
    dsym = gam[:, None, None] ** diff[None]
    dsym[:, np.arange(CHUNK), np.arange(CHUNK)] = 2.0
    head_of_lane = np.repeat(np.arange(RET_HEADS), RET_QK_DIM)
    qf = gam[head_of_lane][None, :] ** (t[:, None] + 1.0)
    qb = gam[head_of_lane][None, :] ** (CHUNK - t[:, None])
    kf = gam[:, None] ** (CHUNK - 1.0 - t[None, :])
    kb = gam[:, None] ** t[None, :]
    gtot = gam ** CHUNK
    pos = np.arange(t_lat)
    rows_p, cols_p = pos // GRID_W, pos % GRID_W
    quarter = RET_QK_DIM // 4
    inv = ROPE_BASE ** (-np.arange(quarter, dtype=np.float64) / quarter)
    ang_r = rows_p[:, None] * inv[None, :]
    ang_c = cols_p[:, None] * inv[None, :]
    cos_h = np.concatenate([np.cos(ang_r), np.cos(ang_r), np.cos(ang_c), np.cos(ang_c)], axis=1)
    sin_h = np.concatenate([-np.sin(ang_r), np.sin(ang_r), -np.sin(ang_c), np.sin(ang_c)], axis=1)
    cos_t = np.concatenate([np.ones((t_ctx, RET_QK_DIM)), cos_h], axis=0)
    sin_t = np.concatenate([np.zeros((t_ctx, RET_QK_DIM)), sin_h], axis=0)
    cos_t = np.tile(cos_t, (1, RET_HEADS))
    sin_t = np.tile(sin_t, (1, RET_HEADS))
    f32 = lambda a: jnp.asarray(a, dtype=F32)
    return (f32(dsym), f32(qf), f32(qb), f32(np.concatenate([kf, kb], axis=0)), [float(g) for g in gtot],
            f32(cos_t), f32(sin_t))


def _ret_kernel(u_ref, dsym_ref, qf_ref, qb_ref, kfac_ref, cos_ref, sin_ref, o_ref, o_scr, s_scr, *,
                gtot, nc_c, nc):
    nqk = RET_HEADS * RET_QK_DIM
    lane = _iota((CHUNK, nqk), 1)
    low_half = (lane % (RET_QK_DIM // 2)) < (RET_QK_DIM // 4)
    tile_lane = _iota((CHUNK, LANES), 1)
    head_mask = [(tile_lane // RET_QK_DIM) == p for p in range(2)]
    kscale = RET_QK_DIM ** -0.5

    def rope(x, cos, sin):
        swapped = jnp.where(low_half, pltpu.roll(x, nqk - RET_QK_DIM // 4, axis=1),
                            pltpu.roll(x, RET_QK_DIM // 4, axis=1))
        return x * cos + swapped * sin

    for d in range(2):
        s_scr[...] = jnp.zeros_like(s_scr)
        qfac = qf_ref if d == 0 else qb_ref

        def body(i, _, d=d, qfac=qfac):
            c = _chunk_order(i, nc_c, nc, d == 1)
            base = pl.multiple_of(c * CHUNK, CHUNK)
            cos = cos_ref[pl.ds(base, CHUNK), :]
            sin = sin_ref[pl.ds(base, CHUNK), :]
            q = rope(u_ref[pl.ds(base, CHUNK), 0:nqk], cos, sin)
            k = rope(u_ref[pl.ds(base, CHUNK), nqk:2 * nqk], cos, sin) * kscale
            kt = k.T
            qd = q * qfac[...]
            for h in range(RET_HEADS):
                p = h // 2
                v = u_ref[pl.ds(base, CHUNK), 2 * nqk + h * RET_V_DIM:2 * nqk + (h + 1) * RET_V_DIM]
                hm = head_mask[h % 2]
                kt_p = kt[p * LANES:(p + 1) * LANES]
                s_old = s_scr[h]
                y = _bdot(jnp.where(hm, qd[:, p * LANES:(p + 1) * LANES], 0.0), s_old)
                kfac = kfac_ref[d * RET_HEADS + h:d * RET_HEADS + h + 1, :]
                s_scr[h] = gtot[h] * s_old + _bdot(kt_p * kfac, v)
                if d == 0:
                    raw = _bdot(jnp.where(hm, q[:, p * LANES:(p + 1) * LANES], 0.0), kt_p)
                    y = y + _bdot(raw * dsym_ref[h], v)
                    o_scr[pl.ds(base, CHUNK), h * RET_V_DIM:(h + 1) * RET_V_DIM] = y
                else:
                    o = o_scr[pl.ds(base, CHUNK), h * RET_V_DIM:(h + 1) * RET_V_DIM] + y
                    on = o * lax.rsqrt(jnp.mean(o * o, axis=-1, keepdims=True) + EPS)
                    gate = u_ref[pl.ds(base, CHUNK), 2 * nqk + RET_WIDTH + h * RET_V_DIM:
                                 2 * nqk + RET_WIDTH + (h + 1) * RET_V_DIM]
                    o_ref[pl.ds(base, CHUNK), h * RET_V_DIM:(h + 1) * RET_V_DIM] = (
                        _silu(gate) * on).astype(o_ref.dtype)
            return 0

        lax.fori_loop(0, nc, body, 0)


def _ret(u, *, n_batch, t_ctx, t_lat):
    t_tot = t_ctx + t_lat
    nc_c, nc = t_ctx // CHUNK, t_tot // CHUNK
    dsym, qf, qb, kfac, gtot, cos_t, sin_t = _ret_consts(t_ctx, t_lat)
    nqk = RET_HEADS * RET_QK_DIM
    const = lambda shape: pl.BlockSpec(shape, lambda b: (0,) * len(shape))
    return pl.pallas_call(
        functools.partial(_ret_kernel, gtot=gtot, nc_c=nc_c, nc=nc),
        grid=(n_batch,),
        in_specs=[pl.BlockSpec((t_tot, RET_COLS), lambda b: (b, 0)),
                  const((RET_HEADS, CHUNK, CHUNK)), const((CHUNK, nqk)), const((CHUNK, nqk)),
                  const((2 * RET_HEADS, CHUNK)), const((t_tot, nqk)), const((t_tot, nqk))],
        out_specs=pl.BlockSpec((t_tot, RET_WIDTH), lambda b: (b, 0)),
        out_shape=jax.ShapeDtypeStruct((n_batch * t_tot, RET_WIDTH), BF16),
        scratch_shapes=[pltpu.VMEM((t_tot, RET_WIDTH), F32), pltpu.VMEM((RET_HEADS, LANES, RET_V_DIM), F32)],
        compiler_params=_cparams(1),
        name="retention",
    )(u, dsym, qf, qb, kfac, cos_t, sin_t)


def kernel(x, c, ctx, c_ctx, w_mod, b_mod, norm_pre, norm_post, ffn_w1, ffn_w3, ffn_w2, w_in, w_out,
           ssd_conv_w, ssd_conv_b, ssd_dt_bias, ssd_a_log, ssd_d, ssd_norm_w,
           lru_conv_w, lru_conv_b, lru_wa, lru_ba, lru_wx, lru_bx, lru_lambda,
           hgrn_lb_logits, hgrn_norm_w):
    n_batch, seq, d = x.shape
    ctx_len = ctx.shape[1]
    depth = w_mod.shape[0]
    lat_rows = n_batch * seq
    all_rows = lat_rows + n_batch * ctx_len
    geom = dict(lat_rows=lat_rows, seq=seq, ctx_len=ctx_len, n_batch=n_batch)
    mix = dict(n_batch=n_batch, t_ctx=ctx_len, t_lat=seq)

    xs = jnp.concatenate([x.reshape(lat_rows, d), ctx.reshape(n_batch * ctx_len, d)], axis=0)
    c_all = jnp.concatenate([c, c_ctx[None, :], jnp.zeros((8 - n_batch - 1, d), F32)], axis=0)
    mods = _modulation(c_all, w_mod, b_mod).reshape(depth, 8, N_MOD, d)

    s0, s1, s2 = SSD_COLS, SSD_COLS + LRU_COLS, SSD_COLS + LRU_COLS + HGRN_COLS
    for l in range(depth):
        last = l == depth - 1
        mod = mods[l]
        w1, w3, w2 = ffn_w1[l].astype(BF16), ffn_w3[l].astype(BF16), ffn_w2[l].astype(BF16)
        wi = w_in[l].astype(BF16)
        wi_ssd = jnp.pad(wi[:, :s0], ((0, 0), (0, SSD_PAD - SSD_COLS)))
        gpre = norm_pre[l].reshape(3, 1, d)
        gpost = norm_post[l].reshape(3, 1, d)

        xs = _ffn(xs, mod, gpre[0], gpost[0], w1[0], w3[0], w2[0], j=0, n_rows=all_rows,
                  lat_rows=lat_rows, seq=seq, n_batch=n_batch)

        ua = _inproj(xs, mod, gpre[1], wi_ssd, **geom)
        ub = _inproj(xs, mod, gpre[1], wi[:, s0:s1], **geom)
        uc = _inproj(xs, mod, gpre[1], wi[:, s1:s2], **geom)
        ud = _inproj(xs, mod, gpre[1], wi[:, s2:], **geom)
        ya = _ssd(ua, ssd_conv_w[l], ssd_conv_b[l], ssd_dt_bias[l], ssd_a_log[l], ssd_d[l], ssd_norm_w[l], **mix)
        yb = _lru(ub, lru_conv_w[l], lru_conv_b[l], lru_wa[l], lru_ba[l], lru_wx[l], lru_bx[l],
                  lru_lambda[l], **mix)
        yc = _hgrn(uc, hgrn_lb_logits, hgrn_norm_w[l], layer=l, **mix)
        yd = _ret(ud, **mix)
        n_rows = lat_rows if last else all_rows
        xs_new = _outproj(xs, (ya, yb, yc, yd), w_out[l].astype(BF16), mod, gpost[1], n_rows=n_rows, **geom)
        xs = xs_new
        xs = _ffn(xs, mod, gpre[2], gpost[2], w1[1], w3[1], w2[1], j=6, n_rows=n_rows,
                  lat_rows=lat_rows, seq=seq, n_batch=n_batch)
    return xs.reshape(n_batch, seq, d)
```

```python
import functools

import numpy as np
import jax
import jax.numpy as jnp
from jax import lax
from jax.experimental import pallas as pl
from jax.experimental.pallas import tpu as pltpu

F32 = jnp.float32
BF16 = jnp.bfloat16

N_MOD = 9
CONV_W = 4
EPS = 1e-6
GRID_W = 64

SSD_WIDTH = 512
SSD_HEADDIM = 64
SSD_HEADS = 8
SSD_GROUPS = 2
SSD_STATE = 64
SSD_XBC = SSD_WIDTH + 2 * SSD_GROUPS * SSD_STATE
SSD_COLS = SSD_WIDTH + SSD_XBC + 2 * SSD_HEADS
SSD_PAD = 1408
LRU_WIDTH = 512
LRU_C = 8.0
LRU_COLS = 2 * LRU_WIDTH
HGRN_WIDTH = 512
HGRN_HEADDIM = 128
HGRN_HEADS = 4
HGRN_COLS = 5 * HGRN_WIDTH
RET_WIDTH = 512
RET_V_DIM = 128
RET_HEADS = 4
RET_QK_DIM = 64
RET_COLS = 2 * RET_HEADS * RET_QK_DIM + 2 * RET_WIDTH
ROPE_BASE = 10000.0

LANES = 128
CHUNK = 128
SUB = 16
VMEM_LIMIT = 56 * 1024 * 1024


def _cparams(n_axes):
    return pltpu.CompilerParams(dimension_semantics=("arbitrary",) * n_axes,
                                vmem_limit_bytes=VMEM_LIMIT)


def _bdot(a, b):
    return jnp.dot(a.astype(BF16), b.astype(BF16), preferred_element_type=F32)


def _bdot_nt(a, b):
    return lax.dot_general(a.astype(BF16), b.astype(BF16), (((1,), (1,)), ((), ())),
                           preferred_element_type=F32)


def _split3(x):
    hi = x.astype(BF16)
    r1 = x - hi.astype(F32)
    mid = r1.astype(BF16)
    lo = (r1 - mid.astype(F32)).astype(BF16)
    return hi, mid, lo


def _dot_sel(sel, parts):
    return (jnp.dot(sel, parts[0], preferred_element_type=F32)
            + jnp.dot(sel, parts[1], preferred_element_type=F32)
            + jnp.dot(sel, parts[2], preferred_element_type=F32))


def _sigmoid(x):
    return 1.0 / (1.0 + jnp.exp(-x))


def _silu(x):
    return x * _sigmoid(x)


def _softplus(x):
    return jnp.maximum(x, 0.0) + jnp.log(1.0 + jnp.exp(-jnp.abs(x)))


def _iota(shape, dim):
    return lax.broadcasted_iota(jnp.int32, shape, dim)


def _const_spec(shape):
    return pl.BlockSpec(shape, lambda *_: (0,) * len(shape))


def _layer_spec(shape, layer):
    return pl.BlockSpec((None,) + shape, lambda *_: (layer,) + (0,) * len(shape))


def _mod_kernel(c_ref, w_ref, b_ref, o_ref):
    o_ref[...] = _bdot(_silu(c_ref[...]), w_ref[...]) + b_ref[...]


def _modulation(c_all, w_mod, b_mod):
    depth, d, n = w_mod.shape
    tn = 1024
    return pl.pallas_call(
        _mod_kernel,
        grid=(depth, n // tn),
        in_specs=[pl.BlockSpec((8, d), lambda l, j: (0, 0)),
                  pl.BlockSpec((None, d, tn), lambda l, j: (l, 0, j)),
                  pl.BlockSpec((None, 1, tn), lambda l, j: (l, 0, j))],
        out_specs=pl.BlockSpec((None, 8, tn), lambda l, j: (l, 0, j)),
        out_shape=jax.ShapeDtypeStruct((depth, 8, n), F32),
        compiler_params=_cparams(2),
        name="modulation",
    )(c_all, w_mod, b_mod.reshape(depth, 1, n))


def _ffn_kernel(*refs, j, n_f, n_split):
    n_x = 1 if n_split is None else 2
    x_refs = refs[:n_x]
    mod_ref, gpre_ref, gpost_ref, w1_ref, w3_ref, w2_ref, o_ref, h_scr, acc_scr = refs[n_x:]
    i = pl.program_id(0)
    f = pl.program_id(1)

    def prologue(x_ref):
        x = x_ref[...]
        y = x * lax.rsqrt(jnp.mean(x * x, axis=-1, keepdims=True) + EPS) * gpre_ref[...]
        h = y * (1.0 + mod_ref[j + 1:j + 2, :]) + mod_ref[j:j + 1, :]
        h_scr[...] = h.astype(BF16)
        acc_scr[...] = jnp.zeros_like(acc_scr)

    def epilogue(x_ref):
        y = acc_scr[...]
        yn = y * lax.rsqrt(jnp.mean(y * y, axis=-1, keepdims=True) + EPS) * gpost_ref[...]
        o_ref[...] = x_ref[...] + 0.5 * mod_ref[j + 2:j + 3, :] * yn

    def on_rows(pred, fn):
        if n_split is None:
            pl.when(pred)(lambda: fn(x_refs[0]))
        else:
            pl.when(jnp.logical_and(pred, i < n_split))(lambda: fn(x_refs[0]))
            pl.when(jnp.logical_and(pred, i >= n_split))(lambda: fn(x_refs[1]))

    on_rows(f == 0, prologue)
    h = h_scr[...]
    a = jnp.dot(h, w1_ref[...], preferred_element_type=F32)
    b = jnp.dot(h, w3_ref[...], preferred_element_type=F32)
    g = (_silu(a) * b).astype(BF16)
    acc_scr[...] += jnp.dot(g, w2_ref[...], preferred_element_type=F32)
    on_rows(f == n_f - 1, epilogue)


def _ffn(xs, mods, gpre, gpost, w1, w3, w2, *, layer, k, j, n_rows, lat_rows, seq, n_batch):
    d = xs[0].shape[1]
    dff = w1.shape[-1]
    tm, tf = 512, 512
    n_f = dff // tf
    norm_row = layer * 3 + (0 if j == 0 else 2)

    def mod_idx(i, f):
        return (layer, jnp.where(i * tm < lat_rows, (i * tm) // seq, n_batch), 0, 0)

    if len(xs) == 1:
        n_split = None
        x_specs = [pl.BlockSpec((tm, d), lambda i, f: (i, 0))]
    else:
        n_split = xs[0].shape[0] // tm
        x_specs = [pl.BlockSpec((tm, d), lambda i, f: (jnp.minimum(i, n_split - 1), 0)),
                   pl.BlockSpec((tm, d), lambda i, f: (jnp.maximum(i - n_split, 0), 0))]

    return pl.pallas_call(
        functools.partial(_ffn_kernel, j=j, n_f=n_f, n_split=n_split),
        grid=(n_rows // tm, n_f),
        in_specs=x_specs + [
            pl.BlockSpec((None, None, N_MOD, d), mod_idx),
            pl.BlockSpec((None, 1, d), lambda i, f: (norm_row, 0, 0)),
            pl.BlockSpec((None, 1, d), lambda i, f: (norm_row, 0, 0)),
            pl.BlockSpec((None, None, d, tf), lambda i, f: (layer, k, 0, f)),
            pl.BlockSpec((None, None, d, tf), lambda i, f: (layer, k, 0, f)),
            pl.BlockSpec((None, None, tf, d), lambda i, f: (layer, k, f, 0))],
        out_specs=pl.BlockSpec((tm, d), lambda i, f: (i, 0)),
        out_shape=jax.ShapeDtypeStruct((n_rows, d), F32),
        scratch_shapes=[pltpu.VMEM((tm, d), BF16), pltpu.VMEM((tm, d), F32)],
        compiler_params=_cparams(2),
        name="ffn",
    )(*xs, mods, gpre, gpost, w1, w3, w2)


def _inproj_kernel(x_ref, mod_ref, gpre_ref, wa_hbm, wb_hbm, wc_hbm, wd_hbm, oa_ref, ob_ref, oc_ref, od_ref,
                   wa_scr, wb_scr, wc_scr, wd_scr, sem, *, layer):
    w_hbm = (wa_hbm, wb_hbm, wc_hbm, wd_hbm)
    w_scr = (wa_scr, wb_scr, wc_scr, wd_scr)

    @pl.when(pl.program_id(0) == 0)
    def _():
        copies = [pltpu.make_async_copy(w_hbm[m].at[layer], w_scr[m], sem.at[m]) for m in range(4)]
        for cp in copies:
            cp.start()
        for cp in copies:
            cp.wait()

    x = x_ref[...]
    y = x * lax.rsqrt(jnp.mean(x * x, axis=-1, keepdims=True) + EPS) * gpre_ref[...]
    h = (y * (1.0 + mod_ref[4:5, :]) + mod_ref[3:4, :]).astype(BF16)
    for w, o_ref in zip(w_scr, (oa_ref, ob_ref, oc_ref, od_ref)):
        o_ref[...] = jnp.dot(h, w[...], preferred_element_type=F32)


def _x_to_b_block(i, tm, lat_rows, seq, ctx_len):
    per_b = (seq + ctx_len) // tm
    lat_b = seq // tm
    ctx_b = ctx_len // tm
    n_lat = lat_rows // tm
    lat_idx = (i // lat_b) * per_b + ctx_b + i % lat_b
    ic = i - n_lat
    ctx_idx = (ic // ctx_b) * per_b + ic % ctx_b
    return jnp.where(i < n_lat, lat_idx, ctx_idx)


def _inproj(x, mods, gpre, ws, *, layer, lat_rows, seq, ctx_len, n_batch):
    rows, d = x.shape
    tm = 256
    widths = [w.shape[-1] for w in ws]

    def mod_idx(i):
        return (layer, jnp.where(i * tm < lat_rows, (i * tm) // seq, n_batch), 0, 0)

    def out_idx(i):
        return (_x_to_b_block(i, tm, lat_rows, seq, ctx_len), 0)

    return pl.pallas_call(
        functools.partial(_inproj_kernel, layer=layer),
        grid=(rows // tm,),
        in_specs=[pl.BlockSpec((tm, d), lambda i: (i, 0)),
                  pl.BlockSpec((None, None, N_MOD, d), mod_idx),
                  pl.BlockSpec((None, 1, d), lambda i: (layer * 3 + 1, 0, 0))]
        + [pl.BlockSpec(memory_space=pl.ANY)] * 4,
        out_specs=[pl.BlockSpec((tm, n), out_idx) for n in widths],
        out_shape=[jax.ShapeDtypeStruct((rows, n), F32) for n in widths],
        scratch_shapes=[pltpu.VMEM((d, n), BF16) for n in widths] + [pltpu.SemaphoreType.DMA((4,))],
        compiler_params=_cparams(1),
        name="inproj",
    )(x, mods, gpre, *ws)


def _outproj_kernel(x_ref, ya_ref, yb_ref, yc_ref, yd_ref, w_ref, mod_ref, gpost_ref, o_ref):
    w = LRU_WIDTH
    y = (jnp.dot(ya_ref[...], w_ref[0 * w:1 * w, :], preferred_element_type=F32)
         + jnp.dot(yb_ref[...], w_ref[1 * w:2 * w, :], preferred_element_type=F32)
         + jnp.dot(yc_ref[...], w_ref[2 * w:3 * w, :], preferred_element_type=F32)
         + jnp.dot(yd_ref[...], w_ref[3 * w:4 * w, :], preferred_element_type=F32))
    yn = y * lax.rsqrt(jnp.mean(y * y, axis=-1, keepdims=True) + EPS) * gpost_ref[...]
    o_ref[...] = x_ref[...] + mod_ref[5:6, :] * yn


def _outproj(x, ys, w, mods, gpost, *, layer, n_rows, lat_rows, seq, ctx_len, n_batch):
    d = x.shape[1]
    tm = 256
    wy = ys[0].shape[1]

    def mod_idx(i):
        return (layer, jnp.where(i * tm < lat_rows, (i * tm) // seq, n_batch), 0, 0)

    def y_idx(i):
        return (_x_to_b_block(i, tm, lat_rows, seq, ctx_len), 0)

    return pl.pallas_call(
        _outproj_kernel,
        grid=(n_rows // tm,),
        in_specs=[pl.BlockSpec((tm, d), lambda i: (i, 0))]
        + [pl.BlockSpec((tm, wy), y_idx)] * 4
        + [_layer_spec((4 * wy, d), layer),
           pl.BlockSpec((None, None, N_MOD, d), mod_idx),
           pl.BlockSpec((None, 1, d), lambda i: (layer * 3 + 1, 0, 0))],
        out_specs=pl.BlockSpec((tm, d), lambda i: (i, 0)),
        out_shape=jax.ShapeDtypeStruct((n_rows, d), F32),
        compiler_params=_cparams(1),
        name="outproj",
    )(x, *ys, w, mods, gpost)


def _conv_chunk(u_ref, col0, ncols, base, first, last, t_tot, w_ref, b_ref):
    cols = slice(col0, col0 + ncols)
    prev = u_ref[pl.ds(pl.multiple_of(jnp.maximum(base - 8, 0), 8), 8), cols]
    nxt = u_ref[pl.ds(pl.multiple_of(jnp.minimum(base + CHUNK, t_tot - 8), 8), 8), cols]
    cur = u_ref[pl.ds(base, CHUNK), cols]
    prev = jnp.where(first, 0.0, prev)
    nxt = jnp.where(last, 0.0, nxt)
    win = jnp.concatenate([prev, cur, nxt], axis=0)
    n = CHUNK + 16
    xm1 = pltpu.roll(win, 1, axis=0)[8:8 + CHUNK]
    xp1 = pltpu.roll(win, n - 1, axis=0)[8:8 + CHUNK]
    xp2 = pltpu.roll(win, n - 2, axis=0)[8:8 + CHUNK]
    return (w_ref[0:1, :] * xm1 + w_ref[1:2, :] * cur + w_ref[2:3, :] * xp1 + w_ref[3:4, :] * xp2
            + b_ref[...])


def _chunk_order(i, nc_c, nc, reverse):
    if not reverse:
        return i
    return jnp.where(i < nc_c, nc_c - 1 - i, nc + nc_c - 1 - i)


def _stream_edges(c, nc_c, nc):
    first = jnp.logical_or(c == 0, c == nc_c)
    last = jnp.logical_or(c == nc_c - 1, c == nc - 1)
    return first, last


def _ssd_kernel(u_ref, cw_ref, cb_ref, dtb_ref, alog_ref, dskip_ref, nw_ref, o_ref,
                act_scr, dl_scr, y_scr, s_scr, *, nc_c, nc):
    t_tot = nc * CHUNK
    xbc0 = SSD_WIDTH
    dt0 = SSD_WIDTH + SSD_XBC

    def prep(c, _):
        base = pl.multiple_of(c * CHUNK, CHUNK)
        first, last = _stream_edges(c, nc_c, nc)
        conv = _conv_chunk(u_ref, xbc0, SSD_XBC, base, first, last, t_tot, cw_ref, cb_ref)
        act_scr[pl.ds(base, CHUNK), :] = _silu(conv)
        dt = u_ref[pl.ds(base, CHUNK), dt0:dt0 + LANES]
        dl_scr[pl.ds(base, CHUNK), :] = _softplus(dt + dtb_ref[...])
        return 0

    lax.fori_loop(0, nc, prep, 0)

    a_neg = -jnp.exp(alog_ref[...])
    row = _iota((CHUNK, CHUNK), 0)
    col = _iota((CHUNK, CHUNK), 1)
    lane_lo = _iota((CHUNK, LANES), 1) < SSD_HEADDIM
    grp_mask = [(_iota((CHUNK, LANES), 1) // SSD_STATE) == g for g in range(SSD_GROUPS)]

    for d in range(2):
        tri = (row >= col) if d == 0 else (row <= col)
        tri_b = tri.astype(BF16)
        edge = CHUNK - 1 if d == 0 else 0
        s_scr[...] = jnp.zeros_like(s_scr)

        def body(i, _, d=d, tri=tri, tri_b=tri_b, edge=edge):
            c = _chunk_order(i, nc_c, nc, d == 1)
            base = pl.multiple_of(c * CHUNK, CHUNK)
            act = act_scr[pl.ds(base, CHUNK), :]
            bm = act[:, SSD_WIDTH:SSD_WIDTH + LANES]
            cm = act[:, SSD_WIDTH + LANES:SSD_WIDTH + 2 * LANES]
            delta = dl_scr[pl.ds(base, CHUNK), :]
            la = delta * a_neg
            cum_col = _dot_sel(tri_b, _split3(la))
            cum_row = cum_col.T
            delta_row = delta.T
            bt = bm.T
            gmat = [_bdot(jnp.where(grp_mask[g], cm, 0.0), bt) for g in range(SSD_GROUPS)]
            for k in range(SSD_HEADS // 2):
                g = (2 * k) // (SSD_HEADS // SSD_GROUPS)
                x_tile = act[:, k * LANES:(k + 1) * LANES]
                cg = jnp.where(grp_mask[g], cm, 0.0)
                m_list, w_list, e_list, tot_list = [], [], [], []
                for h in (2 * k, 2 * k + 1):
                    j = d * SSD_HEADS + h
                    ccol = cum_col[:, j:j + 1]
                    crow = cum_row[j:j + 1, :]
                    drow = delta_row[j:j + 1, :]
                    dec = jnp.where(tri, jnp.exp(jnp.where(tri, ccol - crow, 0.0)), 0.0)
                    m_list.append(gmat[g] * dec * drow)
                    tot = cum_row[j:j + 1, edge:edge + 1]
                    w_list.append(bt * (drow * jnp.exp(tot - crow)))
                    e_list.append(jnp.exp(ccol))
                    tot_list.append(jnp.exp(tot))
                y2 = _bdot(jnp.concatenate(m_list, axis=0), x_tile)
                u2 = _bdot(jnp.concatenate(w_list, axis=0), x_tile)
                s_old = s_scr[:, k * LANES:(k + 1) * LANES]
                y_inter = _bdot(cg, s_old) * jnp.where(lane_lo, e_list[0], e_list[1])
                y_tile = jnp.where(lane_lo, y2[:CHUNK], y2[CHUNK:]) + y_inter
                s_scr[:, k * LANES:(k + 1) * LANES] = (
                    s_old * jnp.where(lane_lo, tot_list[0], tot_list[1])
                    + jnp.where(lane_lo, u2[:CHUNK], u2[CHUNK:]))
                if d == 0:
                    y_scr[pl.ds(base, CHUNK), k * LANES:(k + 1) * LANES] = (
                        y_tile + x_tile * dskip_ref[:, k * LANES:(k + 1) * LANES])
                else:
                    y_scr[pl.ds(base, CHUNK), k * LANES:(k + 1) * LANES] += y_tile
            return 0

        lax.fori_loop(0, nc, body, 0)

    def fin(c, _):
        base = pl.multiple_of(c * CHUNK, CHUNK)
        z = u_ref[pl.ds(base, CHUNK), 0:SSD_WIDTH]
        y = y_scr[pl.ds(base, CHUNK), :] * _silu(z)
        yn = y * lax.rsqrt(jnp.mean(y * y, axis=-1, keepdims=True) + EPS) * nw_ref[...]
        o_ref[pl.ds(base, CHUNK), :] = yn.astype(o_ref.dtype)
        return 0

    lax.fori_loop(0, nc, fin, 0)


def _ssd(u, conv_w, conv_b, dt_bias, a_log, d_skip, norm_w, *, layer, n_batch, t_ctx, t_lat):
    t_tot = t_ctx + t_lat
    nc_c, nc = t_ctx // CHUNK, t_tot // CHUNK
    return pl.pallas_call(
        functools.partial(_ssd_kernel, nc_c=nc_c, nc=nc),
        grid=(n_batch,),
        in_specs=[pl.BlockSpec((t_tot, SSD_PAD), lambda b: (b, 0)),
                  _layer_spec((CONV_W, SSD_XBC), layer), _layer_spec((1, SSD_XBC), layer),
                  _layer_spec((1, LANES), layer), _layer_spec((1, LANES), layer),
                  _layer_spec((1, SSD_WIDTH), layer), _layer_spec((1, SSD_WIDTH), layer)],
        out_specs=pl.BlockSpec((t_tot, SSD_WIDTH), lambda b: (b, 0)),
        out_shape=jax.ShapeDtypeStruct((n_batch * t_tot, SSD_WIDTH), BF16),
        scratch_shapes=[pltpu.VMEM((t_tot, SSD_XBC), F32), pltpu.VMEM((t_tot, LANES), F32),
                        pltpu.VMEM((t_tot, SSD_WIDTH), F32), pltpu.VMEM((CHUNK, SSD_WIDTH), F32)],
        compiler_params=_cparams(1),
        name="ssd",
    )(u, conv_w, conv_b, dt_bias, a_log, d_skip, norm_w)


def _lru_kernel(u_ref, cw_ref, cb_ref, wg_ref, bg_ref, lam_ref, o_ref, xc_scr, h_scr, *, nc_c, nc):
    t_tot = nc * CHUNK
    w = LRU_WIDTH

    def prep(c, _):
        base = pl.multiple_of(c * CHUNK, CHUNK)
        first, last = _stream_edges(c, nc_c, nc)
        xc_scr[pl.ds(base, CHUNK), :] = _conv_chunk(u_ref, 0, w, base, first, last, t_tot, cw_ref, cb_ref)
        return 0

    lax.fori_loop(0, nc, prep, 0)

    sub = _iota((CHUNK, w), 0) % 8
    n_tiles = CHUNK // 8

    for d in range(2):
        coef = -LRU_C * _softplus(-lam_ref[d:d + 1, :])

        def body(i, h_prev, d=d, coef=coef):
            c = _chunk_order(i, nc_c, nc, d == 1)
            base = pl.multiple_of(c * CHUNK, CHUNK)
            xc = xc_scr[pl.ds(base, CHUNK), :]
            gates = _bdot(xc, wg_ref[d]) + bg_ref[d:d + 1, :]
            r = _sigmoid(gates[:, :w])
            ig = _sigmoid(gates[:, w:])
            log_a = r * coef
            a = jnp.exp(log_a)
            b = jnp.sqrt(jnp.maximum(1.0 - jnp.exp(2.0 * log_a), 1e-12)) * (ig * xc)
            for k in (1, 2, 4):
                if d == 0:
                    keep = sub >= k
                    shift = k
                else:
                    keep = sub < 8 - k
                    shift = CHUNK - k
                a_s = jnp.where(keep, pltpu.roll(a, shift, axis=0), 1.0)
                b_s = jnp.where(keep, pltpu.roll(b, shift, axis=0), 0.0)
                b = a * b_s + b
                a = a * a_s
            tiles = range(n_tiles) if d == 0 else range(n_tiles - 1, -1, -1)
            outs = [None] * n_tiles
            for t in tiles:
                h_t = a[t * 8:(t + 1) * 8] * h_prev + b[t * 8:(t + 1) * 8]
                outs[t] = h_t
                h_prev = h_t[7:8] if d == 0 else h_t[0:1]
            h = jnp.concatenate(outs, axis=0)
            if d == 0:
                h_scr[pl.ds(base, CHUNK), :] = h
            else:
                gate = u_ref[pl.ds(base, CHUNK), w:2 * w]
                out = (h_scr[pl.ds(base, CHUNK), :] + h) * jax.nn.gelu(gate)
                o_ref[pl.ds(base, CHUNK), :] = out.astype(o_ref.dtype)
            return h_prev

        lax.fori_loop(0, nc, body, jnp.zeros((1, w), F32))


def _block_diag(wb):
    nb, bi, bj = wb.shape[-3:]
    eye = jnp.eye(nb, dtype=wb.dtype)
    return jnp.einsum('...hij,hg->...higj', wb, eye).reshape(wb.shape[:-3] + (nb * bi, nb * bj))


def _lru(u, conv_w, conv_b, wg, bg, lam, *, layer, n_batch, t_ctx, t_lat):
    t_tot = t_ctx + t_lat
    nc_c, nc = t_ctx // CHUNK, t_tot // CHUNK
    w = LRU_WIDTH
    return pl.pallas_call(
        functools.partial(_lru_kernel, nc_c=nc_c, nc=nc),
        grid=(n_batch,),
        in_specs=[pl.BlockSpec((t_tot, LRU_COLS), lambda b: (b, 0)),
                  _layer_spec((CONV_W, w), layer), _layer_spec((1, w), layer),
                  _layer_spec((2, w, 2 * w), layer), _layer_spec((2, 2 * w), layer), _layer_spec((2, w), layer)],
        out_specs=pl.BlockSpec((t_tot, w), lambda b: (b, 0)),
        out_shape=jax.ShapeDtypeStruct((n_batch * t_tot, w), BF16),
        scratch_shapes=[pltpu.VMEM((t_tot, w), F32), pltpu.VMEM((t_tot, w), F32)],
        compiler_params=_cparams(1),
        name="rglru",
    )(u, conv_w, conv_b, wg, bg, lam)


def _hgrn_kernel(q_ref, ff_ref, fb_ref, i_ref, g_ref, lbl_ref, nw_ref, esel_ref, o_ref, of_scr, ob_scr, st_scr, *,
                 layer, nc_c, nc, n_heads):
    n_sub = CHUNK // SUB
    blk_r = _iota((CHUNK, CHUNK), 0) // SUB
    blk_c = _iota((CHUNK, CHUNK), 1) // SUB
    row = _iota((CHUNK, CHUNK), 0)
    col = _iota((CHUNK, CHUNK), 1)
    same = blk_r == blk_c
    ones_blk = same.astype(BF16)
    scale = HGRN_HEADDIM ** -0.5
    log2e = 1.4426950408889634

    lbs = []
    for d in range(2):
        logits = lbl_ref[d]
        ex = jnp.exp(logits - jnp.max(logits, axis=0, keepdims=True))
        sm = ex / jnp.sum(ex, axis=0, keepdims=True)
        lbs.append(jnp.sum(sm[0:layer + 1], axis=0, keepdims=True) - sm[0:1])
    valids = [jnp.logical_and(same, row >= col), jnp.logical_and(same, row <= col)]
    tri_blks = [m.astype(BF16) for m in valids]
    st_scr[...] = jnp.zeros_like(st_scr)

    def chunk(d, hh, base):
        cols = slice(hh * LANES, (hh + 1) * LANES)
        lb = lbs[d][:, cols]
        f_ref = ff_ref if d == 0 else fb_ref
        qh = _silu(q_ref[pl.ds(base, CHUNK), cols]) * scale
        f = lb + (1.0 - lb) * _sigmoid(f_ref[pl.ds(base, CHUNK), cols])
        kk = 1.0 - f
        gl = jnp.log(f) * log2e
        v = i_ref[pl.ds(base, CHUNK), cols]
        parts = _split3(gl)
        cum = _dot_sel(tri_blks[d], parts)
        tot = _dot_sel(ones_blk, parts)
        halves = []
        for r0 in range(0, CHUNK, CHUNK // 2):
            rows = slice(r0, r0 + CHUNK // 2)
            q3 = qh[rows].reshape(n_sub // 2, SUB, LANES)
            c3 = cum[rows].reshape(n_sub // 2, SUB, LANES)
            k3 = kk[rows].reshape(n_sub // 2, SUB, LANES)
            acc = None
            for s0 in range(0, SUB, 4):
                ps = []
                for s in range(s0, s0 + 4):
                    e = jnp.exp2(jnp.minimum(c3 - c3[:, s:s + 1, :], 0.0))
                    p = (q3 * e) * k3[:, s:s + 1, :]
                    ps.append(p.reshape(CHUNK // 2, LANES).astype(BF16))
                part = jnp.dot(jnp.concatenate(ps, axis=1), esel_ref[s0 * LANES:(s0 + 4) * LANES, :],
                               preferred_element_type=F32)
                acc = part if acc is None else acc + part
            halves.append(acc)
        scores = jnp.concatenate(halves, axis=0)
        y = _bdot(jnp.where(valids[d], scores, 0.0), v)
        q_in = qh * jnp.exp2(cum)
        k_out = kk * jnp.exp2(tot - cum)
        v_t = v.T
        upd = _bdot(jnp.concatenate([jnp.where(blk_c == jb, v_t, 0.0) for jb in range(n_sub)], axis=0),
                    k_out)
        st = st_scr[d, hh]
        outs = [None] * n_sub
        order = range(n_sub) if d == 0 else range(n_sub - 1, -1, -1)
        for jb in order:
            r0 = jb * SUB
            outs[jb] = _bdot_nt(q_in[r0:r0 + SUB], st)
            st = st * jnp.exp2(tot[r0:r0 + 1, :]) + upd[jb * HGRN_HEADDIM:(jb + 1) * HGRN_HEADDIM]
        st_scr[d, hh] = st
        return y + jnp.concatenate(outs, axis=0)

    def body(i, _):
        for d in range(2):
            c = _chunk_order(i, nc_c, nc, d == 1)
            base = pl.multiple_of(c * CHUNK, CHUNK)
            o_scr = of_scr if d == 0 else ob_scr
            for hh in range(n_heads):
                o_scr[pl.ds(base, CHUNK), hh * LANES:(hh + 1) * LANES] = chunk(d, hh, base)
        return 0

    lax.fori_loop(0, nc, body, 0)

    def fin(c, _):
        base = pl.multiple_of(c * CHUNK, CHUNK)
        for hh in range(n_heads):
            cols = slice(hh * LANES, (hh + 1) * LANES)
            o = of_scr[pl.ds(base, CHUNK), cols] + ob_scr[pl.ds(base, CHUNK), cols]
            on = o * lax.rsqrt(jnp.mean(o * o, axis=-1, keepdims=True) + EPS) * nw_ref[...]
            out = on * _silu(g_ref[pl.ds(base, CHUNK), cols])
            o_ref[pl.ds(base, CHUNK), cols] = out.astype(o_ref.dtype)
        return 0

    lax.fori_loop(0, nc, fin, 0)


def _hgrn_selector():
    s_of_row = np.arange(SUB * LANES) // LANES
    return jnp.asarray(s_of_row[:, None] == (np.arange(LANES)[None, :] % SUB), dtype=BF16)


def _hgrn(u, lb_logits, norm_w, *, layer, n_batch, t_ctx, t_lat):
    t_tot = t_ctx + t_lat
    nc_c, nc = t_ctx // CHUNK, t_tot // CHUNK
    n_heads = 2
    ng = HGRN_HEADS // n_heads
    wblk = n_heads * LANES
    depth = lb_logits.shape[1]
    col = lambda part: pl.BlockSpec((t_tot, wblk), lambda b, h, part=part: (b, part * ng + h))
    return pl.pallas_call(
        functools.partial(_hgrn_kernel, layer=layer, nc_c=nc_c, nc=nc, n_heads=n_heads),
        grid=(n_batch, ng),
        in_specs=[col(0), col(1), col(2), col(3), col(4),
                  pl.BlockSpec((2, depth, wblk), lambda b, h: (0, 0, h)),
                  _layer_spec((1, LANES), layer),
                  _const_spec((SUB * LANES, LANES))],
        out_specs=pl.BlockSpec((t_tot, wblk), lambda b, h: (b, h)),
        out_shape=jax.ShapeDtypeStruct((n_batch * t_tot, HGRN_WIDTH), BF16),
        scratch_shapes=[pltpu.VMEM((t_tot, wblk), F32), pltpu.VMEM((t_tot, wblk), F32),
                        pltpu.VMEM((2, n_heads, HGRN_HEADDIM, HGRN_HEADDIM), F32)],
        compiler_params=_cparams(2),
        name="hgrn2",
    )(u, u, u, u, u, lb_logits, norm_w, _hgrn_selector())


def _ret_consts(t_ctx, t_lat):
    gam = 1.0 - np.exp2(-5.0 - np.arange(RET_HEADS, dtype=np.float64))
    t = np.arange(CHUNK, dtype=np.float64)
    diff = np.abs(t[:, None] - t[None, :])
    dsym = gam[:, None, None] ** diff[None]
    dsym[:, np.arange(CHUNK), np.arange(CHUNK)] = 2.0
    head_of_lane = np.repeat(np.arange(RET_HEADS), RET_QK_DIM)
    qf = gam[head_of_lane][None, :] ** (t[:, None] + 1.0)
    qb = gam[head_of_lane][None, :] ** (CHUNK - t[:, None])
    kf = gam[:, None] ** (CHUNK - 1.0 - t[None, :])
    kb = gam[:, None] ** t[None, :]
    gtot = gam ** CHUNK
    pos = np.arange(t_lat)
    rows_p, cols_p = pos // GRID_W, pos % GRID_W
    quarter = RET_QK_DIM // 4
    inv = ROPE_BASE ** (-np.arange(quarter, dtype=np.float64) / quarter)
    ang_r = rows_p[:, None] * inv[None, :]
    ang_c = cols_p[:, None] * inv[None, :]
    cos_h = np.concatenate([np.cos(ang_r), np.cos(ang_r), np.cos(ang_c), np.cos(ang_c)], axis=1)
    sin_h = np.concatenate([-np.sin(ang_r), np.sin(ang_r), -np.sin(ang_c), np.sin(ang_c)], axis=1)
    cos_t = np.concatenate([np.ones((t_ctx, RET_QK_DIM)), cos_h], axis=0)
    sin_t = np.concatenate([np.zeros((t_ctx, RET_QK_DIM)), sin_h], axis=0)
    cos_t = np.tile(cos_t, (1, RET_HEADS))
    sin_t = np.tile(sin_t, (1, RET_HEADS))
    f32 = lambda a: jnp.asarray(a, dtype=F32)
    return (f32(dsym), f32(qf), f32(qb), f32(np.concatenate([kf, kb], axis=0)), [float(g) for g in gtot],
            f32(cos_t), f32(sin_t))


def _ret_kernel(u_ref, dsym_ref, qf_ref, qb_ref, kfac_ref, cos_ref, sin_ref, o_ref, o_scr, s_scr, *,
                gtot, nc_c, nc):
    nqk = RET_HEADS * RET_QK_DIM
    lane = _iota((CHUNK, nqk), 1)
    low_half = (lane % (RET_QK_DIM // 2)) < (RET_QK_DIM // 4)
    tile_lane = _iota((CHUNK, LANES), 1)
    head_mask = [(tile_lane // RET_QK_DIM) == p for p in range(2)]
    kscale = RET_QK_DIM ** -0.5

    def rope(x, cos, sin):
        swapped = jnp.where(low_half, pltpu.roll(x, nqk - RET_QK_DIM // 4, axis=1),
                            pltpu.roll(x, RET_QK_DIM // 4, axis=1))
        return x * cos + swapped * sin

    for d in range(2):
        s_scr[...] = jnp.zeros_like(s_scr)
        qfac = qf_ref if d == 0 else qb_ref

        def body(i, _, d=d, qfac=qfac):
            c = _chunk_order(i, nc_c, nc, d == 1)
            base = pl.multiple_of(c * CHUNK, CHUNK)
            cos = cos_ref[pl.ds(base, CHUNK), :]
            sin = sin_ref[pl.ds(base, CHUNK), :]
            q = rope(u_ref[pl.ds(base, CHUNK), 0:nqk], cos, sin)
            k = rope(u_ref[pl.ds(base, CHUNK), nqk:2 * nqk], cos, sin) * kscale
            kt = k.T
            qd = q * qfac[...]
            for h in range(RET_HEADS):
                p = h // 2
                v = u_ref[pl.ds(base, CHUNK), 2 * nqk + h * RET_V_DIM:2 * nqk + (h + 1) * RET_V_DIM]
                hm = head_mask[h % 2]
                kt_p = kt[p * LANES:(p + 1) * LANES]
                s_old = s_scr[h]
                y = _bdot(jnp.where(hm, qd[:, p * LANES:(p + 1) * LANES], 0.0), s_old)
                kfac = kfac_ref[d * RET_HEADS + h:d * RET_HEADS + h + 1, :]
                s_scr[h] = gtot[h] * s_old + _bdot(kt_p * kfac, v)
                if d == 0:
                    raw = _bdot(jnp.where(hm, q[:, p * LANES:(p + 1) * LANES], 0.0), kt_p)
                    y = y + _bdot(raw * dsym_ref[h], v)
                    o_scr[pl.ds(base, CHUNK), h * RET_V_DIM:(h + 1) * RET_V_DIM] = y
                else:
                    o = o_scr[pl.ds(base, CHUNK), h * RET_V_DIM:(h + 1) * RET_V_DIM] + y
                    on = o * lax.rsqrt(jnp.mean(o * o, axis=-1, keepdims=True) + EPS)
                    gate = u_ref[pl.ds(base, CHUNK), 2 * nqk + RET_WIDTH + h * RET_V_DIM:
                                 2 * nqk + RET_WIDTH + (h + 1) * RET_V_DIM]
                    o_ref[pl.ds(base, CHUNK), h * RET_V_DIM:(h + 1) * RET_V_DIM] = (
                        _silu(gate) * on).astype(o_ref.dtype)
            return 0

        lax.fori_loop(0, nc, body, 0)


def _ret(u, *, n_batch, t_ctx, t_lat):
    t_tot = t_ctx + t_lat
    nc_c, nc = t_ctx // CHUNK, t_tot // CHUNK
    dsym, qf, qb, kfac, gtot, cos_t, sin_t = _ret_consts(t_ctx, t_lat)
    nqk = RET_HEADS * RET_QK_DIM
    return pl.pallas_call(
        functools.partial(_ret_kernel, gtot=gtot, nc_c=nc_c, nc=nc),
        grid=(n_batch,),
        in_specs=[pl.BlockSpec((t_tot, RET_COLS), lambda b: (b, 0)),
                  _const_spec((RET_HEADS, CHUNK, CHUNK)), _const_spec((CHUNK, nqk)), _const_spec((CHUNK, nqk)),
                  _const_spec((2 * RET_HEADS, CHUNK)), _const_spec((t_tot, nqk)), _const_spec((t_tot, nqk))],
        out_specs=pl.BlockSpec((t_tot, RET_WIDTH), lambda b: (b, 0)),
        out_shape=jax.ShapeDtypeStruct((n_batch * t_tot, RET_WIDTH), BF16),
        scratch_shapes=[pltpu.VMEM((t_tot, RET_WIDTH), F32), pltpu.VMEM((RET_HEADS, LANES, RET_V_DIM), F32)],
        compiler_params=_cparams(1),
        name="retention",
    )(u, dsym, qf, qb, kfac, cos_t, sin_t)


def kernel(x, c, ctx, c_ctx, w_mod, b_mod, norm_pre, norm_post, ffn_w1, ffn_w3, ffn_w2, w_in, w_out,
           ssd_conv_w, ssd_conv_b, ssd_dt_bias, ssd_a_log, ssd_d, ssd_norm_w,
           lru_conv_w, lru_conv_b, lru_wa, lru_ba, lru_wx, lru_bx, lru_lambda,
           hgrn_lb_logits, hgrn_norm_w):
    n_batch, seq, d = x.shape
    ctx_len = ctx.shape[1]
    depth = w_mod.shape[0]
    lat_rows = n_batch * seq
    all_rows = lat_rows + n_batch * ctx_len
    geom = dict(lat_rows=lat_rows, seq=seq, ctx_len=ctx_len, n_batch=n_batch)
    mix = dict(n_batch=n_batch, t_ctx=ctx_len, t_lat=seq)

    c_all = jnp.concatenate([c, c_ctx[None, :], jnp.zeros((8 - n_batch - 1, d), F32)], axis=0)
    mods = _modulation(c_all, w_mod, b_mod).reshape(depth, 8, N_MOD, d)
    w1, w3, w2 = ffn_w1.astype(BF16), ffn_w3.astype(BF16), ffn_w2.astype(BF16)
    s0, s1, s2 = SSD_COLS, SSD_COLS + LRU_COLS, SSD_COLS + LRU_COLS + HGRN_COLS
    w_in_parts = (jnp.pad(w_in[:, :, :s0].astype(BF16), ((0, 0), (0, 0), (0, SSD_PAD - SSD_COLS))),
                  w_in[:, :, s0:s1].astype(BF16), w_in[:, :, s1:s2].astype(BF16), w_in[:, :, s2:].astype(BF16))
    w_out_b = w_out.astype(BF16)
    gpre = norm_pre.reshape(depth * 3, 1, d)
    gpost = norm_post.reshape(depth * 3, 1, d)
    rowvec = lambda v: v.reshape(depth, 1, -1)
    lane_pad = lambda v: jnp.pad(v.reshape(depth, 1, -1), ((0, 0), (0, 0), (0, LANES - v[0].size)))
    ssd_params = (ssd_conv_w, rowvec(ssd_conv_b), lane_pad(ssd_dt_bias), lane_pad(ssd_a_log),
                  rowvec(jnp.repeat(ssd_d, SSD_HEADDIM, axis=-1)), rowvec(ssd_norm_w))
    lru_wg = jnp.concatenate([_block_diag(lru_wa), _block_diag(lru_wx)], axis=-1).astype(BF16)
    lru_bg = jnp.concatenate([lru_ba.reshape(depth, 2, -1), lru_bx.reshape(depth, 2, -1)], axis=-1)
    lru_params = (lru_conv_w, rowvec(lru_conv_b), lru_wg, lru_bg, lru_lambda)
    hgrn_nw = rowvec(hgrn_norm_w)

    xs = (x.reshape(lat_rows, d), ctx.reshape(n_batch * ctx_len, d))
    for l in range(depth):
        last = l == depth - 1
        xs = (_ffn(xs, mods, gpre, gpost, w1, w3, w2, layer=l, k=0, j=0, n_rows=all_rows,
                   lat_rows=lat_rows, seq=seq, n_batch=n_batch),)
        ua, ub, uc, ud = _inproj(xs[0], mods, gpre, w_in_parts, layer=l, **geom)
        ya = _ssd(ua, *ssd_params, layer=l, **mix)
        yb = _lru(ub, *lru_params, layer=l, **mix)
        yc = _hgrn(uc, hgrn_lb_logits, hgrn_nw, layer=l, **mix)
        yd = _ret(ud, **mix)
        n_rows = lat_rows if last else all_rows
        xs = (_outproj(xs[0], (ya, yb, yc, yd), w_out_b, mods, gpost, layer=l, n_rows=n_rows, **geom),)
        xs = (_ffn(xs, mods, gpre, gpost, w1, w3, w2, layer=l, k=1, j=6, n_rows=n_rows,
                   lat_rows=lat_rows, seq=seq, n_batch=n_batch),)
    return xs[0].reshape(n_batch, seq, d)
```

```python
import functools

import numpy as np
import jax
import jax.numpy as jnp
from jax import lax
from jax.experimental import pallas as pl
from jax.experimental.pallas import tpu as pltpu

F32 = jnp.float32
BF16 = jnp.bfloat16

N_MOD = 9
CONV_W = 4
EPS = 1e-6
GRID_W = 64

SSD_WIDTH = 512
SSD_HEADDIM = 64
SSD_HEADS = 8
SSD_GROUPS = 2
SSD_STATE = 64
SSD_XBC = SSD_WIDTH + 2 * SSD_GROUPS * SSD_STATE
SSD_COLS = SSD_WIDTH + SSD_XBC + 2 * SSD_HEADS
SSD_PAD = 1408
LRU_WIDTH = 512
LRU_C = 8.0
LRU_COLS = 2 * LRU_WIDTH
HGRN_WIDTH = 512
HGRN_HEADDIM = 128
HGRN_HEADS = 4
HGRN_COLS = 5 * HGRN_WIDTH
RET_WIDTH = 512
RET_V_DIM = 128
RET_HEADS = 4
RET_QK_DIM = 64
RET_COLS = 2 * RET_HEADS * RET_QK_DIM + 2 * RET_WIDTH
ROPE_BASE = 10000.0

LANES = 128
CHUNK = 128
SUB = 16
VMEM_LIMIT = 56 * 1024 * 1024


def _cparams(n_axes):
    return pltpu.CompilerParams(dimension_semantics=("arbitrary",) * n_axes,
                                vmem_limit_bytes=VMEM_LIMIT)


def _bdot(a, b):
    return jnp.dot(a.astype(BF16), b.astype(BF16), preferred_element_type=F32)


def _bdot_nt(a, b):
    return lax.dot_general(a.astype(BF16), b.astype(BF16), (((1,), (1,)), ((), ())),
                           preferred_element_type=F32)


def _split3(x):
    hi = x.astype(BF16)
    r1 = x - hi.astype(F32)
    mid = r1.astype(BF16)
    lo = (r1 - mid.astype(F32)).astype(BF16)
    return hi, mid, lo


def _dot_sel(sel, parts):
    return (jnp.dot(sel, parts[0], preferred_element_type=F32)
            + jnp.dot(sel, parts[1], preferred_element_type=F32)
            + jnp.dot(sel, parts[2], preferred_element_type=F32))


def _sigmoid(x):
    return 1.0 / (1.0 + jnp.exp(-x))


def _silu(x):
    return x * _sigmoid(x)


def _softplus(x):
    return jnp.maximum(x, 0.0) + jnp.log(1.0 + jnp.exp(-jnp.abs(x)))


def _iota(shape, dim):
    return lax.broadcasted_iota(jnp.int32, shape, dim)


def _const_spec(shape):
    return pl.BlockSpec(shape, lambda *_: (0,) * len(shape))


def _layer_spec(shape, layer):
    return pl.BlockSpec((None,) + shape, lambda *_: (layer,) + (0,) * len(shape))


def _mod_kernel(c_ref, w_ref, b_ref, o_ref):
    o_ref[...] = _bdot(_silu(c_ref[...]), w_ref[...]) + b_ref[...]


def _modulation(c_all, w_mod, b_mod):
    depth, d, n = w_mod.shape
    tn = 1024
    return pl.pallas_call(
        _mod_kernel,
        grid=(depth, n // tn),
        in_specs=[pl.BlockSpec((8, d), lambda l, j: (0, 0)),
                  pl.BlockSpec((None, d, tn), lambda l, j: (l, 0, j)),
                  pl.BlockSpec((None, 1, tn), lambda l, j: (l, 0, j))],
        out_specs=pl.BlockSpec((None, 8, tn), lambda l, j: (l, 0, j)),
        out_shape=jax.ShapeDtypeStruct((depth, 8, n), F32),
        compiler_params=_cparams(2),
        name="modulation",
    )(c_all, w_mod, b_mod.reshape(depth, 1, n))


def _mod_row_idx(i, tm, layer, lat_rows, seq, n_batch):
    return (layer, jnp.where(i * tm < lat_rows, (i * tm) // seq, n_batch), 0, 0)


def _mod_norm(x, g, shift, scale):
    y = x * lax.rsqrt(jnp.mean(x * x, axis=-1, keepdims=True) + EPS) * g
    return (y * (1.0 + scale) + shift).astype(BF16)


def _prenorm_kernel(xl_ref, xc_ref, mod_ref, g_ref, x_ref, h_ref, *, n_split):
    x = jnp.where(pl.program_id(0) < n_split, xl_ref[...], xc_ref[...])
    x_ref[...] = x
    h_ref[...] = _mod_norm(x, g_ref[...], mod_ref[0:1, :], mod_ref[1:2, :])


def _prenorm(xl, xc, mods, gpre, *, lat_rows, seq, n_batch):
    d = xl.shape[1]
    rows = xl.shape[0] + xc.shape[0]
    tm = 512
    n_split = xl.shape[0] // tm
    blk = pl.BlockSpec((tm, d), lambda i: (i, 0))
    return pl.pallas_call(
        functools.partial(_prenorm_kernel, n_split=n_split),
        grid=(rows // tm,),
        in_specs=[pl.BlockSpec((tm, d), lambda i: (jnp.minimum(i, n_split - 1), 0)),
                  pl.BlockSpec((tm, d), lambda i: (jnp.maximum(i - n_split, 0), 0)),
                  pl.BlockSpec((None, None, N_MOD, d), lambda i: _mod_row_idx(i, tm, 0, lat_rows, seq, n_batch)),
                  pl.BlockSpec((None, 1, d), lambda i: (0, 0, 0))],
        out_specs=[blk, blk],
        out_shape=[jax.ShapeDtypeStruct((rows, d), F32), jax.ShapeDtypeStruct((rows, d), BF16)],
        compiler_params=_cparams(1),
        name="prenorm",
    )(xl, xc, mods, gpre)


def _ffn_up_kernel(h_ref, w1_ref, w3_ref, g_ref, w1_scr, w3_scr, *, n_sub):
    @pl.when(pl.program_id(1) == 0)
    def _():
        w1_scr[...] = w1_ref[...].astype(BF16)
        w3_scr[...] = w3_ref[...].astype(BF16)

    rows = h_ref.shape[0] // n_sub
    for r in range(n_sub):
        h = h_ref[r * rows:(r + 1) * rows, :]
        a = jnp.dot(h, w1_scr[...], preferred_element_type=F32)
        b = jnp.dot(h, w3_scr[...], preferred_element_type=F32)
        g_ref[r * rows:(r + 1) * rows, :] = (_silu(a) * b).astype(BF16)


def _ffn_up(h, w1, w3, *, layer, k, n_rows):
    d = h.shape[1]
    dff = w1.shape[-1]
    tm, tf = 1024, 512
    w_spec = pl.BlockSpec((None, None, d, tf), lambda f, i: (layer, k, 0, f))
    return pl.pallas_call(
        functools.partial(_ffn_up_kernel, n_sub=2),
        grid=(dff // tf, n_rows // tm),
        in_specs=[pl.BlockSpec((tm, d), lambda f, i: (i, 0)), w_spec, w_spec],
        out_specs=pl.BlockSpec((tm, tf), lambda f, i: (i, f)),
        out_shape=jax.ShapeDtypeStruct((n_rows, dff), BF16),
        scratch_shapes=[pltpu.VMEM((d, tf), BF16), pltpu.VMEM((d, tf), BF16)],
        compiler_params=_cparams(2),
        name="ffn_up",
    )(h, w1, w3)


EPI_SUB = 2


def _residual_epilogue(rows, x_ref, y, gate, gpost, mod_ref, gnext_ref, nxt, o_ref, hn_ref):
    yn = y * lax.rsqrt(jnp.mean(y * y, axis=-1, keepdims=True) + EPS) * gpost
    xn = x_ref[rows, :] + gate * yn
    o_ref[rows, :] = xn
    if nxt is not None:
        hn_ref[rows, :] = _mod_norm(xn, gnext_ref[...], mod_ref[nxt:nxt + 1, :], mod_ref[nxt + 1:nxt + 2, :])


def _ffn_down_kernel(*refs, layer, k, j, nxt, nxt_mod_own):
    g_ref, x_ref, mod_ref, gpost_ref = refs[:4]
    pos = 4
    nmod_ref = mod_ref
    gnext_ref = hn_ref = None
    if nxt is not None:
        if not nxt_mod_own:
            nmod_ref = refs[pos]
            pos += 1
        gnext_ref = refs[pos]
        pos += 1
    w2_hbm, o_ref = refs[pos], refs[pos + 1]
    pos += 2
    if nxt is not None:
        hn_ref = refs[pos]
        pos += 1
    w2_scr, sem = refs[pos], refs[pos + 1]

    @pl.when(pl.program_id(0) == 0)
    def _():
        cp = pltpu.make_async_copy(w2_hbm.at[layer, k], w2_scr, sem)
        cp.start()
        cp.wait()

    n = x_ref.shape[0] // EPI_SUB
    for r in range(EPI_SUB):
        rows = slice(r * n, (r + 1) * n)
        y = jnp.dot(g_ref[rows, :], w2_scr[...], preferred_element_type=F32)
        _residual_epilogue(rows, x_ref, y, 0.5 * mod_ref[j + 2:j + 3, :], gpost_ref[...], nmod_ref, gnext_ref,
                           nxt, o_ref, hn_ref)


def _ffn_down(g, x, mods, gpost, gpre, w2, *, layer, k, j, nxt, nxt_layer, n_rows, lat_rows, seq, n_batch):
    d = x.shape[1]
    dff = g.shape[1]
    tm = 256
    norm_row = layer * 3 + (0 if j == 0 else 2)
    blk = pl.BlockSpec((tm, d), lambda i: (i, 0))
    mod_spec = lambda ly: pl.BlockSpec((None, None, N_MOD, d),
                                       lambda i: _mod_row_idx(i, tm, ly, lat_rows, seq, n_batch))
    in_specs = [pl.BlockSpec((tm, dff), lambda i: (i, 0)), blk, mod_spec(layer),
                pl.BlockSpec((None, 1, d), lambda i: (norm_row, 0, 0))]
    args = [g, x, mods, gpost]
    out_specs, out_shape = [blk], [jax.ShapeDtypeStruct((n_rows, d), F32)]
    nxt_mod_own = nxt_layer == layer
    if nxt is not None:
        if not nxt_mod_own:
            in_specs.append(mod_spec(nxt_layer))
            args.append(mods)
        in_specs.append(pl.BlockSpec((None, 1, d), lambda i: (nxt_layer * 3 + nxt // 3, 0, 0)))
        args.append(gpre)
        out_specs.append(blk)
        out_shape.append(jax.ShapeDtypeStruct((n_rows, d), BF16))
    in_specs.append(pl.BlockSpec(memory_space=pl.ANY))
    args.append(w2)
    return pl.pallas_call(
        functools.partial(_ffn_down_kernel, layer=layer, k=k, j=j, nxt=nxt, nxt_mod_own=nxt_mod_own),
        grid=(n_rows // tm,),
        in_specs=in_specs,
        out_specs=out_specs,
        out_shape=out_shape,
        scratch_shapes=[pltpu.VMEM((dff, d), BF16), pltpu.SemaphoreType.DMA(())],
        compiler_params=_cparams(1),
        name="ffn_down",
    )(*args)


def _inproj_kernel(h_ref, wa_hbm, wb_hbm, wc_hbm, wd_hbm, oa_ref, ob_ref, oc_ref, od_ref,
                   wa_scr, wb_scr, wc_scr, wd_scr, sem, *, layer):
    w_hbm = (wa_hbm, wb_hbm, wc_hbm, wd_hbm)
    w_scr = (wa_scr, wb_scr, wc_scr, wd_scr)

    @pl.when(pl.program_id(0) == 0)
    def _():
        copies = [pltpu.make_async_copy(w_hbm[m].at[layer], w_scr[m], sem.at[m]) for m in range(4)]
        for cp in copies:
            cp.start()
        for cp in copies:
            cp.wait()

    h = h_ref[...]
    for w, o_ref in zip(w_scr, (oa_ref, ob_ref, oc_ref, od_ref)):
        o_ref[...] = jnp.dot(h, w[...], preferred_element_type=F32)


def _x_to_b_block(i, tm, lat_rows, seq, ctx_len):
    per_b = (seq + ctx_len) // tm
    lat_b = seq // tm
    ctx_b = ctx_len // tm
    n_lat = lat_rows // tm
    lat_idx = (i // lat_b) * per_b + ctx_b + i % lat_b
    ic = i - n_lat
    ctx_idx = (ic // ctx_b) * per_b + ic % ctx_b
    return jnp.where(i < n_lat, lat_idx, ctx_idx)


def _inproj(h, ws, *, layer, lat_rows, seq, ctx_len, n_batch):
    rows, d = h.shape
    tm = 256
    widths = [w.shape[-1] for w in ws]

    def out_idx(i):
        return (_x_to_b_block(i, tm, lat_rows, seq, ctx_len), 0)

    return pl.pallas_call(
        functools.partial(_inproj_kernel, layer=layer),
        grid=(rows // tm,),
        in_specs=[pl.BlockSpec((tm, d), lambda i: (i, 0))] + [pl.BlockSpec(memory_space=pl.ANY)] * 4,
        out_specs=[pl.BlockSpec((tm, n), out_idx) for n in widths],
        out_shape=[jax.ShapeDtypeStruct((rows, n), F32) for n in widths],
        scratch_shapes=[pltpu.VMEM((d, n), BF16) for n in widths] + [pltpu.SemaphoreType.DMA((4,))],
        compiler_params=_cparams(1),
        name="inproj",
    )(h, *ws)


def _outproj_kernel(x_ref, ya_ref, yb_ref, yc_ref, yd_ref, w_ref, mod_ref, gpost_ref, gnext_ref, o_ref, hn_ref):
    w = LRU_WIDTH
    n = x_ref.shape[0] // EPI_SUB
    for r in range(EPI_SUB):
        rows = slice(r * n, (r + 1) * n)
        y = (jnp.dot(ya_ref[rows, :], w_ref[0 * w:1 * w, :], preferred_element_type=F32)
             + jnp.dot(yb_ref[rows, :], w_ref[1 * w:2 * w, :], preferred_element_type=F32)
             + jnp.dot(yc_ref[rows, :], w_ref[2 * w:3 * w, :], preferred_element_type=F32)
             + jnp.dot(yd_ref[rows, :], w_ref[3 * w:4 * w, :], preferred_element_type=F32))
        _residual_epilogue(rows, x_ref, y, mod_ref[5:6, :], gpost_ref[...], mod_ref, gnext_ref, 6, o_ref, hn_ref)


def _outproj(x, ys, w, mods, gpost, gpre, *, layer, n_rows, lat_rows, seq, ctx_len, n_batch):
    d = x.shape[1]
    tm = 256
    wy = ys[0].shape[1]
    blk = pl.BlockSpec((tm, d), lambda i: (i, 0))

    def y_idx(i):
        return (_x_to_b_block(i, tm, lat_rows, seq, ctx_len), 0)

    return pl.pallas_call(
        _outproj_kernel,
        grid=(n_rows // tm,),
        in_specs=[blk] + [pl.BlockSpec((tm, wy), y_idx)] * 4
        + [_layer_spec((4 * wy, d), layer),
           pl.BlockSpec((None, None, N_MOD, d), lambda i: _mod_row_idx(i, tm, layer, lat_rows, seq, n_batch)),
           pl.BlockSpec((None, 1, d), lambda i: (layer * 3 + 1, 0, 0)),
           pl.BlockSpec((None, 1, d), lambda i: (layer * 3 + 2, 0, 0))],
        out_specs=[blk, blk],
        out_shape=[jax.ShapeDtypeStruct((n_rows, d), F32), jax.ShapeDtypeStruct((n_rows, d), BF16)],
        compiler_params=_cparams(1),
        name="outproj",
    )(x, *ys, w, mods, gpost, gpre)


def _conv_chunk(u_ref, col0, ncols, base, first, last, t_tot, w_ref, b_ref):
    cols = slice(col0, col0 + ncols)
    prev = u_ref[pl.ds(pl.multiple_of(jnp.maximum(base - 8, 0), 8), 8), cols]
    nxt = u_ref[pl.ds(pl.multiple_of(jnp.minimum(base + CHUNK, t_tot - 8), 8), 8), cols]
    cur = u_ref[pl.ds(base, CHUNK), cols]
    prev = jnp.where(first, 0.0, prev)
    nxt = jnp.where(last, 0.0, nxt)
    win = jnp.concatenate([prev, cur, nxt], axis=0)
    n = CHUNK + 16
    xm1 = pltpu.roll(win, 1, axis=0)[8:8 + CHUNK]
    xp1 = pltpu.roll(win, n - 1, axis=0)[8:8 + CHUNK]
    xp2 = pltpu.roll(win, n - 2, axis=0)[8:8 + CHUNK]
    return (w_ref[0:1, :] * xm1 + w_ref[1:2, :] * cur + w_ref[2:3, :] * xp1 + w_ref[3:4, :] * xp2
            + b_ref[...])


def _chunk_order(i, nc_c, nc, reverse):
    if not reverse:
        return i
    return jnp.where(i < nc_c, nc_c - 1 - i, nc + nc_c - 1 - i)


def _stream_edges(c, nc_c, nc):
    first = jnp.logical_or(c == 0, c == nc_c)
    last = jnp.logical_or(c == nc_c - 1, c == nc - 1)
    return first, last


def _ssd_kernel(u_ref, cw_ref, cb_ref, dtb_ref, alog_ref, dskip_ref, nw_ref, o_ref,
                act_scr, dl_scr, y_scr, s_scr, *, nc_c, nc):
    t_tot = nc * CHUNK
    xbc0 = SSD_WIDTH
    dt0 = SSD_WIDTH + SSD_XBC

    def prep(c, _):
        base = pl.multiple_of(c * CHUNK, CHUNK)
        first, last = _stream_edges(c, nc_c, nc)
        conv = _conv_chunk(u_ref, xbc0, SSD_XBC, base, first, last, t_tot, cw_ref, cb_ref)
        act_scr[pl.ds(base, CHUNK), :] = _silu(conv)
        dt = u_ref[pl.ds(base, CHUNK), dt0:dt0 + LANES]
        dl_scr[pl.ds(base, CHUNK), :] = _softplus(dt + dtb_ref[...])
        return 0

    lax.fori_loop(0, nc, prep, 0)

    a_neg = -jnp.exp(alog_ref[...])
    row = _iota((CHUNK, CHUNK), 0)
    col = _iota((CHUNK, CHUNK), 1)
    lane_lo = _iota((CHUNK, LANES), 1) < SSD_HEADDIM
    grp_mask = [(_iota((CHUNK, LANES), 1) // SSD_STATE) == g for g in range(SSD_GROUPS)]

    for d in range(2):
        tri = (row >= col) if d == 0 else (row <= col)
        tri_b = tri.astype(BF16)
        edge = CHUNK - 1 if d == 0 else 0
        s_scr[...] = jnp.zeros_like(s_scr)

        def body(i, _, d=d, tri=tri, tri_b=tri_b, edge=edge):
            c = _chunk_order(i, nc_c, nc, d == 1)
            base = pl.multiple_of(c * CHUNK, CHUNK)
            act = act_scr[pl.ds(base, CHUNK), :]
            bm = act[:, SSD_WIDTH:SSD_WIDTH + LANES]
            cm = act[:, SSD_WIDTH + LANES:SSD_WIDTH + 2 * LANES]
            delta = dl_scr[pl.ds(base, CHUNK), :]
            la = delta * a_neg
            cum_col = _dot_sel(tri_b, _split3(la))
            cum_row = cum_col.T
            delta_row = delta.T
            bt = bm.T
            gmat = [_bdot(jnp.where(grp_mask[g], cm, 0.0), bt) for g in range(SSD_GROUPS)]
            for k in range(SSD_HEADS // 2):
                g = (2 * k) // (SSD_HEADS // SSD_GROUPS)
                x_tile = act[:, k * LANES:(k + 1) * LANES]
                cg = jnp.where(grp_mask[g], cm, 0.0)
                m_list, w_list, e_list, tot_list = [], [], [], []
                for h in (2 * k, 2 * k + 1):
                    j = d * SSD_HEADS + h
                    ccol = cum_col[:, j:j + 1]
                    crow = cum_row[j:j + 1, :]
                    drow = delta_row[j:j + 1, :]
                    dec = jnp.where(tri, jnp.exp(jnp.where(tri, ccol - crow, 0.0)), 0.0)
                    m_list.append(gmat[g] * dec * drow)
                    tot = cum_row[j:j + 1, edge:edge + 1]
                    w_list.append(bt * (drow * jnp.exp(tot - crow)))
                    e_list.append(jnp.exp(ccol))
                    tot_list.append(jnp.exp(tot))
                y2 = _bdot(jnp.concatenate(m_list, axis=0), x_tile)
                u2 = _bdot(jnp.concatenate(w_list, axis=0), x_tile)
                s_old = s_scr[:, k * LANES:(k + 1) * LANES]
                y_inter = _bdot(cg, s_old) * jnp.where(lane_lo, e_list[0], e_list[1])
                y_tile = jnp.where(lane_lo, y2[:CHUNK], y2[CHUNK:]) + y_inter
                s_scr[:, k * LANES:(k + 1) * LANES] = (
                    s_old * jnp.where(lane_lo, tot_list[0], tot_list[1])
                    + jnp.where(lane_lo, u2[:CHUNK], u2[CHUNK:]))
                if d == 0:
                    y_scr[pl.ds(base, CHUNK), k * LANES:(k + 1) * LANES] = (
                        y_tile + x_tile * dskip_ref[:, k * LANES:(k + 1) * LANES])
                else:
                    y_scr[pl.ds(base, CHUNK), k * LANES:(k + 1) * LANES] += y_tile
            return 0

        lax.fori_loop(0, nc, body, 0)

    def fin(c, _):
        base = pl.multiple_of(c * CHUNK, CHUNK)
        z = u_ref[pl.ds(base, CHUNK), 0:SSD_WIDTH]
        y = y_scr[pl.ds(base, CHUNK), :] * _silu(z)
        yn = y * lax.rsqrt(jnp.mean(y * y, axis=-1, keepdims=True) + EPS) * nw_ref[...]
        o_ref[pl.ds(base, CHUNK), :] = yn.astype(o_ref.dtype)
        return 0

    lax.fori_loop(0, nc, fin, 0)


def _ssd(u, conv_w, conv_b, dt_bias, a_log, d_skip, norm_w, *, layer, n_batch, t_ctx, t_lat):
    t_tot = t_ctx + t_lat
    nc_c, nc = t_ctx // CHUNK, t_tot // CHUNK
    return pl.pallas_call(
        functools.partial(_ssd_kernel, nc_c=nc_c, nc=nc),
        grid=(n_batch,),
        in_specs=[pl.BlockSpec((t_tot, SSD_PAD), lambda b: (b, 0)),
                  _layer_spec((CONV_W, SSD_XBC), layer), _layer_spec((1, SSD_XBC), layer),
                  _layer_spec((1, LANES), layer), _layer_spec((1, LANES), layer),
                  _layer_spec((1, SSD_WIDTH), layer), _layer_spec((1, SSD_WIDTH), layer)],
        out_specs=pl.BlockSpec((t_tot, SSD_WIDTH), lambda b: (b, 0)),
        out_shape=jax.ShapeDtypeStruct((n_batch * t_tot, SSD_WIDTH), BF16),
        scratch_shapes=[pltpu.VMEM((t_tot, SSD_XBC), F32), pltpu.VMEM((t_tot, LANES), F32),
                        pltpu.VMEM((t_tot, SSD_WIDTH), F32), pltpu.VMEM((CHUNK, SSD_WIDTH), F32)],
        compiler_params=_cparams(1),
        name="ssd",
    )(u, conv_w, conv_b, dt_bias, a_log, d_skip, norm_w)


def _lru_kernel(u_ref, cw_ref, cb_ref, wg_ref, bg_ref, lam_ref, o_ref, xc_scr, h_scr, *, nc_c, nc):
    t_tot = nc * CHUNK
    w = LRU_WIDTH

    def prep(c, _):
        base = pl.multiple_of(c * CHUNK, CHUNK)
        first, last = _stream_edges(c, nc_c, nc)
        xc_scr[pl.ds(base, CHUNK), :] = _conv_chunk(u_ref, 0, w, base, first, last, t_tot, cw_ref, cb_ref)
        return 0

    lax.fori_loop(0, nc, prep, 0)

    sub = _iota((CHUNK, w), 0) % 8
    n_tiles = CHUNK // 8

    for d in range(2):
        coef = -LRU_C * _softplus(-lam_ref[d:d + 1, :])

        def body(i, h_prev, d=d, coef=coef):
            c = _chunk_order(i, nc_c, nc, d == 1)
            base = pl.multiple_of(c * CHUNK, CHUNK)
            xc = xc_scr[pl.ds(base, CHUNK), :]
            gates = _bdot(xc, wg_ref[d]) + bg_ref[d:d + 1, :]
            r = _sigmoid(gates[:, :w])
            ig = _sigmoid(gates[:, w:])
            log_a = r * coef
            a = jnp.exp(log_a)
            b = jnp.sqrt(jnp.maximum(1.0 - jnp.exp(2.0 * log_a), 1e-12)) * (ig * xc)
            for k in (1, 2, 4):
                if d == 0:
                    keep = sub >= k
                    shift = k
                else:
                    keep = sub < 8 - k
                    shift = CHUNK - k
                a_s = jnp.where(keep, pltpu.roll(a, shift, axis=0), 1.0)
                b_s = jnp.where(keep, pltpu.roll(b, shift, axis=0), 0.0)
                b = a * b_s + b
                a = a * a_s
            tiles = range(n_tiles) if d == 0 else range(n_tiles - 1, -1, -1)
            outs = [None] * n_tiles
            for t in tiles:
                h_t = a[t * 8:(t + 1) * 8] * h_prev + b[t * 8:(t + 1) * 8]
                outs[t] = h_t
                h_prev = h_t[7:8] if d == 0 else h_t[0:1]
            h = jnp.concatenate(outs, axis=0)
            if d == 0:
                h_scr[pl.ds(base, CHUNK), :] = h
            else:
                gate = u_ref[pl.ds(base, CHUNK), w:2 * w]
                out = (h_scr[pl.ds(base, CHUNK), :] + h) * jax.nn.gelu(gate)
                o_ref[pl.ds(base, CHUNK), :] = out.astype(o_ref.dtype)
            return h_prev

        lax.fori_loop(0, nc, body, jnp.zeros((1, w), F32))


def _block_diag(wb):
    nb, bi, bj = wb.shape[-3:]
    eye = jnp.eye(nb, dtype=wb.dtype)
    return jnp.einsum('...hij,hg->...higj', wb, eye).reshape(wb.shape[:-3] + (nb * bi, nb * bj))


def _lru(u, conv_w, conv_b, wg, bg, lam, *, layer, n_batch, t_ctx, t_lat):
    t_tot = t_ctx + t_lat
    nc_c, nc = t_ctx // CHUNK, t_tot // CHUNK
    w = LRU_WIDTH
    return pl.pallas_call(
        functools.partial(_lru_kernel, nc_c=nc_c, nc=nc),
        grid=(n_batch,),
        in_specs=[pl.BlockSpec((t_tot, LRU_COLS), lambda b: (b, 0)),
                  _layer_spec((CONV_W, w), layer), _layer_spec((1, w), layer),
                  _layer_spec((2, w, 2 * w), layer), _layer_spec((2, 2 * w), layer), _layer_spec((2, w), layer)],
        out_specs=pl.BlockSpec((t_tot, w), lambda b: (b, 0)),
        out_shape=jax.ShapeDtypeStruct((n_batch * t_tot, w), BF16),
        scratch_shapes=[pltpu.VMEM((t_tot, w), F32), pltpu.VMEM((t_tot, w), F32)],
        compiler_params=_cparams(1),
        name="rglru",
    )(u, conv_w, conv_b, wg, bg, lam)


def _hgrn_kernel(q_ref, ff_ref, fb_ref, i_ref, g_ref, lbl_ref, nw_ref, esel_ref, o_ref, of_scr, ob_scr, st_scr, *,
                 layer, nc_c, nc, n_heads):
    n_sub = CHUNK // SUB
    blk_r = _iota((CHUNK, CHUNK), 0) // SUB
    blk_c = _iota((CHUNK, CHUNK), 1) // SUB
    row = _iota((CHUNK, CHUNK), 0)
    col = _iota((CHUNK, CHUNK), 1)
    same = blk_r == blk_c
    ones_blk = same.astype(BF16)
    scale = HGRN_HEADDIM ** -0.5
    log2e = 1.4426950408889634

    lbs = []
    for d in range(2):
        logits = lbl_ref[d]
        ex = jnp.exp(logits - jnp.max(logits, axis=0, keepdims=True))
        sm = ex / jnp.sum(ex, axis=0, keepdims=True)
        lbs.append(jnp.sum(sm[0:layer + 1], axis=0, keepdims=True) - sm[0:1])
    valids = [jnp.logical_and(same, row >= col), jnp.logical_and(same, row <= col)]
    tri_blks = [m.astype(BF16) for m in valids]
    st_scr[...] = jnp.zeros_like(st_scr)

    def chunk(d, hh, base):
        cols = slice(hh * LANES, (hh + 1) * LANES)
        lb = lbs[d][:, cols]
        f_ref = ff_ref if d == 0 else fb_ref
        qh = _silu(q_ref[pl.ds(base, CHUNK), cols]) * scale
        f = lb + (1.0 - lb) * _sigmoid(f_ref[pl.ds(base, CHUNK), cols])
        kk = 1.0 - f
        gl = jnp.log(f) * log2e
        v = i_ref[pl.ds(base, CHUNK), cols]
        parts = _split3(gl)
        cum = _dot_sel(tri_blks[d], parts)
        tot = _dot_sel(ones_blk, parts)
        halves = []
        for r0 in range(0, CHUNK, CHUNK // 2):
            rows = slice(r0, r0 + CHUNK // 2)
            q3 = qh[rows].reshape(n_sub // 2, SUB, LANES)
            c3 = cum[rows].reshape(n_sub // 2, SUB, LANES)
            k3 = kk[rows].reshape(n_sub // 2, SUB, LANES)
            acc = None
            for s0 in range(0, SUB, 4):
                ps = []
                for s in range(s0, s0 + 4):
                    e = jnp.exp2(jnp.minimum(c3 - c3[:, s:s + 1, :], 0.0))
                    p = (q3 * e) * k3[:, s:s + 1, :]
                    ps.append(p.reshape(CHUNK // 2, LANES).astype(BF16))
                part = jnp.dot(jnp.concatenate(ps, axis=1), esel_ref[s0 * LANES:(s0 + 4) * LANES, :],
                               preferred_element_type=F32)
                acc = part if acc is None else acc + part
            halves.append(acc)
        scores = jnp.concatenate(halves, axis=0)
        y = _bdot(jnp.where(valids[d], scores, 0.0), v)
        q_in = qh * jnp.exp2(cum)
        k_out = kk * jnp.exp2(tot - cum)
        v_t = v.T
        upd = _bdot(jnp.concatenate([jnp.where(blk_c == jb, v_t, 0.0) for jb in range(n_sub)], axis=0),
                    k_out)
        st = st_scr[d, hh]
        outs = [None] * n_sub
        order = range(n_sub) if d == 0 else range(n_sub - 1, -1, -1)
        for jb in order:
            r0 = jb * SUB
            outs[jb] = _bdot_nt(q_in[r0:r0 + SUB], st)
            st = st * jnp.exp2(tot[r0:r0 + 1, :]) + upd[jb * HGRN_HEADDIM:(jb + 1) * HGRN_HEADDIM]
        st_scr[d, hh] = st
        return y + jnp.concatenate(outs, axis=0)

    def body(i, _):
        for d in range(2):
            c = _chunk_order(i, nc_c, nc, d == 1)
            base = pl.multiple_of(c * CHUNK, CHUNK)
            o_scr = of_scr if d == 0 else ob_scr
            for hh in range(n_heads):
                o_scr[pl.ds(base, CHUNK), hh * LANES:(hh + 1) * LANES] = chunk(d, hh, base)
        return 0

    lax.fori_loop(0, nc, body, 0)

    def fin(c, _):
        base = pl.multiple_of(c * CHUNK, CHUNK)
        for hh in range(n_heads):
            cols = slice(hh * LANES, (hh + 1) * LANES)
            o = of_scr[pl.ds(base, CHUNK), cols] + ob_scr[pl.ds(base, CHUNK), cols]
            on = o * lax.rsqrt(jnp.mean(o * o, axis=-1, keepdims=True) + EPS) * nw_ref[...]
            out = on * _silu(g_ref[pl.ds(base, CHUNK), cols])
            o_ref[pl.ds(base, CHUNK), cols] = out.astype(o_ref.dtype)
        return 0

    lax.fori_loop(0, nc, fin, 0)


def _hgrn_selector():
    s_of_row = np.arange(SUB * LANES) // LANES
    return jnp.asarray(s_of_row[:, None] == (np.arange(LANES)[None, :] % SUB), dtype=BF16)


def _hgrn(u, lb_logits, norm_w, *, layer, n_batch, t_ctx, t_lat):
    t_tot = t_ctx + t_lat
    nc_c, nc = t_ctx // CHUNK, t_tot // CHUNK
    n_heads = 2
    ng = HGRN_HEADS // n_heads
    wblk = n_heads * LANES
    depth = lb_logits.shape[1]
    col = lambda part: pl.BlockSpec((t_tot, wblk), lambda b, h, part=part: (b, part * ng + h))
    return pl.pallas_call(
        functools.partial(_hgrn_kernel, layer=layer, nc_c=nc_c, nc=nc, n_heads=n_heads),
        grid=(n_batch, ng),
        in_specs=[col(0), col(1), col(2), col(3), col(4),
                  pl.BlockSpec((2, depth, wblk), lambda b, h: (0, 0, h)),
                  _layer_spec((1, LANES), layer),
                  _const_spec((SUB * LANES, LANES))],
        out_specs=pl.BlockSpec((t_tot, wblk), lambda b, h: (b, h)),
        out_shape=jax.ShapeDtypeStruct((n_batch * t_tot, HGRN_WIDTH), BF16),
        scratch_shapes=[pltpu.VMEM((t_tot, wblk), F32), pltpu.VMEM((t_tot, wblk), F32),
                        pltpu.VMEM((2, n_heads, HGRN_HEADDIM, HGRN_HEADDIM), F32)],
        compiler_params=_cparams(2),
        name="hgrn2",
    )(u, u, u, u, u, lb_logits, norm_w, _hgrn_selector())


def _ret_consts(t_ctx, t_lat):
    gam = 1.0 - np.exp2(-5.0 - np.arange(RET_HEADS, dtype=np.float64))
    t = np.arange(CHUNK, dtype=np.float64)
    diff = np.abs(t[:, None] - t[None, :])
    dsym = gam[:, None, None] ** diff[None]
    dsym[:, np.arange(CHUNK), np.arange(CHUNK)] = 2.0
    head_of_lane = np.repeat(np.arange(RET_HEADS), RET_QK_DIM)
    qf = gam[head_of_lane][None, :] ** (t[:, None] + 1.0)
    qb = gam[head_of_lane][None, :] ** (CHUNK - t[:, None])
    kf = gam[:, None] ** (CHUNK - 1.0 - t[None, :])
    kb = gam[:, None] ** t[None, :]
    gtot = gam ** CHUNK
    pos = np.arange(t_lat)
    rows_p, cols_p = pos // GRID_W, pos % GRID_W
    quarter = RET_QK_DIM // 4
    inv = ROPE_BASE ** (-np.arange(quarter, dtype=np.float64) / quarter)
    ang_r = rows_p[:, None] * inv[None, :]
    ang_c = cols_p[:, None] * inv[None, :]
    cos_h = np.concatenate([np.cos(ang_r), np.cos(ang_r), np.cos(ang_c), np.cos(ang_c)], axis=1)
    sin_h = np.concatenate([-np.sin(ang_r), np.sin(ang_r), -np.sin(ang_c), np.sin(ang_c)], axis=1)
    cos_t = np.concatenate([np.ones((t_ctx, RET_QK_DIM)), cos_h], axis=0)
    sin_t = np.concatenate([np.zeros((t_ctx, RET_QK_DIM)), sin_h], axis=0)
    cos_t = np.tile(cos_t, (1, RET_HEADS))
    sin_t = np.tile(sin_t, (1, RET_HEADS))
    f32 = lambda a: jnp.asarray(a, dtype=F32)
    return (f32(dsym), f32(qf), f32(qb), f32(np.concatenate([kf, kb], axis=0)), [float(g) for g in gtot],
            f32(cos_t), f32(sin_t))


def _ret_kernel(u_ref, dsym_ref, qf_ref, qb_ref, kfac_ref, cos_ref, sin_ref, o_ref, o_scr, s_scr, *,
                gtot, nc_c, nc):
    nqk = RET_HEADS * RET_QK_DIM
    lane = _iota((CHUNK, nqk), 1)
    low_half = (lane % (RET_QK_DIM // 2)) < (RET_QK_DIM // 4)
    tile_lane = _iota((CHUNK, LANES), 1)
    head_mask = [(tile_lane // RET_QK_DIM) == p for p in range(2)]
    kscale = RET_QK_DIM ** -0.5

    def rope(x, cos, sin):
        swapped = jnp.where(low_half, pltpu.roll(x, nqk - RET_QK_DIM // 4, axis=1),
                            pltpu.roll(x, RET_QK_DIM // 4, axis=1))
        return x * cos + swapped * sin

    for d in range(2):
        s_scr[...] = jnp.zeros_like(s_scr)
        qfac = qf_ref if d == 0 else qb_ref

        def body(i, _, d=d, qfac=qfac):
            c = _chunk_order(i, nc_c, nc, d == 1)
            base = pl.multiple_of(c * CHUNK, CHUNK)
            cos = cos_ref[pl.ds(base, CHUNK), :]
            sin = sin_ref[pl.ds(base, CHUNK), :]
            q = rope(u_ref[pl.ds(base, CHUNK), 0:nqk], cos, sin)
            k = rope(u_ref[pl.ds(base, CHUNK), nqk:2 * nqk], cos, sin) * kscale
            kt = k.T
            qd = q * qfac[...]
            for h in range(RET_HEADS):
                p = h // 2
                v = u_ref[pl.ds(base, CHUNK), 2 * nqk + h * RET_V_DIM:2 * nqk + (h + 1) * RET_V_DIM]
                hm = head_mask[h % 2]
                kt_p = kt[p * LANES:(p + 1) * LANES]
                s_old = s_scr[h]
                y = _bdot(jnp.where(hm, qd[:, p * LANES:(p + 1) * LANES], 0.0), s_old)
                kfac = kfac_ref[d * RET_HEADS + h:d * RET_HEADS + h + 1, :]
                s_scr[h] = gtot[h] * s_old + _bdot(kt_p * kfac, v)
                if d == 0:
                    raw = _bdot(jnp.where(hm, q[:, p * LANES:(p + 1) * LANES], 0.0), kt_p)
                    y = y + _bdot(raw * dsym_ref[h], v)
                    o_scr[pl.ds(base, CHUNK), h * RET_V_DIM:(h + 1) * RET_V_DIM] = y
                else:
                    o = o_scr[pl.ds(base, CHUNK), h * RET_V_DIM:(h + 1) * RET_V_DIM] + y
                    on = o * lax.rsqrt(jnp.mean(o * o, axis=-1, keepdims=True) + EPS)
                    gate = u_ref[pl.ds(base, CHUNK), 2 * nqk + RET_WIDTH + h * RET_V_DIM:
                                 2 * nqk + RET_WIDTH + (h + 1) * RET_V_DIM]
                    o_ref[pl.ds(base, CHUNK), h * RET_V_DIM:(h + 1) * RET_V_DIM] = (
                        _silu(gate) * on).astype(o_ref.dtype)
            return 0

        lax.fori_loop(0, nc, body, 0)


def _ret(u, *, n_batch, t_ctx, t_lat):
    t_tot = t_ctx + t_lat
    nc_c, nc = t_ctx // CHUNK, t_tot // CHUNK
    dsym, qf, qb, kfac, gtot, cos_t, sin_t = _ret_consts(t_ctx, t_lat)
    nqk = RET_HEADS * RET_QK_DIM
    return pl.pallas_call(
        functools.partial(_ret_kernel, gtot=gtot, nc_c=nc_c, nc=nc),
        grid=(n_batch,),
        in_specs=[pl.BlockSpec((t_tot, RET_COLS), lambda b: (b, 0)),
                  _const_spec((RET_HEADS, CHUNK, CHUNK)), _const_spec((CHUNK, nqk)), _const_spec((CHUNK, nqk)),
                  _const_spec((2 * RET_HEADS, CHUNK)), _const_spec((t_tot, nqk)), _const_spec((t_tot, nqk))],
        out_specs=pl.BlockSpec((t_tot, RET_WIDTH), lambda b: (b, 0)),
        out_shape=jax.ShapeDtypeStruct((n_batch * t_tot, RET_WIDTH), BF16),
        scratch_shapes=[pltpu.VMEM((t_tot, RET_WIDTH), F32), pltpu.VMEM((RET_HEADS, LANES, RET_V_DIM), F32)],
        compiler_params=_cparams(1),
        name="retention",
    )(u, dsym, qf, qb, kfac, cos_t, sin_t)


def kernel(x, c, ctx, c_ctx, w_mod, b_mod, norm_pre, norm_post, ffn_w1, ffn_w3, ffn_w2, w_in, w_out,
           ssd_conv_w, ssd_conv_b, ssd_dt_bias, ssd_a_log, ssd_d, ssd_norm_w,
           lru_conv_w, lru_conv_b, lru_wa, lru_ba, lru_wx, lru_bx, lru_lambda,
           hgrn_lb_logits, hgrn_norm_w):
    n_batch, seq, d = x.shape
    ctx_len = ctx.shape[1]
    depth = w_mod.shape[0]
    lat_rows = n_batch * seq
    all_rows = lat_rows + n_batch * ctx_len
    geom = dict(lat_rows=lat_rows, seq=seq, ctx_len=ctx_len, n_batch=n_batch)
    mix = dict(n_batch=n_batch, t_ctx=ctx_len, t_lat=seq)

    c_all = jnp.concatenate([c, c_ctx[None, :], jnp.zeros((8 - n_batch - 1, d), F32)], axis=0)
    mods = _modulation(c_all, w_mod, b_mod).reshape(depth, 8, N_MOD, d)
    w2 = ffn_w2.astype(BF16)
    s0, s1, s2 = SSD_COLS, SSD_COLS + LRU_COLS, SSD_COLS + LRU_COLS + HGRN_COLS
    w_in_parts = (jnp.pad(w_in[:, :, :s0].astype(BF16), ((0, 0), (0, 0), (0, SSD_PAD - SSD_COLS))),
                  w_in[:, :, s0:s1].astype(BF16), w_in[:, :, s1:s2].astype(BF16), w_in[:, :, s2:].astype(BF16))
    w_out_b = w_out.astype(BF16)
    gpre = norm_pre.reshape(depth * 3, 1, d)
    gpost = norm_post.reshape(depth * 3, 1, d)
    rowvec = lambda v: v.reshape(depth, 1, -1)
    lane_pad = lambda v: jnp.pad(v.reshape(depth, 1, -1), ((0, 0), (0, 0), (0, LANES - v[0].size)))
    ssd_params = (ssd_conv_w, rowvec(ssd_conv_b), lane_pad(ssd_dt_bias), lane_pad(ssd_a_log),
                  rowvec(jnp.repeat(ssd_d, SSD_HEADDIM, axis=-1)), rowvec(ssd_norm_w))
    lru_wg = jnp.concatenate([_block_diag(lru_wa), _block_diag(lru_wx)], axis=-1).astype(BF16)
    lru_bg = jnp.concatenate([lru_ba.reshape(depth, 2, -1), lru_bx.reshape(depth, 2, -1)], axis=-1)
    lru_params = (lru_conv_w, rowvec(lru_conv_b), lru_wg, lru_bg, lru_lambda)
    hgrn_nw = rowvec(hgrn_norm_w)

    rowgeom = dict(lat_rows=lat_rows, seq=seq, n_batch=n_batch)
    xs, h = _prenorm(x.reshape(lat_rows, d), ctx.reshape(n_batch * ctx_len, d), mods, gpre, **rowgeom)
    for l in range(depth):
        last = l == depth - 1
        g = _ffn_up(h, ffn_w1, ffn_w3, layer=l, k=0, n_rows=all_rows)
        xs, h = _ffn_down(g, xs, mods, gpost, gpre, w2, layer=l, k=0, j=0, nxt=3, nxt_layer=l,
                          n_rows=all_rows, **rowgeom)
        ua, ub, uc, ud = _inproj(h, w_in_parts, layer=l, **geom)
        ya = _ssd(ua, *ssd_params, layer=l, **mix)
        yb = _lru(ub, *lru_params, layer=l, **mix)
        yc = _hgrn(uc, hgrn_lb_logits, hgrn_nw, layer=l, **mix)
        yd = _ret(ud, **mix)
        n_rows = lat_rows if last else all_rows
        xs, h = _outproj(xs, (ya, yb, yc, yd), w_out_b, mods, gpost, gpre, layer=l, n_rows=n_rows, **geom)
        g = _ffn_up(h, ffn_w1, ffn_w3, layer=l, k=1, n_rows=n_rows)
        if last:
            (xs,) = _ffn_down(g, xs, mods, gpost, gpre, w2, layer=l, k=1, j=6, nxt=None, nxt_layer=l,
                              n_rows=n_rows, **rowgeom)
        else:
            xs, h = _ffn_down(g, xs, mods, gpost, gpre, w2, layer=l, k=1, j=6, nxt=0, nxt_layer=l + 1,
                              n_rows=n_rows, **rowgeom)
    return xs.reshape(n_batch, seq, d)
```

```python
import functools

import numpy as np
import jax
import jax.numpy as jnp
from jax import lax
from jax.experimental import pallas as pl
from jax.experimental.pallas import tpu as pltpu

F32 = jnp.float32
BF16 = jnp.bfloat16

N_MOD = 9
CONV_W = 4
EPS = 1e-6
GRID_W = 64

SSD_WIDTH = 512
SSD_HEADDIM = 64
SSD_HEADS = 8
SSD_GROUPS = 2
SSD_STATE = 64
SSD_XBC = SSD_WIDTH + 2 * SSD_GROUPS * SSD_STATE
SSD_COLS = SSD_WIDTH + SSD_XBC + 2 * SSD_HEADS
SSD_PAD = 1408
LRU_WIDTH = 512
LRU_C = 8.0
LRU_COLS = 2 * LRU_WIDTH
HGRN_WIDTH = 512
HGRN_HEADDIM = 128
HGRN_HEADS = 4
HGRN_COLS = 5 * HGRN_WIDTH
RET_WIDTH = 512
RET_V_DIM = 128
RET_HEADS = 4
RET_QK_DIM = 64
RET_COLS = 2 * RET_HEADS * RET_QK_DIM + 2 * RET_WIDTH
ROPE_BASE = 10000.0

LANES = 128
CHUNK = 128
SUB = 16
VMEM_LIMIT = 56 * 1024 * 1024


def _cparams(n_axes):
    return pltpu.CompilerParams(dimension_semantics=("arbitrary",) * n_axes,
                                vmem_limit_bytes=VMEM_LIMIT)


def _bdot(a, b):
    return jnp.dot(a.astype(BF16), b.astype(BF16), preferred_element_type=F32)


def _bdot_nt(a, b):
    return lax.dot_general(a.astype(BF16), b.astype(BF16), (((1,), (1,)), ((), ())),
                           preferred_element_type=F32)


def _split3(x):
    hi = x.astype(BF16)
    r1 = x - hi.astype(F32)
    mid = r1.astype(BF16)
    lo = (r1 - mid.astype(F32)).astype(BF16)
    return hi, mid, lo


def _dot_sel(sel, parts):
    return (jnp.dot(sel, parts[0], preferred_element_type=F32)
            + jnp.dot(sel, parts[1], preferred_element_type=F32)
            + jnp.dot(sel, parts[2], preferred_element_type=F32))


def _sigmoid(x):
    return 1.0 / (1.0 + jnp.exp(-x))


def _silu(x):
    return x * _sigmoid(x)


def _softplus(x):
    return jnp.maximum(x, 0.0) + jnp.log(1.0 + jnp.exp(-jnp.abs(x)))


def _iota(shape, dim):
    return lax.broadcasted_iota(jnp.int32, shape, dim)


def _const_spec(shape):
    return pl.BlockSpec(shape, lambda *_: (0,) * len(shape))


def _layer_spec(shape, layer):
    return pl.BlockSpec((None,) + shape, lambda *_: (layer,) + (0,) * len(shape))


def _mod_kernel(c_ref, w_ref, b_ref, o_ref):
    o_ref[...] = _bdot(_silu(c_ref[...]), w_ref[...]) + b_ref[...]


def _modulation(c_all, w_mod, b_mod):
    depth, d, n = w_mod.shape
    tn = 1024
    return pl.pallas_call(
        _mod_kernel,
        grid=(depth, n // tn),
        in_specs=[pl.BlockSpec((8, d), lambda l, j: (0, 0)),
                  pl.BlockSpec((None, d, tn), lambda l, j: (l, 0, j)),
                  pl.BlockSpec((None, 1, tn), lambda l, j: (l, 0, j))],
        out_specs=pl.BlockSpec((None, 8, tn), lambda l, j: (l, 0, j)),
        out_shape=jax.ShapeDtypeStruct((depth, 8, n), F32),
        compiler_params=_cparams(2),
        name="modulation",
    )(c_all, w_mod, b_mod.reshape(depth, 1, n))


def _mod_row_idx(i, tm, layer, lat_rows, seq, n_batch):
    return (layer, jnp.where(i * tm < lat_rows, (i * tm) // seq, n_batch), 0, 0)


def _mod_norm(x, g, shift, scale):
    y = x * lax.rsqrt(jnp.mean(x * x, axis=-1, keepdims=True) + EPS) * g
    return (y * (1.0 + scale) + shift).astype(BF16)


def _prenorm_kernel(xl_ref, xc_ref, mod_ref, g_ref, x_ref, h_ref, *, n_split):
    x = jnp.where(pl.program_id(0) < n_split, xl_ref[...], xc_ref[...])
    x_ref[...] = x
    h_ref[...] = _mod_norm(x, g_ref[...], mod_ref[0:1, :], mod_ref[1:2, :])


def _prenorm(xl, xc, mods, gpre, *, lat_rows, seq, n_batch):
    d = xl.shape[1]
    rows = xl.shape[0] + xc.shape[0]
    tm = 512
    n_split = xl.shape[0] // tm
    blk = pl.BlockSpec((tm, d), lambda i: (i, 0))
    return pl.pallas_call(
        functools.partial(_prenorm_kernel, n_split=n_split),
        grid=(rows // tm,),
        in_specs=[pl.BlockSpec((tm, d), lambda i: (jnp.minimum(i, n_split - 1), 0)),
                  pl.BlockSpec((tm, d), lambda i: (jnp.maximum(i - n_split, 0), 0)),
                  pl.BlockSpec((None, None, N_MOD, d), lambda i: _mod_row_idx(i, tm, 0, lat_rows, seq, n_batch)),
                  pl.BlockSpec((None, 1, d), lambda i: (0, 0, 0))],
        out_specs=[blk, blk],
        out_shape=[jax.ShapeDtypeStruct((rows, d), F32), jax.ShapeDtypeStruct((rows, d), BF16)],
        compiler_params=_cparams(1),
        name="prenorm",
    )(xl, xc, mods, gpre)


def _ffn_up_kernel(h_ref, w1_ref, w3_ref, g_ref, w1_scr, w3_scr, *, n_sub):
    @pl.when(pl.program_id(1) == 0)
    def _():
        w1_scr[...] = w1_ref[...].astype(BF16)
        w3_scr[...] = w3_ref[...].astype(BF16)

    rows = h_ref.shape[0] // n_sub
    for r in range(n_sub):
        h = h_ref[r * rows:(r + 1) * rows, :]
        a = jnp.dot(h, w1_scr[...], preferred_element_type=F32)
        b = jnp.dot(h, w3_scr[...], preferred_element_type=F32)
        g_ref[r * rows:(r + 1) * rows, :] = (_silu(a) * b).astype(BF16)


def _ffn_up(h, w1, w3, *, layer, k, n_rows):
    d = h.shape[1]
    dff = w1.shape[-1]
    tm, tf = 1024, 512
    w_spec = pl.BlockSpec((None, None, d, tf), lambda f, i: (layer, k, 0, f))
    return pl.pallas_call(
        functools.partial(_ffn_up_kernel, n_sub=2),
        grid=(dff // tf, n_rows // tm),
        in_specs=[pl.BlockSpec((tm, d), lambda f, i: (i, 0)), w_spec, w_spec],
        out_specs=pl.BlockSpec((tm, tf), lambda f, i: (i, f)),
        out_shape=jax.ShapeDtypeStruct((n_rows, dff), BF16),
        scratch_shapes=[pltpu.VMEM((d, tf), BF16), pltpu.VMEM((d, tf), BF16)],
        compiler_params=_cparams(2),
        name="ffn_up",
    )(h, w1, w3)


EPI_SUB = 2


def _residual_epilogue(rows, x_ref, y, gate, gpost, mod_ref, gnext_ref, nxt, o_ref, hn_ref):
    yn = y * lax.rsqrt(jnp.mean(y * y, axis=-1, keepdims=True) + EPS) * gpost
    xn = x_ref[rows, :] + gate * yn
    o_ref[rows, :] = xn
    if nxt is not None:
        hn_ref[rows, :] = _mod_norm(xn, gnext_ref[...], mod_ref[nxt:nxt + 1, :], mod_ref[nxt + 1:nxt + 2, :])


def _ffn_down_kernel(*refs, layer, k, j, nxt, nxt_mod_own):
    g_ref, x_ref, mod_ref, gpost_ref = refs[:4]
    pos = 4
    nmod_ref = mod_ref
    gnext_ref = hn_ref = None
    if nxt is not None:
        if not nxt_mod_own:
            nmod_ref = refs[pos]
            pos += 1
        gnext_ref = refs[pos]
        pos += 1
    w2_hbm, o_ref = refs[pos], refs[pos + 1]
    pos += 2
    if nxt is not None:
        hn_ref = refs[pos]
        pos += 1
    w2_scr, stage_scr, sem = refs[pos], refs[pos + 1], refs[pos + 2]

    @pl.when(pl.program_id(0) == 0)
    def _():
        def consume(r0, chunk):
            w2_scr[pl.ds(r0, chunk.shape[0]), :] = chunk.astype(BF16)

        _stage_rows(w2_hbm.at[layer, k], stage_scr, sem, w2_hbm.shape[2], consume)

    n = x_ref.shape[0] // EPI_SUB
    for r in range(EPI_SUB):
        rows = slice(r * n, (r + 1) * n)
        y = jnp.dot(g_ref[rows, :], w2_scr[...], preferred_element_type=F32)
        _residual_epilogue(rows, x_ref, y, 0.5 * mod_ref[j + 2:j + 3, :], gpost_ref[...], nmod_ref, gnext_ref,
                           nxt, o_ref, hn_ref)


def _ffn_down(g, x, mods, gpost, gpre, w2, *, layer, k, j, nxt, nxt_layer, n_rows, lat_rows, seq, n_batch):
    d = x.shape[1]
    dff = g.shape[1]
    tm = 256
    norm_row = layer * 3 + (0 if j == 0 else 2)
    blk = pl.BlockSpec((tm, d), lambda i: (i, 0))
    mod_spec = lambda ly: pl.BlockSpec((None, None, N_MOD, d),
                                       lambda i: _mod_row_idx(i, tm, ly, lat_rows, seq, n_batch))
    in_specs = [pl.BlockSpec((tm, dff), lambda i: (i, 0)), blk, mod_spec(layer),
                pl.BlockSpec((None, 1, d), lambda i: (norm_row, 0, 0))]
    args = [g, x, mods, gpost]
    out_specs, out_shape = [blk], [jax.ShapeDtypeStruct((n_rows, d), F32)]
    nxt_mod_own = nxt_layer == layer
    if nxt is not None:
        if not nxt_mod_own:
            in_specs.append(mod_spec(nxt_layer))
            args.append(mods)
        in_specs.append(pl.BlockSpec((None, 1, d), lambda i: (nxt_layer * 3 + nxt // 3, 0, 0)))
        args.append(gpre)
        out_specs.append(blk)
        out_shape.append(jax.ShapeDtypeStruct((n_rows, d), BF16))
    in_specs.append(pl.BlockSpec(memory_space=pl.ANY))
    args.append(w2)
    return pl.pallas_call(
        functools.partial(_ffn_down_kernel, layer=layer, k=k, j=j, nxt=nxt, nxt_mod_own=nxt_mod_own),
        grid=(n_rows // tm,),
        in_specs=in_specs,
        out_specs=out_specs,
        out_shape=out_shape,
        scratch_shapes=[pltpu.VMEM((dff, d), BF16), pltpu.VMEM((2, 512, d), F32), pltpu.SemaphoreType.DMA((2,))],
        compiler_params=_cparams(1),
        name="ffn_down",
    )(*args)


def _stage_rows(src_hbm, stage_scr, sem, n_rows, consume):
    rpc = stage_scr.shape[1]
    n_chunks = n_rows // rpc

    def copy(c, slot):
        return pltpu.make_async_copy(src_hbm.at[pl.ds(c * rpc, rpc)], stage_scr.at[slot], sem.at[slot])

    copy(0, 0).start()

    def step(c, _):
        slot = c % 2

        @pl.when(c + 1 < n_chunks)
        def _():
            copy(c + 1, 1 - slot).start()

        copy(c, slot).wait()
        consume(pl.multiple_of(c * rpc, rpc), stage_scr[slot])
        return 0

    lax.fori_loop(0, n_chunks, step, 0)


def _inproj_kernel(h_ref, w_hbm, oa_ref, ob_ref, oc_ref, od_ref, wa_scr, wb_scr, wc_scr, wd_scr, stage_scr, sem, *,
                   layer, splits):
    w_scr = (wa_scr, wb_scr, wc_scr, wd_scr)

    @pl.when(pl.program_id(0) == 0)
    def _():
        def consume(r0, chunk):
            rows = pl.ds(r0, chunk.shape[0])
            for m, scr in enumerate(w_scr):
                part = chunk[:, splits[m]:splits[m + 1]].astype(BF16)
                pad = scr.shape[1] - part.shape[1]
                if pad:
                    part = jnp.concatenate([part, jnp.zeros((part.shape[0], pad), BF16)], axis=1)
                scr[rows, :] = part

        _stage_rows(w_hbm.at[layer], stage_scr, sem, w_hbm.shape[1], consume)

    h = h_ref[...]
    for w, o_ref in zip(w_scr, (oa_ref, ob_ref, oc_ref, od_ref)):
        o_ref[...] = jnp.dot(h, w[...], preferred_element_type=F32)


def _x_to_b_block(i, tm, lat_rows, seq, ctx_len):
    per_b = (seq + ctx_len) // tm
    lat_b = seq // tm
    ctx_b = ctx_len // tm
    n_lat = lat_rows // tm
    lat_idx = (i // lat_b) * per_b + ctx_b + i % lat_b
    ic = i - n_lat
    ctx_idx = (ic // ctx_b) * per_b + ic % ctx_b
    return jnp.where(i < n_lat, lat_idx, ctx_idx)


def _inproj(h, w_in, *, layer, lat_rows, seq, ctx_len, n_batch):
    rows, d = h.shape
    tm = 256
    stage_rows = 128
    splits = (0, SSD_COLS, SSD_COLS + LRU_COLS, SSD_COLS + LRU_COLS + HGRN_COLS, w_in.shape[-1])
    widths = [SSD_PAD, LRU_COLS, HGRN_COLS, RET_COLS]

    def out_idx(i):
        return (_x_to_b_block(i, tm, lat_rows, seq, ctx_len), 0)

    return pl.pallas_call(
        functools.partial(_inproj_kernel, layer=layer, splits=splits),
        grid=(rows // tm,),
        in_specs=[pl.BlockSpec((tm, d), lambda i: (i, 0)), pl.BlockSpec(memory_space=pl.ANY)],
        out_specs=[pl.BlockSpec((tm, n), out_idx) for n in widths],
        out_shape=[jax.ShapeDtypeStruct((rows, n), F32) for n in widths],
        scratch_shapes=[pltpu.VMEM((d, n), BF16) for n in widths]
        + [pltpu.VMEM((2, stage_rows, w_in.shape[-1]), F32), pltpu.SemaphoreType.DMA((2,))],
        compiler_params=_cparams(1),
        name="inproj",
    )(h, w_in)


def _outproj_kernel(x_ref, ya_ref, yb_ref, yc_ref, yd_ref, w_ref, mod_ref, gpost_ref, gnext_ref, o_ref, hn_ref):
    w = LRU_WIDTH
    n = x_ref.shape[0] // EPI_SUB
    for r in range(EPI_SUB):
        rows = slice(r * n, (r + 1) * n)
        y = (jnp.dot(ya_ref[rows, :], w_ref[0 * w:1 * w, :], preferred_element_type=F32)
             + jnp.dot(yb_ref[rows, :], w_ref[1 * w:2 * w, :], preferred_element_type=F32)
             + jnp.dot(yc_ref[rows, :], w_ref[2 * w:3 * w, :], preferred_element_type=F32)
             + jnp.dot(yd_ref[rows, :], w_ref[3 * w:4 * w, :], preferred_element_type=F32))
        _residual_epilogue(rows, x_ref, y, mod_ref[5:6, :], gpost_ref[...], mod_ref, gnext_ref, 6, o_ref, hn_ref)


def _outproj(x, ys, w, mods, gpost, gpre, *, layer, n_rows, lat_rows, seq, ctx_len, n_batch):
    d = x.shape[1]
    tm = 256
    wy = ys[0].shape[1]
    blk = pl.BlockSpec((tm, d), lambda i: (i, 0))

    def y_idx(i):
        return (_x_to_b_block(i, tm, lat_rows, seq, ctx_len), 0)

    return pl.pallas_call(
        _outproj_kernel,
        grid=(n_rows // tm,),
        in_specs=[blk] + [pl.BlockSpec((tm, wy), y_idx)] * 4
        + [_layer_spec((4 * wy, d), layer),
           pl.BlockSpec((None, None, N_MOD, d), lambda i: _mod_row_idx(i, tm, layer, lat_rows, seq, n_batch)),
           pl.BlockSpec((None, 1, d), lambda i: (layer * 3 + 1, 0, 0)),
           pl.BlockSpec((None, 1, d), lambda i: (layer * 3 + 2, 0, 0))],
        out_specs=[blk, blk],
        out_shape=[jax.ShapeDtypeStruct((n_rows, d), F32), jax.ShapeDtypeStruct((n_rows, d), BF16)],
        compiler_params=_cparams(1),
        name="outproj",
    )(x, *ys, w, mods, gpost, gpre)


def _conv_chunk(u_ref, col0, ncols, base, first, last, t_tot, w_ref, b_ref):
    cols = slice(col0, col0 + ncols)
    prev = u_ref[pl.ds(pl.multiple_of(jnp.maximum(base - 8, 0), 8), 8), cols]
    nxt = u_ref[pl.ds(pl.multiple_of(jnp.minimum(base + CHUNK, t_tot - 8), 8), 8), cols]
    cur = u_ref[pl.ds(base, CHUNK), cols]
    prev = jnp.where(first, 0.0, prev)
    nxt = jnp.where(last, 0.0, nxt)
    win = jnp.concatenate([prev, cur, nxt], axis=0)
    n = CHUNK + 16
    xm1 = pltpu.roll(win, 1, axis=0)[8:8 + CHUNK]
    xp1 = pltpu.roll(win, n - 1, axis=0)[8:8 + CHUNK]
    xp2 = pltpu.roll(win, n - 2, axis=0)[8:8 + CHUNK]
    return (w_ref[0:1, :] * xm1 + w_ref[1:2, :] * cur + w_ref[2:3, :] * xp1 + w_ref[3:4, :] * xp2
            + b_ref[...])


def _chunk_order(i, nc_c, nc, reverse):
    if not reverse:
        return i
    return jnp.where(i < nc_c, nc_c - 1 - i, nc + nc_c - 1 - i)


def _stream_edges(c, nc_c, nc):
    first = jnp.logical_or(c == 0, c == nc_c)
    last = jnp.logical_or(c == nc_c - 1, c == nc - 1)
    return first, last


def _ssd_kernel(u_ref, cw_ref, cb_ref, dtb_ref, alog_ref, dskip_ref, nw_ref, o_ref,
                act_scr, dl_scr, y_scr, s_scr, *, nc_c, nc):
    t_tot = nc * CHUNK
    xbc0 = SSD_WIDTH
    dt0 = SSD_WIDTH + SSD_XBC

    def prep(c, _):
        base = pl.multiple_of(c * CHUNK, CHUNK)
        first, last = _stream_edges(c, nc_c, nc)
        conv = _conv_chunk(u_ref, xbc0, SSD_XBC, base, first, last, t_tot, cw_ref, cb_ref)
        act_scr[pl.ds(base, CHUNK), :] = _silu(conv)
        dt = u_ref[pl.ds(base, CHUNK), dt0:dt0 + LANES]
        dl_scr[pl.ds(base, CHUNK), :] = _softplus(dt + dtb_ref[...])
        return 0

    lax.fori_loop(0, nc, prep, 0)

    a_neg = -jnp.exp(alog_ref[...])
    row = _iota((CHUNK, CHUNK), 0)
    col = _iota((CHUNK, CHUNK), 1)
    lane_lo = _iota((CHUNK, LANES), 1) < SSD_HEADDIM
    grp_mask = [(_iota((CHUNK, LANES), 1) // SSD_STATE) == g for g in range(SSD_GROUPS)]

    s_scr[...] = jnp.zeros_like(s_scr)
    bodies = []
    for d in range(2):
        tri = (row >= col) if d == 0 else (row <= col)
        tri_b = tri.astype(BF16)
        edge = CHUNK - 1 if d == 0 else 0

        def body(i, d=d, tri=tri, tri_b=tri_b, edge=edge):
            c = _chunk_order(i, nc_c, nc, d == 1)
            base = pl.multiple_of(c * CHUNK, CHUNK)
            act = act_scr[pl.ds(base, CHUNK), :]
            bm = act[:, SSD_WIDTH:SSD_WIDTH + LANES]
            cm = act[:, SSD_WIDTH + LANES:SSD_WIDTH + 2 * LANES]
            delta = dl_scr[pl.ds(base, CHUNK), :]
            la = delta * a_neg
            cum_col = _dot_sel(tri_b, _split3(la))
            cum_row = cum_col.T
            delta_row = delta.T
            bt = bm.T
            gmat = [_bdot(jnp.where(grp_mask[g], cm, 0.0), bt) for g in range(SSD_GROUPS)]
            for k in range(SSD_HEADS // 2):
                g = (2 * k) // (SSD_HEADS // SSD_GROUPS)
                x_tile = act[:, k * LANES:(k + 1) * LANES]
                cg = jnp.where(grp_mask[g], cm, 0.0)
                m_list, w_list, e_list, tot_list = [], [], [], []
                for h in (2 * k, 2 * k + 1):
                    j = d * SSD_HEADS + h
                    ccol = cum_col[:, j:j + 1]
                    crow = cum_row[j:j + 1, :]
                    drow = delta_row[j:j + 1, :]
                    dec = jnp.where(tri, jnp.exp(jnp.where(tri, ccol - crow, 0.0)), 0.0)
                    m_list.append(gmat[g] * dec * drow)
                    tot = cum_row[j:j + 1, edge:edge + 1]
                    w_list.append(bt * (drow * jnp.exp(tot - crow)))
                    e_list.append(jnp.exp(ccol))
                    tot_list.append(jnp.exp(tot))
                y2 = _bdot(jnp.concatenate(m_list, axis=0), x_tile)
                u2 = _bdot(jnp.concatenate(w_list, axis=0), x_tile)
                s_old = s_scr[d, :, k * LANES:(k + 1) * LANES]
                y_inter = _bdot(cg, s_old) * jnp.where(lane_lo, e_list[0], e_list[1])
                y_tile = jnp.where(lane_lo, y2[:CHUNK], y2[CHUNK:]) + y_inter
                s_scr[d, :, k * LANES:(k + 1) * LANES] = (
                    s_old * jnp.where(lane_lo, tot_list[0], tot_list[1])
                    + jnp.where(lane_lo, u2[:CHUNK], u2[CHUNK:]))
                if d == 0:
                    y_tile = y_tile + x_tile * dskip_ref[:, k * LANES:(k + 1) * LANES]
                y_scr[d, pl.ds(base, CHUNK), k * LANES:(k + 1) * LANES] = y_tile

        bodies.append(body)

    def both(i, _):
        bodies[0](i)
        bodies[1](i)
        return 0

    lax.fori_loop(0, nc, both, 0)

    def fin(c, _):
        base = pl.multiple_of(c * CHUNK, CHUNK)
        z = u_ref[pl.ds(base, CHUNK), 0:SSD_WIDTH]
        y = (y_scr[0, pl.ds(base, CHUNK), :] + y_scr[1, pl.ds(base, CHUNK), :]) * _silu(z)
        yn = y * lax.rsqrt(jnp.mean(y * y, axis=-1, keepdims=True) + EPS) * nw_ref[...]
        o_ref[pl.ds(base, CHUNK), :] = yn.astype(o_ref.dtype)
        return 0

    lax.fori_loop(0, nc, fin, 0)


def _ssd(u, conv_w, conv_b, dt_bias, a_log, d_skip, norm_w, *, layer, n_batch, t_ctx, t_lat):
    t_tot = t_ctx + t_lat
    nc_c, nc = t_ctx // CHUNK, t_tot // CHUNK
    return pl.pallas_call(
        functools.partial(_ssd_kernel, nc_c=nc_c, nc=nc),
        grid=(n_batch,),
        in_specs=[pl.BlockSpec((t_tot, SSD_PAD), lambda b: (b, 0)),
                  _layer_spec((CONV_W, SSD_XBC), layer), _layer_spec((1, SSD_XBC), layer),
                  _layer_spec((1, LANES), layer), _layer_spec((1, LANES), layer),
                  _layer_spec((1, SSD_WIDTH), layer), _layer_spec((1, SSD_WIDTH), layer)],
        out_specs=pl.BlockSpec((t_tot, SSD_WIDTH), lambda b: (b, 0)),
        out_shape=jax.ShapeDtypeStruct((n_batch * t_tot, SSD_WIDTH), BF16),
        scratch_shapes=[pltpu.VMEM((t_tot, SSD_XBC), F32), pltpu.VMEM((t_tot, LANES), F32),
                        pltpu.VMEM((2, t_tot, SSD_WIDTH), F32), pltpu.VMEM((2, CHUNK, SSD_WIDTH), F32)],
        compiler_params=_cparams(1),
        name="ssd",
    )(u, conv_w, conv_b, dt_bias, a_log, d_skip, norm_w)


def _lru_kernel(u_ref, cw_ref, cb_ref, wg_ref, bg_ref, lam_ref, o_ref, xc_scr, h_scr, *, nc_c, nc):
    t_tot = nc * CHUNK
    w = LRU_WIDTH

    def prep(c, _):
        base = pl.multiple_of(c * CHUNK, CHUNK)
        first, last = _stream_edges(c, nc_c, nc)
        xc_scr[pl.ds(base, CHUNK), :] = _conv_chunk(u_ref, 0, w, base, first, last, t_tot, cw_ref, cb_ref)
        return 0

    lax.fori_loop(0, nc, prep, 0)

    sub = _iota((CHUNK, w), 0) % 8
    n_tiles = CHUNK // 8

    bodies = []
    for d in range(2):
        coef = -LRU_C * _softplus(-lam_ref[d:d + 1, :])

        def body(i, h_prev, d=d, coef=coef):
            c = _chunk_order(i, nc_c, nc, d == 1)
            base = pl.multiple_of(c * CHUNK, CHUNK)
            xc = xc_scr[pl.ds(base, CHUNK), :]
            gates = _bdot(xc, wg_ref[d]) + bg_ref[d:d + 1, :]
            r = _sigmoid(gates[:, :w])
            ig = _sigmoid(gates[:, w:])
            log_a = r * coef
            a = jnp.exp(log_a)
            b = jnp.sqrt(jnp.maximum(1.0 - jnp.exp(2.0 * log_a), 1e-12)) * (ig * xc)
            for k in (1, 2, 4):
                if d == 0:
                    keep = sub >= k
                    shift = k
                else:
                    keep = sub < 8 - k
                    shift = CHUNK - k
                a_s = jnp.where(keep, pltpu.roll(a, shift, axis=0), 1.0)
                b_s = jnp.where(keep, pltpu.roll(b, shift, axis=0), 0.0)
                b = a * b_s + b
                a = a * a_s
            tiles = range(n_tiles) if d == 0 else range(n_tiles - 1, -1, -1)
            outs = [None] * n_tiles
            for t in tiles:
                h_t = a[t * 8:(t + 1) * 8] * h_prev + b[t * 8:(t + 1) * 8]
                outs[t] = h_t
                h_prev = h_t[7:8] if d == 0 else h_t[0:1]
            h_scr[d, pl.ds(base, CHUNK), :] = jnp.concatenate(outs, axis=0)
            return h_prev

        bodies.append(body)

    def both(i, carry):
        return bodies[0](i, carry[0]), bodies[1](i, carry[1])

    zero = jnp.zeros((1, w), F32)
    lax.fori_loop(0, nc, both, (zero, zero))

    def fin(c, _):
        base = pl.multiple_of(c * CHUNK, CHUNK)
        gate = u_ref[pl.ds(base, CHUNK), w:2 * w]
        out = (h_scr[0, pl.ds(base, CHUNK), :] + h_scr[1, pl.ds(base, CHUNK), :]) * jax.nn.gelu(gate)
        o_ref[pl.ds(base, CHUNK), :] = out.astype(o_ref.dtype)
        return 0

    lax.fori_loop(0, nc, fin, 0)


def _block_diag(wb):
    nb, bi, bj = wb.shape[-3:]
    eye = jnp.eye(nb, dtype=wb.dtype)
    return jnp.einsum('...hij,hg->...higj', wb, eye).reshape(wb.shape[:-3] + (nb * bi, nb * bj))


def _lru(u, conv_w, conv_b, wg, bg, lam, *, layer, n_batch, t_ctx, t_lat):
    t_tot = t_ctx + t_lat
    nc_c, nc = t_ctx // CHUNK, t_tot // CHUNK
    w = LRU_WIDTH
    return pl.pallas_call(
        functools.partial(_lru_kernel, nc_c=nc_c, nc=nc),
        grid=(n_batch,),
        in_specs=[pl.BlockSpec((t_tot, LRU_COLS), lambda b: (b, 0)),
                  _layer_spec((CONV_W, w), layer), _layer_spec((1, w), layer),
                  _layer_spec((2, w, 2 * w), layer), _layer_spec((2, 2 * w), layer), _layer_spec((2, w), layer)],
        out_specs=pl.BlockSpec((t_tot, w), lambda b: (b, 0)),
        out_shape=jax.ShapeDtypeStruct((n_batch * t_tot, w), BF16),
        scratch_shapes=[pltpu.VMEM((t_tot, w), F32), pltpu.VMEM((2, t_tot, w), F32)],
        compiler_params=_cparams(1),
        name="rglru",
    )(u, conv_w, conv_b, wg, bg, lam)


def _hgrn_kernel(q_ref, ff_ref, fb_ref, i_ref, g_ref, lbl_ref, nw_ref, esel_ref, o_ref, of_scr, ob_scr, st_scr, *,
                 layer, nc_c, nc, n_heads):
    n_sub = CHUNK // SUB
    row = _iota((CHUNK, CHUNK), 0)
    col = _iota((CHUNK, CHUNK), 1)
    scale = HGRN_HEADDIM ** -0.5
    log2e = 1.4426950408889634

    lbs = []
    for d in range(2):
        logits = lbl_ref[d]
        ex = jnp.exp(logits - jnp.max(logits, axis=0, keepdims=True))
        sm = ex / jnp.sum(ex, axis=0, keepdims=True)
        lbs.append(jnp.sum(sm[0:layer + 1], axis=0, keepdims=True) - sm[0:1])
    causal = [row >= col, row <= col]
    tri_full = [m.astype(BF16) for m in causal]
    valids = [jnp.logical_and(row // SUB == col // SUB, m) for m in causal]
    levels = (2 * SUB, 4 * SUB, CHUNK)
    q_side = [[((row % g) >= g // 2) if d == 0 else ((row % g) < g // 2) for g in levels] for d in range(2)]
    pair_mask = [[jnp.logical_and(row // g == col // g,
                                  jnp.logical_and(q_side[d][n], ((col % g) < g // 2) if d == 0
                                                  else ((col % g) >= g // 2)))
                  for n, g in enumerate(levels)] for d in range(2)]
    st_scr[...] = jnp.zeros_like(st_scr)

    def chunk(d, hh, base):
        cols = slice(hh * LANES, (hh + 1) * LANES)
        lb = lbs[d][:, cols]
        f_ref = ff_ref if d == 0 else fb_ref
        qh = _silu(q_ref[pl.ds(base, CHUNK), cols]) * scale
        f = lb + (1.0 - lb) * _sigmoid(f_ref[pl.ds(base, CHUNK), cols])
        kk = 1.0 - f
        gl = jnp.log(f) * log2e
        v = i_ref[pl.ds(base, CHUNK), cols]
        cum = _dot_sel(tri_full[d], _split3(gl))
        halves = []
        nb = n_sub // 2
        for r0 in range(0, CHUNK, CHUNK // 2):
            rows = slice(r0, r0 + CHUNK // 2)
            q4 = qh[rows].reshape(nb, 2, 8, LANES)
            c4 = cum[rows].reshape(nb, 2, 8, LANES)
            k4 = kk[rows].reshape(nb, 2, 8, LANES)
            zero = jnp.zeros((nb, 8, LANES), F32)
            acc = None
            for s0 in range(0, SUB, 8):
                ps = []
                for s in range(s0, s0 + 8):
                    ts, rs = s // 8, s % 8
                    cs = c4[:, ts, rs:rs + 1, :]
                    ks = k4[:, ts, rs:rs + 1, :]
                    tiles = []
                    for tt in range(2):
                        if tt == ts:
                            e = jnp.exp2(jnp.minimum(c4[:, tt] - cs, 0.0))
                        elif (tt > ts) == (d == 0):
                            e = jnp.exp2(c4[:, tt] - cs)
                        else:
                            tiles.append(zero)
                            continue
                        tiles.append((q4[:, tt] * e) * ks)
                    ps.append(jnp.stack(tiles, axis=1).reshape(CHUNK // 2, LANES).astype(BF16))
                part = jnp.dot(jnp.concatenate(ps, axis=1), esel_ref[s0 * LANES:(s0 + 8) * LANES, :],
                               preferred_element_type=F32)
                acc = part if acc is None else acc + part
            halves.append(acc)
        scores = jnp.where(valids[d], jnp.concatenate(halves, axis=0), 0.0)
        for n, g in enumerate(levels):
            pieces = []
            for g0 in range(0, CHUNK, g):
                b = g0 + g // 2 - 1 if d == 0 else g0 + g // 2
                pieces.append(jnp.exp2(-jnp.abs(cum[g0:g0 + g] - cum[b:b + 1])))
            w = pieces[0] if len(pieces) == 1 else jnp.concatenate(pieces, axis=0)
            xg = (jnp.where(q_side[d][n], qh, kk) * w).astype(BF16)
            scores = scores + jnp.where(pair_mask[d][n], _bdot_nt(xg, xg), 0.0)
        y = _bdot(scores, v)
        edge = CHUNK - 1 if d == 0 else 0
        tot = cum[edge:edge + 1]
        st = st_scr[d, hh]
        y = y + _bdot_nt(qh * jnp.exp2(cum), st)
        st_scr[d, hh] = st * jnp.exp2(tot) + _bdot(v.T, kk * jnp.exp2(tot - cum))
        return y

    def body(i, _):
        for d in range(2):
            c = _chunk_order(i, nc_c, nc, d == 1)
            base = pl.multiple_of(c * CHUNK, CHUNK)
            o_scr = of_scr if d == 0 else ob_scr
            for hh in range(n_heads):
                o_scr[pl.ds(base, CHUNK), hh * LANES:(hh + 1) * LANES] = chunk(d, hh, base)
        return 0

    lax.fori_loop(0, nc, body, 0)

    def fin(c, _):
        base = pl.multiple_of(c * CHUNK, CHUNK)
        for hh in range(n_heads):
            cols = slice(hh * LANES, (hh + 1) * LANES)
            o = of_scr[pl.ds(base, CHUNK), cols] + ob_scr[pl.ds(base, CHUNK), cols]
            on = o * lax.rsqrt(jnp.mean(o * o, axis=-1, keepdims=True) + EPS) * nw_ref[...]
            out = on * _silu(g_ref[pl.ds(base, CHUNK), cols])
            o_ref[pl.ds(base, CHUNK), cols] = out.astype(o_ref.dtype)
        return 0

    lax.fori_loop(0, nc, fin, 0)


def _hgrn_selector():
    s_of_row = np.arange(SUB * LANES) // LANES
    return jnp.asarray(s_of_row[:, None] == (np.arange(LANES)[None, :] % SUB), dtype=BF16)


def _hgrn(u, lb_logits, norm_w, *, layer, n_batch, t_ctx, t_lat):
    t_tot = t_ctx + t_lat
    nc_c, nc = t_ctx // CHUNK, t_tot // CHUNK
    n_heads = 2
    ng = HGRN_HEADS // n_heads
    wblk = n_heads * LANES
    depth = lb_logits.shape[1]
    col = lambda part: pl.BlockSpec((t_tot, wblk), lambda b, h, part=part: (b, part * ng + h))
    return pl.pallas_call(
        functools.partial(_hgrn_kernel, layer=layer, nc_c=nc_c, nc=nc, n_heads=n_heads),
        grid=(n_batch, ng),
        in_specs=[col(0), col(1), col(2), col(3), col(4),
                  pl.BlockSpec((2, depth, wblk), lambda b, h: (0, 0, h)),
                  _layer_spec((1, LANES), layer),
                  _const_spec((SUB * LANES, LANES))],
        out_specs=pl.BlockSpec((t_tot, wblk), lambda b, h: (b, h)),
        out_shape=jax.ShapeDtypeStruct((n_batch * t_tot, HGRN_WIDTH), BF16),
        scratch_shapes=[pltpu.VMEM((t_tot, wblk), F32), pltpu.VMEM((t_tot, wblk), F32),
                        pltpu.VMEM((2, n_heads, HGRN_HEADDIM, HGRN_HEADDIM), F32)],
        compiler_params=_cparams(2),
        name="hgrn2",
    )(u, u, u, u, u, lb_logits, norm_w, _hgrn_selector())


def _ret_consts(t_ctx, t_lat):
    gam = 1.0 - np.exp2(-5.0 - np.arange(RET_HEADS, dtype=np.float64))
    t = np.arange(CHUNK, dtype=np.float64)
    diff = np.abs(t[:, None] - t[None, :])
    dsym = gam[:, None, None] ** diff[None]
    dsym[:, np.arange(CHUNK), np.arange(CHUNK)] = 2.0
    head_of_lane = np.repeat(np.arange(RET_HEADS), RET_QK_DIM)
    qf = gam[head_of_lane][None, :] ** (t[:, None] + 1.0)
    qb = gam[head_of_lane][None, :] ** (CHUNK - t[:, None])
    kf = gam[:, None] ** (CHUNK - 1.0 - t[None, :])
    kb = gam[:, None] ** t[None, :]
    gtot = gam ** CHUNK
    pos = np.arange(t_lat)
    rows_p, cols_p = pos // GRID_W, pos % GRID_W
    quarter = RET_QK_DIM // 4
    inv = ROPE_BASE ** (-np.arange(quarter, dtype=np.float64) / quarter)
    ang_r = rows_p[:, None] * inv[None, :]
    ang_c = cols_p[:, None] * inv[None, :]
    cos_h = np.concatenate([np.cos(ang_r), np.cos(ang_r), np.cos(ang_c), np.cos(ang_c)], axis=1)
    sin_h = np.concatenate([-np.sin(ang_r), np.sin(ang_r), -np.sin(ang_c), np.sin(ang_c)], axis=1)
    cos_t = np.concatenate([np.ones((t_ctx, RET_QK_DIM)), cos_h], axis=0)
    sin_t = np.concatenate([np.zeros((t_ctx, RET_QK_DIM)), sin_h], axis=0)
    cos_t = np.tile(cos_t, (1, RET_HEADS))
    sin_t = np.tile(sin_t, (1, RET_HEADS))
    f32 = lambda a: jnp.asarray(a, dtype=F32)
    return (f32(dsym), f32(qf), f32(qb), f32(np.concatenate([kf, kb], axis=0)), [float(g) for g in gtot],
            f32(cos_t), f32(sin_t))


def _ret_kernel(u_ref, dsym_ref, qf_ref, qb_ref, kfac_ref, cos_ref, sin_ref, o_ref, o_scr, s_scr, *,
                gtot, nc_c, nc):
    nqk = RET_HEADS * RET_QK_DIM
    lane = _iota((CHUNK, nqk), 1)
    low_half = (lane % (RET_QK_DIM // 2)) < (RET_QK_DIM // 4)
    tile_lane = _iota((CHUNK, LANES), 1)
    head_mask = [(tile_lane // RET_QK_DIM) == p for p in range(2)]
    kscale = RET_QK_DIM ** -0.5

    def rope(x, cos, sin):
        swapped = jnp.where(low_half, pltpu.roll(x, nqk - RET_QK_DIM // 4, axis=1),
                            pltpu.roll(x, RET_QK_DIM // 4, axis=1))
        return x * cos + swapped * sin

    s_scr[...] = jnp.zeros_like(s_scr)
    bodies = []
    for d in range(2):
        qfac = qf_ref if d == 0 else qb_ref

        def body(i, d=d, qfac=qfac):
            c = _chunk_order(i, nc_c, nc, d == 1)
            base = pl.multiple_of(c * CHUNK, CHUNK)
            cos = cos_ref[pl.ds(base, CHUNK), :]
            sin = sin_ref[pl.ds(base, CHUNK), :]
            q = rope(u_ref[pl.ds(base, CHUNK), 0:nqk], cos, sin)
            k = rope(u_ref[pl.ds(base, CHUNK), nqk:2 * nqk], cos, sin) * kscale
            kt = k.T
            qd = q * qfac[...]
            for h in range(RET_HEADS):
                p = h // 2
                v = u_ref[pl.ds(base, CHUNK), 2 * nqk + h * RET_V_DIM:2 * nqk + (h + 1) * RET_V_DIM]
                hm = head_mask[h % 2]
                kt_p = kt[p * LANES:(p + 1) * LANES]
                s_old = s_scr[d, h]
                y = _bdot(jnp.where(hm, qd[:, p * LANES:(p + 1) * LANES], 0.0), s_old)
                kfac = kfac_ref[d * RET_HEADS + h:d * RET_HEADS + h + 1, :]
                s_scr[d, h] = gtot[h] * s_old + _bdot(kt_p * kfac, v)
                if d == 0:
                    raw = _bdot(jnp.where(hm, q[:, p * LANES:(p + 1) * LANES], 0.0), kt_p)
                    y = y + _bdot(raw * dsym_ref[h], v)
                o_scr[d, pl.ds(base, CHUNK), h * RET_V_DIM:(h + 1) * RET_V_DIM] = y

        bodies.append(body)

    def both(i, _):
        bodies[0](i)
        bodies[1](i)
        return 0

    lax.fori_loop(0, nc, both, 0)

    def fin(c, _):
        base = pl.multiple_of(c * CHUNK, CHUNK)
        for h in range(RET_HEADS):
            cols = slice(h * RET_V_DIM, (h + 1) * RET_V_DIM)
            o = o_scr[0, pl.ds(base, CHUNK), cols] + o_scr[1, pl.ds(base, CHUNK), cols]
            on = o * lax.rsqrt(jnp.mean(o * o, axis=-1, keepdims=True) + EPS)
            gate = u_ref[pl.ds(base, CHUNK), 2 * nqk + RET_WIDTH + h * RET_V_DIM:
                         2 * nqk + RET_WIDTH + (h + 1) * RET_V_DIM]
            o_ref[pl.ds(base, CHUNK), cols] = (_silu(gate) * on).astype(o_ref.dtype)
        return 0

    lax.fori_loop(0, nc, fin, 0)


def _ret(u, *, n_batch, t_ctx, t_lat):
    t_tot = t_ctx + t_lat
    nc_c, nc = t_ctx // CHUNK, t_tot // CHUNK
    dsym, qf, qb, kfac, gtot, cos_t, sin_t = _ret_consts(t_ctx, t_lat)
    nqk = RET_HEADS * RET_QK_DIM
    return pl.pallas_call(
        functools.partial(_ret_kernel, gtot=gtot, nc_c=nc_c, nc=nc),
        grid=(n_batch,),
        in_specs=[pl.BlockSpec((t_tot, RET_COLS), lambda b: (b, 0)),
                  _const_spec((RET_HEADS, CHUNK, CHUNK)), _const_spec((CHUNK, nqk)), _const_spec((CHUNK, nqk)),
                  _const_spec((2 * RET_HEADS, CHUNK)), _const_spec((t_tot, nqk)), _const_spec((t_tot, nqk))],
        out_specs=pl.BlockSpec((t_tot, RET_WIDTH), lambda b: (b, 0)),
        out_shape=jax.ShapeDtypeStruct((n_batch * t_tot, RET_WIDTH), BF16),
        scratch_shapes=[pltpu.VMEM((2, t_tot, RET_WIDTH), F32),
                        pltpu.VMEM((2, RET_HEADS, LANES, RET_V_DIM), F32)],
        compiler_params=_cparams(1),
        name="retention",
    )(u, dsym, qf, qb, kfac, cos_t, sin_t)


def kernel(x, c, ctx, c_ctx, w_mod, b_mod, norm_pre, norm_post, ffn_w1, ffn_w3, ffn_w2, w_in, w_out,
           ssd_conv_w, ssd_conv_b, ssd_dt_bias, ssd_a_log, ssd_d, ssd_norm_w,
           lru_conv_w, lru_conv_b, lru_wa, lru_ba, lru_wx, lru_bx, lru_lambda,
           hgrn_lb_logits, hgrn_norm_w):
    n_batch, seq, d = x.shape
    ctx_len = ctx.shape[1]
    depth = w_mod.shape[0]
    lat_rows = n_batch * seq
    all_rows = lat_rows + n_batch * ctx_len
    geom = dict(lat_rows=lat_rows, seq=seq, ctx_len=ctx_len, n_batch=n_batch)
    mix = dict(n_batch=n_batch, t_ctx=ctx_len, t_lat=seq)

    c_all = jnp.concatenate([c, c_ctx[None, :], jnp.zeros((8 - n_batch - 1, d), F32)], axis=0)
    mods = _modulation(c_all, w_mod, b_mod).reshape(depth, 8, N_MOD, d)
    w2 = ffn_w2
    w_in_parts = w_in
    w_out_b = w_out.astype(BF16)
    gpre = norm_pre.reshape(depth * 3, 1, d)
    gpost = norm_post.reshape(depth * 3, 1, d)
    rowvec = lambda v: v.reshape(depth, 1, -1)
    lane_pad = lambda v: jnp.pad(v.reshape(depth, 1, -1), ((0, 0), (0, 0), (0, LANES - v[0].size)))
    ssd_params = (ssd_conv_w, rowvec(ssd_conv_b), lane_pad(ssd_dt_bias), lane_pad(ssd_a_log),
                  rowvec(jnp.repeat(ssd_d, SSD_HEADDIM, axis=-1)), rowvec(ssd_norm_w))
    lru_wg = jnp.concatenate([_block_diag(lru_wa), _block_diag(lru_wx)], axis=-1).astype(BF16)
    lru_bg = jnp.concatenate([lru_ba.reshape(depth, 2, -1), lru_bx.reshape(depth, 2, -1)], axis=-1)
    lru_params = (lru_conv_w, rowvec(lru_conv_b), lru_wg, lru_bg, lru_lambda)
    hgrn_nw = rowvec(hgrn_norm_w)

    rowgeom = dict(lat_rows=lat_rows, seq=seq, n_batch=n_batch)
    xs, h = _prenorm(x.reshape(lat_rows, d), ctx.reshape(n_batch * ctx_len, d), mods, gpre, **rowgeom)
    for l in range(depth):
        last = l == depth - 1
        g = _ffn_up(h, ffn_w1, ffn_w3, layer=l, k=0, n_rows=all_rows)
        xs, h = _ffn_down(g, xs, mods, gpost, gpre, w2, layer=l, k=0, j=0, nxt=3, nxt_layer=l,
                          n_rows=all_rows, **rowgeom)
        ua, ub, uc, ud = _inproj(h, w_in_parts, layer=l, **geom)
        ya = _ssd(ua, *ssd_params, layer=l, **mix)
        yb = _lru(ub, *lru_params, layer=l, **mix)
        yc = _hgrn(uc, hgrn_lb_logits, hgrn_nw, layer=l, **mix)
        yd = _ret(ud, **mix)
        n_rows = lat_rows if last else all_rows
        xs, h = _outproj(xs, (ya, yb, yc, yd), w_out_b, mods, gpost, gpre, layer=l, n_rows=n_rows, **geom)
        g = _ffn_up(h, ffn_w1, ffn_w3, layer=l, k=1, n_rows=n_rows)
        if last:
            (xs,) = _ffn_down(g, xs, mods, gpost, gpre, w2, layer=l, k=1, j=6, nxt=None, nxt_layer=l,
                              n_rows=n_rows, **rowgeom)
        else:
            xs, h = _ffn_down(g, xs, mods, gpost, gpre, w2, layer=l, k=1, j=6, nxt=0, nxt_layer=l + 1,
                              n_rows=n_rows, **rowgeom)
    return xs.reshape(n_batch, seq, d)
```

```python
import functools

import numpy as np
import jax
import jax.numpy as jnp
from jax import lax
from jax.experimental import pallas as pl
from jax.experimental.pallas import tpu as pltpu

F32 = jnp.float32
BF16 = jnp.bfloat16

N_MOD = 9
CONV_W = 4
EPS = 1e-6
GRID_W = 64

SSD_WIDTH = 512
SSD_HEADDIM = 64
SSD_HEADS = 8
SSD_GROUPS = 2
SSD_STATE = 64
SSD_XBC = SSD_WIDTH + 2 * SSD_GROUPS * SSD_STATE
SSD_COLS = SSD_WIDTH + SSD_XBC + 2 * SSD_HEADS
SSD_PAD = 1408
LRU_WIDTH = 512
LRU_C = 8.0
LRU_COLS = 2 * LRU_WIDTH
HGRN_WIDTH = 512
HGRN_HEADDIM = 128
HGRN_HEADS = 4
HGRN_COLS = 5 * HGRN_WIDTH
RET_WIDTH = 512
RET_V_DIM = 128
RET_HEADS = 4
RET_QK_DIM = 64
RET_COLS = 2 * RET_HEADS * RET_QK_DIM + 2 * RET_WIDTH
ROPE_BASE = 10000.0

LANES = 128
CHUNK = 128
SUB = 16
VMEM_LIMIT = 56 * 1024 * 1024


def _cparams(n_axes):
    return pltpu.CompilerParams(dimension_semantics=("arbitrary",) * n_axes,
                                vmem_limit_bytes=VMEM_LIMIT)


def _bdot(a, b):
    return jnp.dot(a.astype(BF16), b.astype(BF16), preferred_element_type=F32)


def _bdot_nt(a, b):
    return lax.dot_general(a.astype(BF16), b.astype(BF16), (((1,), (1,)), ((), ())),
                           preferred_element_type=F32)


def _split3(x):
    hi = x.astype(BF16)
    r1 = x - hi.astype(F32)
    mid = r1.astype(BF16)
    lo = (r1 - mid.astype(F32)).astype(BF16)
    return hi, mid, lo


def _dot_sel(sel, parts):
    return (jnp.dot(sel, parts[0], preferred_element_type=F32)
            + jnp.dot(sel, parts[1], preferred_element_type=F32)
            + jnp.dot(sel, parts[2], preferred_element_type=F32))


def _sigmoid(x):
    return 1.0 / (1.0 + jnp.exp(-x))


def _silu(x):
    return x * _sigmoid(x)


def _softplus(x):
    return jnp.maximum(x, 0.0) + jnp.log(1.0 + jnp.exp(-jnp.abs(x)))


def _iota(shape, dim):
    return lax.broadcasted_iota(jnp.int32, shape, dim)


def _const_spec(shape):
    return pl.BlockSpec(shape, lambda *_: (0,) * len(shape))


def _layer_spec(shape, layer):
    return pl.BlockSpec((None,) + shape, lambda *_: (layer,) + (0,) * len(shape))


def _mod_kernel(c_ref, w_ref, b_ref, o_ref):
    o_ref[...] = _bdot(_silu(c_ref[...]), w_ref[...]) + b_ref[...]


def _modulation(c_all, w_mod, b_mod):
    depth, d, n = w_mod.shape
    tn = 1024
    return pl.pallas_call(
        _mod_kernel,
        grid=(depth, n // tn),
        in_specs=[pl.BlockSpec((8, d), lambda l, j: (0, 0)),
                  pl.BlockSpec((None, d, tn), lambda l, j: (l, 0, j)),
                  pl.BlockSpec((None, 1, tn), lambda l, j: (l, 0, j))],
        out_specs=pl.BlockSpec((None, 8, tn), lambda l, j: (l, 0, j)),
        out_shape=jax.ShapeDtypeStruct((depth, 8, n), F32),
        compiler_params=_cparams(2),
        name="modulation",
    )(c_all, w_mod, b_mod.reshape(depth, 1, n))


def _mod_row_idx(i, tm, layer, lat_rows, seq, n_batch):
    return (layer, jnp.where(i * tm < lat_rows, (i * tm) // seq, n_batch), 0, 0)


def _mod_norm(x, g, shift, scale):
    return (x * lax.rsqrt(jnp.mean(x * x, axis=-1, keepdims=True) + EPS) * (g * (1.0 + scale))
            + shift).astype(BF16)


def _prenorm_kernel(xl_ref, xc_ref, mod_ref, g_ref, x_ref, h_ref, *, n_split):
    x = jnp.where(pl.program_id(0) < n_split, xl_ref[...], xc_ref[...])
    x_ref[...] = x
    h_ref[...] = _mod_norm(x, g_ref[...], mod_ref[0:1, :], mod_ref[1:2, :])


def _prenorm(xl, xc, mods, gpre, *, lat_rows, seq, n_batch):
    d = xl.shape[1]
    rows = xl.shape[0] + xc.shape[0]
    tm = 512
    n_split = xl.shape[0] // tm
    blk = pl.BlockSpec((tm, d), lambda i: (i, 0))
    return pl.pallas_call(
        functools.partial(_prenorm_kernel, n_split=n_split),
        grid=(rows // tm,),
        in_specs=[pl.BlockSpec((tm, d), lambda i: (jnp.minimum(i, n_split - 1), 0)),
                  pl.BlockSpec((tm, d), lambda i: (jnp.maximum(i - n_split, 0), 0)),
                  pl.BlockSpec((None, None, N_MOD, d), lambda i: _mod_row_idx(i, tm, 0, lat_rows, seq, n_batch)),
                  pl.BlockSpec((None, 1, d), lambda i: (0, 0, 0))],
        out_specs=[blk, blk],
        out_shape=[jax.ShapeDtypeStruct((rows, d), F32), jax.ShapeDtypeStruct((rows, d), BF16)],
        compiler_params=_cparams(1),
        name="prenorm",
    )(xl, xc, mods, gpre)


def _ffn_up_kernel(h_ref, w1_ref, w3_ref, g_ref, w1_scr, w3_scr, *, n_sub):
    @pl.when(pl.program_id(1) == 0)
    def _():
        w1_scr[...] = w1_ref[...].astype(BF16)
        w3_scr[...] = w3_ref[...].astype(BF16)

    rows = h_ref.shape[0] // n_sub
    for r in range(n_sub):
        h = h_ref[r * rows:(r + 1) * rows, :]
        a = jnp.dot(h, w1_scr[...], preferred_element_type=F32)
        b = jnp.dot(h, w3_scr[...], preferred_element_type=F32)
        g_ref[r * rows:(r + 1) * rows, :] = (_silu(a) * b).astype(BF16)


def _ffn_up(h, w1, w3, *, layer, k, n_rows):
    d = h.shape[1]
    dff = w1.shape[-1]
    n_sub = 4
    tm, tf = n_rows // 4, 512
    w_spec = pl.BlockSpec((None, None, d, tf), lambda f, i: (layer, k, 0, f))
    return pl.pallas_call(
        functools.partial(_ffn_up_kernel, n_sub=n_sub),
        grid=(dff // tf, n_rows // tm),
        in_specs=[pl.BlockSpec((tm, d), lambda f, i: (i, 0)), w_spec, w_spec],
        out_specs=pl.BlockSpec((tm, tf), lambda f, i: (i, f)),
        out_shape=jax.ShapeDtypeStruct((n_rows, dff), BF16),
        scratch_shapes=[pltpu.VMEM((d, tf), BF16), pltpu.VMEM((d, tf), BF16)],
        compiler_params=_cparams(2),
        name="ffn_up",
    )(h, w1, w3)


EPI_SUB = 2


def _residual_epilogue(rows, x_ref, y, gate, gpost, mod_ref, gnext_ref, nxt, o_ref, hn_ref):
    xn = x_ref[rows, :] + y * lax.rsqrt(jnp.mean(y * y, axis=-1, keepdims=True) + EPS) * (gate * gpost)
    o_ref[rows, :] = xn
    if nxt is not None:
        hn_ref[rows, :] = _mod_norm(xn, gnext_ref[...], mod_ref[nxt:nxt + 1, :], mod_ref[nxt + 1:nxt + 2, :])


def _ffn_down_kernel(*refs, layer, k, j, nxt, nxt_mod_own):
    g_ref, x_ref, mod_ref, gpost_ref = refs[:4]
    pos = 4
    nmod_ref = mod_ref
    gnext_ref = hn_ref = None
    if nxt is not None:
        if not nxt_mod_own:
            nmod_ref = refs[pos]
            pos += 1
        gnext_ref = refs[pos]
        pos += 1
    w2_hbm, o_ref = refs[pos], refs[pos + 1]
    pos += 2
    if nxt is not None:
        hn_ref = refs[pos]
        pos += 1
    w2_scr, stage_scr, sem = refs[pos], refs[pos + 1], refs[pos + 2]

    @pl.when(pl.program_id(0) == 0)
    def _():
        def consume(r0, chunk):
            w2_scr[pl.ds(r0, chunk.shape[0]), :] = chunk.astype(BF16)

        _stage_rows(w2_hbm.at[layer, k], 0, w2_hbm.shape[2], stage_scr.shape[1], stage_scr, sem, consume)

    n = x_ref.shape[0] // EPI_SUB
    for r in range(EPI_SUB):
        rows = slice(r * n, (r + 1) * n)
        y = jnp.dot(g_ref[rows, :], w2_scr[...], preferred_element_type=F32)
        _residual_epilogue(rows, x_ref, y, 0.5 * mod_ref[j + 2:j + 3, :], gpost_ref[...], nmod_ref, gnext_ref,
                           nxt, o_ref, hn_ref)


def _ffn_down(g, x, mods, gpost, gpre, w2, *, layer, k, j, nxt, nxt_layer, n_rows, lat_rows, seq, n_batch):
    d = x.shape[1]
    dff = g.shape[1]
    tm = 256
    norm_row = layer * 3 + (0 if j == 0 else 2)
    blk = pl.BlockSpec((tm, d), lambda i: (i, 0))
    mod_spec = lambda ly: pl.BlockSpec((None, None, N_MOD, d),
                                       lambda i: _mod_row_idx(i, tm, ly, lat_rows, seq, n_batch))
    in_specs = [pl.BlockSpec((tm, dff), lambda i: (i, 0)), blk, mod_spec(layer),
                pl.BlockSpec((None, 1, d), lambda i: (norm_row, 0, 0))]
    args = [g, x, mods, gpost]
    out_specs, out_shape = [blk], [jax.ShapeDtypeStruct((n_rows, d), F32)]
    nxt_mod_own = nxt_layer == layer
    if nxt is not None:
        if not nxt_mod_own:
            in_specs.append(mod_spec(nxt_layer))
            args.append(mods)
        in_specs.append(pl.BlockSpec((None, 1, d), lambda i: (nxt_layer * 3 + nxt // 3, 0, 0)))
        args.append(gpre)
        out_specs.append(blk)
        out_shape.append(jax.ShapeDtypeStruct((n_rows, d), BF16))
    in_specs.append(pl.BlockSpec(memory_space=pl.ANY))
    args.append(w2)
    return pl.pallas_call(
        functools.partial(_ffn_down_kernel, layer=layer, k=k, j=j, nxt=nxt, nxt_mod_own=nxt_mod_own),
        grid=(n_rows // tm,),
        in_specs=in_specs,
        out_specs=out_specs,
        out_shape=out_shape,
        scratch_shapes=[pltpu.VMEM((dff, d), BF16), pltpu.VMEM((2, 512, d), F32), pltpu.SemaphoreType.DMA((2,))],
        compiler_params=_cparams(1),
        name="ffn_down",
    )(*args)


def _stage_rows(src_hbm, row0, n_rows, rpc, stage_scr, sem, consume):
    n_chunks = n_rows // rpc
    assert n_chunks * rpc == n_rows and rpc <= stage_scr.shape[1]

    def copy(c, slot):
        return pltpu.make_async_copy(src_hbm.at[pl.ds(row0 + c * rpc, rpc)],
                                     stage_scr.at[slot, pl.ds(0, rpc)], sem.at[slot])

    copy(0, 0).start()

    def step(c, _):
        slot = c % 2

        @pl.when(c + 1 < n_chunks)
        def _():
            copy(c + 1, 1 - slot).start()

        copy(c, slot).wait()
        consume(pl.multiple_of(c * rpc, 16), stage_scr[slot, pl.ds(0, rpc), :])
        return 0

    lax.fori_loop(0, n_chunks, step, 0)


def _inproj_kernel(h_ref, w_hbm, oa_ref, ob_ref, oc_ref, od_ref, wa_scr, wb_scr, wc_scr, wd_scr, stage_scr, sem, *,
                   layer, splits):
    w_scr = (wa_scr, wb_scr, wc_scr, wd_scr)

    @pl.when(pl.program_id(0) == 0)
    def _():
        for m, scr in enumerate(w_scr):
            n_rows = splits[m + 1] - splits[m]
            rpc = max(r for r in range(16, stage_scr.shape[1] + 1, 16) if n_rows % r == 0)

            def consume(r, chunk, scr=scr):
                scr[pl.ds(r, chunk.shape[0]), :] = chunk.astype(BF16)

            _stage_rows(w_hbm.at[layer], splits[m], n_rows, rpc, stage_scr, sem, consume)
            if scr.shape[0] > n_rows:
                scr[n_rows:, :] = jnp.zeros((scr.shape[0] - n_rows, scr.shape[1]), BF16)

    h = h_ref[...]
    for w, o_ref in zip(w_scr, (oa_ref, ob_ref, oc_ref, od_ref)):
        o_ref[...] = lax.dot_general(h, w[...], (((1,), (1,)), ((), ())), preferred_element_type=F32)


def _x_to_b_block(i, tm, lat_rows, seq, ctx_len):
    per_b = (seq + ctx_len) // tm
    lat_b = seq // tm
    ctx_b = ctx_len // tm
    n_lat = lat_rows // tm
    lat_idx = (i // lat_b) * per_b + ctx_b + i % lat_b
    ic = i - n_lat
    ctx_idx = (ic // ctx_b) * per_b + ic % ctx_b
    return jnp.where(i < n_lat, lat_idx, ctx_idx)


def _inproj(h, w_in, *, layer, lat_rows, seq, ctx_len, n_batch):
    rows, d = h.shape
    tm = 256
    stage_rows = 512
    splits = (0, SSD_COLS, SSD_COLS + LRU_COLS, SSD_COLS + LRU_COLS + HGRN_COLS, w_in.shape[1])
    widths = [SSD_PAD, LRU_COLS, HGRN_COLS, RET_COLS]

    def out_idx(i):
        return (_x_to_b_block(i, tm, lat_rows, seq, ctx_len), 0)

    return pl.pallas_call(
        functools.partial(_inproj_kernel, layer=layer, splits=splits),
        grid=(rows // tm,),
        in_specs=[pl.BlockSpec((tm, d), lambda i: (i, 0)), pl.BlockSpec(memory_space=pl.ANY)],
        out_specs=[pl.BlockSpec((tm, n), out_idx) for n in widths],
        out_shape=[jax.ShapeDtypeStruct((rows, n), F32) for n in widths],
        scratch_shapes=[pltpu.VMEM((n, d), BF16) for n in widths]
        + [pltpu.VMEM((2, stage_rows, d), F32), pltpu.SemaphoreType.DMA((2,))],
        compiler_params=_cparams(1),
        name="inproj",
    )(h, w_in)


def _outproj_kernel(x_ref, ya_ref, yb_ref, yc_ref, yd_ref, w_ref, mod_ref, gpost_ref, gnext_ref, o_ref, hn_ref):
    w = LRU_WIDTH
    n = x_ref.shape[0] // EPI_SUB
    for r in range(EPI_SUB):
        rows = slice(r * n, (r + 1) * n)
        y = (jnp.dot(ya_ref[rows, :], w_ref[0 * w:1 * w, :], preferred_element_type=F32)
             + jnp.dot(yb_ref[rows, :], w_ref[1 * w:2 * w, :], preferred_element_type=F32)
             + jnp.dot(yc_ref[rows, :], w_ref[2 * w:3 * w, :], preferred_element_type=F32)
             + jnp.dot(yd_ref[rows, :], w_ref[3 * w:4 * w, :], preferred_element_type=F32))
        _residual_epilogue(rows, x_ref, y, mod_ref[5:6, :], gpost_ref[...], mod_ref, gnext_ref, 6, o_ref, hn_ref)


def _outproj(x, ys, w, mods, gpost, gpre, *, layer, n_rows, lat_rows, seq, ctx_len, n_batch):
    d = x.shape[1]
    tm = 256
    wy = ys[0].shape[1]
    blk = pl.BlockSpec((tm, d), lambda i: (i, 0))

    def y_idx(i):
        return (_x_to_b_block(i, tm, lat_rows, seq, ctx_len), 0)

    return pl.pallas_call(
        _outproj_kernel,
        grid=(n_rows // tm,),
        in_specs=[blk] + [pl.BlockSpec((tm, wy), y_idx)] * 4
        + [_layer_spec((4 * wy, d), layer),
           pl.BlockSpec((None, None, N_MOD, d), lambda i: _mod_row_idx(i, tm, layer, lat_rows, seq, n_batch)),
           pl.BlockSpec((None, 1, d), lambda i: (layer * 3 + 1, 0, 0)),
           pl.BlockSpec((None, 1, d), lambda i: (layer * 3 + 2, 0, 0))],
        out_specs=[blk, blk],
        out_shape=[jax.ShapeDtypeStruct((n_rows, d), F32), jax.ShapeDtypeStruct((n_rows, d), BF16)],
        compiler_params=_cparams(1),
        name="outproj",
    )(x, *ys, w, mods, gpost, gpre)


def _conv_chunk(u_ref, col0, ncols, base, first, last, t_tot, w_ref, b_ref):
    cols = slice(col0, col0 + ncols)
    prev = u_ref[pl.ds(pl.multiple_of(jnp.maximum(base - 8, 0), 8), 8), cols]
    nxt = u_ref[pl.ds(pl.multiple_of(jnp.minimum(base + CHUNK, t_tot - 8), 8), 8), cols]
    cur = u_ref[pl.ds(base, CHUNK), cols]
    prev = jnp.where(first, 0.0, prev)
    nxt = jnp.where(last, 0.0, nxt)
    win = jnp.concatenate([prev, cur, nxt], axis=0)
    n = CHUNK + 16
    xm1 = pltpu.roll(win, 1, axis=0)[8:8 + CHUNK]
    xp1 = pltpu.roll(win, n - 1, axis=0)[8:8 + CHUNK]
    xp2 = pltpu.roll(win, n - 2, axis=0)[8:8 + CHUNK]
    return (w_ref[0:1, :] * xm1 + w_ref[1:2, :] * cur + w_ref[2:3, :] * xp1 + w_ref[3:4, :] * xp2
            + b_ref[...])


def _chunk_order(i, nc_c, nc, reverse):
    if not reverse:
        return i
    return jnp.where(i < nc_c, nc_c - 1 - i, nc + nc_c - 1 - i)


def _stream_edges(c, nc_c, nc):
    first = jnp.logical_or(c == 0, c == nc_c)
    last = jnp.logical_or(c == nc_c - 1, c == nc - 1)
    return first, last


def _ssd_kernel(u_ref, cw_ref, cb_ref, dtb_ref, alog_ref, dskip_ref, nw_ref, o_ref,
                act_scr, dl_scr, y_scr, s_scr, *, nc_c, nc):
    t_tot = nc * CHUNK
    xbc0 = SSD_WIDTH
    dt0 = SSD_WIDTH + SSD_XBC

    def prep(c, _):
        base = pl.multiple_of(c * CHUNK, CHUNK)
        first, last = _stream_edges(c, nc_c, nc)
        conv = _conv_chunk(u_ref, xbc0, SSD_XBC, base, first, last, t_tot, cw_ref, cb_ref)
        act_scr[pl.ds(base, CHUNK), :] = _silu(conv)
        dt = u_ref[pl.ds(base, CHUNK), dt0:dt0 + LANES]
        dl_scr[pl.ds(base, CHUNK), :] = _softplus(dt + dtb_ref[...])
        return 0

    lax.fori_loop(0, nc, prep, 0)

    a_neg = -jnp.exp(alog_ref[...])
    row = _iota((CHUNK, CHUNK), 0)
    col = _iota((CHUNK, CHUNK), 1)
    lane_lo = _iota((CHUNK, LANES), 1) < SSD_HEADDIM
    grp_mask = [(_iota((CHUNK, LANES), 1) // SSD_STATE) == g for g in range(SSD_GROUPS)]

    s_scr[...] = jnp.zeros_like(s_scr)
    bodies = []
    for d in range(2):
        tri = (row >= col) if d == 0 else (row <= col)
        tri_b = tri.astype(BF16)
        edge = CHUNK - 1 if d == 0 else 0

        def body(i, d=d, tri=tri, tri_b=tri_b, edge=edge):
            c = _chunk_order(i, nc_c, nc, d == 1)
            base = pl.multiple_of(c * CHUNK, CHUNK)
            act = act_scr[pl.ds(base, CHUNK), :]
            bm = act[:, SSD_WIDTH:SSD_WIDTH + LANES]
            cm = act[:, SSD_WIDTH + LANES:SSD_WIDTH + 2 * LANES]
            delta = dl_scr[pl.ds(base, CHUNK), :]
            la = delta * a_neg
            cum_col = _dot_sel(tri_b, _split3(la))
            cum_row = cum_col.T
            delta_row = delta.T
            bt = bm.T
            gmat = [_bdot(jnp.where(grp_mask[g], cm, 0.0), bt) for g in range(SSD_GROUPS)]
            for k in range(SSD_HEADS // 2):
                g = (2 * k) // (SSD_HEADS // SSD_GROUPS)
                x_tile = act[:, k * LANES:(k + 1) * LANES]
                cg = jnp.where(grp_mask[g], cm, 0.0)
                m_list, w_list, e_list, tot_list = [], [], [], []
                for h in (2 * k, 2 * k + 1):
                    j = d * SSD_HEADS + h
                    ccol = cum_col[:, j:j + 1]
                    crow = cum_row[j:j + 1, :]
                    drow = delta_row[j:j + 1, :]
                    dec = jnp.where(tri, jnp.exp(jnp.where(tri, ccol - crow, 0.0)), 0.0)
                    m_list.append(gmat[g] * dec * drow)
                    tot = cum_row[j:j + 1, edge:edge + 1]
                    w_list.append(bt * (drow * jnp.exp(tot - crow)))
                    e_list.append(jnp.exp(ccol))
                    tot_list.append(jnp.exp(tot))
                y2 = _bdot(jnp.concatenate(m_list, axis=0), x_tile)
                u2 = _bdot(jnp.concatenate(w_list, axis=0), x_tile)
                s_old = s_scr[d, :, k * LANES:(k + 1) * LANES]
                y_inter = _bdot(cg, s_old) * jnp.where(lane_lo, e_list[0], e_list[1])
                y_tile = jnp.where(lane_lo, y2[:CHUNK], y2[CHUNK:]) + y_inter
                s_scr[d, :, k * LANES:(k + 1) * LANES] = (
                    s_old * jnp.where(lane_lo, tot_list[0], tot_list[1])
                    + jnp.where(lane_lo, u2[:CHUNK], u2[CHUNK:]))
                if d == 0:
                    y_tile = y_tile + x_tile * dskip_ref[:, k * LANES:(k + 1) * LANES]
                y_scr[d, pl.ds(base, CHUNK), k * LANES:(k + 1) * LANES] = y_tile

        bodies.append(body)

    def both(i, _):
        bodies[0](i)
        bodies[1](i)
        return 0

    lax.fori_loop(0, nc, both, 0)

    def fin(c, _):
        base = pl.multiple_of(c * CHUNK, CHUNK)
        z = u_ref[pl.ds(base, CHUNK), 0:SSD_WIDTH]
        y = (y_scr[0, pl.ds(base, CHUNK), :] + y_scr[1, pl.ds(base, CHUNK), :]) * _silu(z)
        yn = y * lax.rsqrt(jnp.mean(y * y, axis=-1, keepdims=True) + EPS) * nw_ref[...]
        o_ref[pl.ds(base, CHUNK), :] = yn.astype(o_ref.dtype)
        return 0

    lax.fori_loop(0, nc, fin, 0)


def _ssd(u, conv_w, conv_b, dt_bias, a_log, d_skip, norm_w, *, layer, n_batch, t_ctx, t_lat):
    t_tot = t_ctx + t_lat
    nc_c, nc = t_ctx // CHUNK, t_tot // CHUNK
    return pl.pallas_call(
        functools.partial(_ssd_kernel, nc_c=nc_c, nc=nc),
        grid=(n_batch,),
        in_specs=[pl.BlockSpec((t_tot, SSD_PAD), lambda b: (b, 0)),
                  _layer_spec((CONV_W, SSD_XBC), layer), _layer_spec((1, SSD_XBC), layer),
                  _layer_spec((1, LANES), layer), _layer_spec((1, LANES), layer),
                  _layer_spec((1, SSD_WIDTH), layer), _layer_spec((1, SSD_WIDTH), layer)],
        out_specs=pl.BlockSpec((t_tot, SSD_WIDTH), lambda b: (b, 0)),
        out_shape=jax.ShapeDtypeStruct((n_batch * t_tot, SSD_WIDTH), BF16),
        scratch_shapes=[pltpu.VMEM((t_tot, SSD_XBC), F32), pltpu.VMEM((t_tot, LANES), F32),
                        pltpu.VMEM((2, t_tot, SSD_WIDTH), F32), pltpu.VMEM((2, CHUNK, SSD_WIDTH), F32)],
        compiler_params=_cparams(1),
        name="ssd",
    )(u, conv_w, conv_b, dt_bias, a_log, d_skip, norm_w)


def _lru_kernel(u_ref, cw_ref, cb_ref, wg_ref, bg_ref, lam_ref, o_ref, xc_scr, h_scr, *, nc_c, nc):
    t_tot = nc * CHUNK
    w = LRU_WIDTH

    def prep(c, _):
        base = pl.multiple_of(c * CHUNK, CHUNK)
        first, last = _stream_edges(c, nc_c, nc)
        xc_scr[pl.ds(base, CHUNK), :] = _conv_chunk(u_ref, 0, w, base, first, last, t_tot, cw_ref, cb_ref)
        return 0

    lax.fori_loop(0, nc, prep, 0)

    sub = _iota((CHUNK, w), 0) % 8
    n_tiles = CHUNK // 8

    bodies = []
    for d in range(2):
        coef = -LRU_C * _softplus(-lam_ref[d:d + 1, :])

        def body(i, h_prev, d=d, coef=coef):
            c = _chunk_order(i, nc_c, nc, d == 1)
            base = pl.multiple_of(c * CHUNK, CHUNK)
            xc = xc_scr[pl.ds(base, CHUNK), :]
            gates = _bdot(xc, wg_ref[d]) + bg_ref[d:d + 1, :]
            r = _sigmoid(gates[:, :w])
            ig = _sigmoid(gates[:, w:])
            log_a = r * coef
            a = jnp.exp(log_a)
            b = jnp.sqrt(jnp.maximum(1.0 - jnp.exp(2.0 * log_a), 1e-12)) * (ig * xc)
            for k in (1, 2, 4):
                if d == 0:
                    keep = sub >= k
                    shift = k
                else:
                    keep = sub < 8 - k
                    shift = CHUNK - k
                a_s = jnp.where(keep, pltpu.roll(a, shift, axis=0), 1.0)
                b_s = jnp.where(keep, pltpu.roll(b, shift, axis=0), 0.0)
                b = a * b_s + b
                a = a * a_s
            tiles = range(n_tiles) if d == 0 else range(n_tiles - 1, -1, -1)
            outs = [None] * n_tiles
            for t in tiles:
                h_t = a[t * 8:(t + 1) * 8] * h_prev + b[t * 8:(t + 1) * 8]
                outs[t] = h_t
                h_prev = h_t[7:8] if d == 0 else h_t[0:1]
            h_scr[d, pl.ds(base, CHUNK), :] = jnp.concatenate(outs, axis=0)
            return h_prev

        bodies.append(body)

    def both(i, carry):
        return bodies[0](i, carry[0]), bodies[1](i, carry[1])

    zero = jnp.zeros((1, w), F32)
    lax.fori_loop(0, nc, both, (zero, zero))

    def fin(c, _):
        base = pl.multiple_of(c * CHUNK, CHUNK)
        gate = u_ref[pl.ds(base, CHUNK), w:2 * w]
        out = (h_scr[0, pl.ds(base, CHUNK), :] + h_scr[1, pl.ds(base, CHUNK), :]) * jax.nn.gelu(gate)
        o_ref[pl.ds(base, CHUNK), :] = out.astype(o_ref.dtype)
        return 0

    lax.fori_loop(0, nc, fin, 0)


def _block_diag(wb):
    nb, bi, bj = wb.shape[-3:]
    eye = jnp.eye(nb, dtype=wb.dtype)
    return jnp.einsum('...hij,hg->...higj', wb, eye).reshape(wb.shape[:-3] + (nb * bi, nb * bj))


def _lru(u, conv_w, conv_b, wg, bg, lam, *, layer, n_batch, t_ctx, t_lat):
    t_tot = t_ctx + t_lat
    nc_c, nc = t_ctx // CHUNK, t_tot // CHUNK
    w = LRU_WIDTH
    return pl.pallas_call(
        functools.partial(_lru_kernel, nc_c=nc_c, nc=nc),
        grid=(n_batch,),
        in_specs=[pl.BlockSpec((t_tot, LRU_COLS), lambda b: (b, 0)),
                  _layer_spec((CONV_W, w), layer), _layer_spec((1, w), layer),
                  _layer_spec((2, w, 2 * w), layer), _layer_spec((2, 2 * w), layer), _layer_spec((2, w), layer)],
        out_specs=pl.BlockSpec((t_tot, w), lambda b: (b, 0)),
        out_shape=jax.ShapeDtypeStruct((n_batch * t_tot, w), BF16),
        scratch_shapes=[pltpu.VMEM((t_tot, w), F32), pltpu.VMEM((2, t_tot, w), F32)],
        compiler_params=_cparams(1),
        name="rglru",
    )(u, conv_w, conv_b, wg, bg, lam)


def _hgrn_kernel(q_ref, ff_ref, fb_ref, i_ref, g_ref, lbl_ref, nw_ref, esel_ref, o_ref, of_scr, ob_scr, st_scr, *,
                 layer, nc_c, nc, n_heads):
    n_sub = CHUNK // SUB
    row = _iota((CHUNK, CHUNK), 0)
    col = _iota((CHUNK, CHUNK), 1)
    scale = HGRN_HEADDIM ** -0.5
    log2e = 1.4426950408889634

    lbs = []
    for d in range(2):
        logits = lbl_ref[d]
        ex = jnp.exp(logits - jnp.max(logits, axis=0, keepdims=True))
        sm = ex / jnp.sum(ex, axis=0, keepdims=True)
        lbs.append(jnp.sum(sm[0:layer + 1], axis=0, keepdims=True) - sm[0:1])
    causal = [row >= col, row <= col]
    tri_full = [m.astype(BF16) for m in causal]
    valids = [jnp.logical_and(row // SUB == col // SUB, m) for m in causal]
    levels = (2 * SUB, 4 * SUB, CHUNK)
    q_side = [[((row % g) >= g // 2) if d == 0 else ((row % g) < g // 2) for g in levels] for d in range(2)]
    pair_mask = [[jnp.logical_and(row // g == col // g,
                                  jnp.logical_and(q_side[d][n], ((col % g) < g // 2) if d == 0
                                                  else ((col % g) >= g // 2)))
                  for n, g in enumerate(levels)] for d in range(2)]
    st_scr[...] = jnp.zeros_like(st_scr)

    def chunk(d, hh, base):
        cols = slice(hh * LANES, (hh + 1) * LANES)
        lb = lbs[d][:, cols]
        f_ref = ff_ref if d == 0 else fb_ref
        qh = _silu(q_ref[pl.ds(base, CHUNK), cols]) * scale
        f = lb + (1.0 - lb) * _sigmoid(f_ref[pl.ds(base, CHUNK), cols])
        kk = 1.0 - f
        gl = jnp.log(f) * log2e
        v = i_ref[pl.ds(base, CHUNK), cols]
        cum = _dot_sel(tri_full[d], _split3(gl))
        halves = []
        nb = n_sub // 2
        for r0 in range(0, CHUNK, CHUNK // 2):
            rows = slice(r0, r0 + CHUNK // 2)
            q4 = qh[rows].reshape(nb, 2, 8, LANES)
            c4 = cum[rows].reshape(nb, 2, 8, LANES)
            k4 = kk[rows].reshape(nb, 2, 8, LANES)
            zero = jnp.zeros((nb, 8, LANES), F32)
            acc = None
            for s0 in range(0, SUB, 8):
                ps = []
                for s in range(s0, s0 + 8):
                    ts, rs = s // 8, s % 8
                    cs = c4[:, ts, rs:rs + 1, :]
                    ks = k4[:, ts, rs:rs + 1, :]
                    tiles = []
                    for tt in range(2):
                        if tt == ts:
                            e = jnp.exp2(jnp.minimum(c4[:, tt] - cs, 0.0))
                        elif (tt > ts) == (d == 0):
                            e = jnp.exp2(c4[:, tt] - cs)
                        else:
                            tiles.append(zero)
                            continue
                        tiles.append((q4[:, tt] * e) * ks)
                    ps.append(jnp.stack(tiles, axis=1).reshape(CHUNK // 2, LANES).astype(BF16))
                part = jnp.dot(jnp.concatenate(ps, axis=1), esel_ref[s0 * LANES:(s0 + 8) * LANES, :],
                               preferred_element_type=F32)
                acc = part if acc is None else acc + part
            halves.append(acc)
        scores = jnp.where(valids[d], jnp.concatenate(halves, axis=0), 0.0)
        for n, g in enumerate(levels):
            pieces = []
            for g0 in range(0, CHUNK, g):
                b = g0 + g // 2 - 1 if d == 0 else g0 + g // 2
                pieces.append(jnp.exp2(-jnp.abs(cum[g0:g0 + g] - cum[b:b + 1])))
            w = pieces[0] if len(pieces) == 1 else jnp.concatenate(pieces, axis=0)
            xg = (jnp.where(q_side[d][n], qh, kk) * w).astype(BF16)
            scores = scores + jnp.where(pair_mask[d][n], _bdot_nt(xg, xg), 0.0)
        y = _bdot(scores, v)
        edge = CHUNK - 1 if d == 0 else 0
        tot = cum[edge:edge + 1]
        st = st_scr[d, hh]
        y = y + _bdot_nt(qh * jnp.exp2(cum), st)
        st_scr[d, hh] = st * jnp.exp2(tot) + _bdot(v.T, kk * jnp.exp2(tot - cum))
        return y

    def body(i, _):
        for d in range(2):
            c = _chunk_order(i, nc_c, nc, d == 1)
            base = pl.multiple_of(c * CHUNK, CHUNK)
            o_scr = of_scr if d == 0 else ob_scr
            for hh in range(n_heads):
                o_scr[pl.ds(base, CHUNK), hh * LANES:(hh + 1) * LANES] = chunk(d, hh, base)
        return 0

    lax.fori_loop(0, nc, body, 0)

    def fin(c, _):
        base = pl.multiple_of(c * CHUNK, CHUNK)
        for hh in range(n_heads):
            cols = slice(hh * LANES, (hh + 1) * LANES)
            o = of_scr[pl.ds(base, CHUNK), cols] + ob_scr[pl.ds(base, CHUNK), cols]
            on = o * lax.rsqrt(jnp.mean(o * o, axis=-1, keepdims=True) + EPS) * nw_ref[...]
            out = on * _silu(g_ref[pl.ds(base, CHUNK), cols])
            o_ref[pl.ds(base, CHUNK), cols] = out.astype(o_ref.dtype)
        return 0

    lax.fori_loop(0, nc, fin, 0)


def _hgrn_selector():
    s_of_row = np.arange(SUB * LANES) // LANES
    return jnp.asarray(s_of_row[:, None] == (np.arange(LANES)[None, :] % SUB), dtype=BF16)


def _hgrn(u, lb_logits, norm_w, *, layer, n_batch, t_ctx, t_lat):
    t_tot = t_ctx + t_lat
    nc_c, nc = t_ctx // CHUNK, t_tot // CHUNK
    n_heads = 2
    ng = HGRN_HEADS // n_heads
    wblk = n_heads * LANES
    depth = lb_logits.shape[1]
    col = lambda part: pl.BlockSpec((t_tot, wblk), lambda b, h, part=part: (b, part * ng + h))
    return pl.pallas_call(
        functools.partial(_hgrn_kernel, layer=layer, nc_c=nc_c, nc=nc, n_heads=n_heads),
        grid=(n_batch, ng),
        in_specs=[col(0), col(1), col(2), col(3), col(4),
                  pl.BlockSpec((2, depth, wblk), lambda b, h: (0, 0, h)),
                  _layer_spec((1, LANES), layer),
                  _const_spec((SUB * LANES, LANES))],
        out_specs=pl.BlockSpec((t_tot, wblk), lambda b, h: (b, h)),
        out_shape=jax.ShapeDtypeStruct((n_batch * t_tot, HGRN_WIDTH), BF16),
        scratch_shapes=[pltpu.VMEM((t_tot, wblk), F32), pltpu.VMEM((t_tot, wblk), F32),
                        pltpu.VMEM((2, n_heads, HGRN_HEADDIM, HGRN_HEADDIM), F32)],
        compiler_params=_cparams(2),
        name="hgrn2",
    )(u, u, u, u, u, lb_logits, norm_w, _hgrn_selector())


def _ret_consts(t_ctx, t_lat):
    gam = 1.0 - np.exp2(-5.0 - np.arange(RET_HEADS, dtype=np.float64))
    t = np.arange(CHUNK, dtype=np.float64)
    diff = np.abs(t[:, None] - t[None, :])
    dsym = gam[:, None, None] ** diff[None]
    dsym[:, np.arange(CHUNK), np.arange(CHUNK)] = 2.0
    head_of_lane = np.repeat(np.arange(RET_HEADS), RET_QK_DIM)
    qf = gam[head_of_lane][None, :] ** (t[:, None] + 1.0)
    qb = gam[head_of_lane][None, :] ** (CHUNK - t[:, None])
    kf = gam[:, None] ** (CHUNK - 1.0 - t[None, :])
    kb = gam[:, None] ** t[None, :]
    gtot = gam ** CHUNK
    pos = np.arange(t_lat)
    rows_p, cols_p = pos // GRID_W, pos % GRID_W
    quarter = RET_QK_DIM // 4
    inv = ROPE_BASE ** (-np.arange(quarter, dtype=np.float64) / quarter)
    ang_r = rows_p[:, None] * inv[None, :]
    ang_c = cols_p[:, None] * inv[None, :]
    cos_h = np.concatenate([np.cos(ang_r), np.cos(ang_r), np.cos(ang_c), np.cos(ang_c)], axis=1)
    sin_h = np.concatenate([-np.sin(ang_r), np.sin(ang_r), -np.sin(ang_c), np.sin(ang_c)], axis=1)
    cos_t = np.concatenate([np.ones((t_ctx, RET_QK_DIM)), cos_h], axis=0)
    sin_t = np.concatenate([np.zeros((t_ctx, RET_QK_DIM)), sin_h], axis=0)
    cos_t = np.tile(cos_t, (1, RET_HEADS))
    sin_t = np.tile(sin_t, (1, RET_HEADS))
    f32 = lambda a: jnp.asarray(a, dtype=F32)
    return (f32(dsym), f32(qf), f32(qb), f32(np.concatenate([kf, kb], axis=0)), [float(g) for g in gtot],
            f32(cos_t), f32(sin_t))


def _ret_kernel(u_ref, dsym_ref, qf_ref, qb_ref, kfac_ref, cos_ref, sin_ref, o_ref, o_scr, s_scr, q_scr, kt_scr, *,
                gtot, nc_c, nc):
    nqk = RET_HEADS * RET_QK_DIM
    lane = _iota((CHUNK, nqk), 1)
    low_half = (lane % (RET_QK_DIM // 2)) < (RET_QK_DIM // 4)
    tile_lane = _iota((CHUNK, LANES), 1)
    head_mask = [(tile_lane // RET_QK_DIM) == p for p in range(2)]
    kscale = RET_QK_DIM ** -0.5

    def rope(x, cos, sin):
        swapped = jnp.where(low_half, pltpu.roll(x, nqk - RET_QK_DIM // 4, axis=1),
                            pltpu.roll(x, RET_QK_DIM // 4, axis=1))
        return x * cos + swapped * sin

    def v_of(base, h):
        return u_ref[pl.ds(base, CHUNK), 2 * nqk + h * RET_V_DIM:2 * nqk + (h + 1) * RET_V_DIM]

    def prep(c):
        base = pl.multiple_of(c * CHUNK, CHUNK)
        cos = cos_ref[pl.ds(base, CHUNK), :]
        sin = sin_ref[pl.ds(base, CHUNK), :]
        q = rope(u_ref[pl.ds(base, CHUNK), 0:nqk], cos, sin)
        k = rope(u_ref[pl.ds(base, CHUNK), nqk:2 * nqk], cos, sin) * kscale
        kt = k.T
        q_scr[pl.ds(base, CHUNK), :] = q
        kt_scr[c] = kt
        for h in range(RET_HEADS):
            p = h // 2
            raw = _bdot(jnp.where(head_mask[h % 2], q[:, p * LANES:(p + 1) * LANES], 0.0),
                        kt[p * LANES:(p + 1) * LANES])
            o_scr[2, pl.ds(base, CHUNK), h * RET_V_DIM:(h + 1) * RET_V_DIM] = _bdot(raw * dsym_ref[h], v_of(base, h))

    def prep_pair(i, _):
        prep(2 * i)
        prep(2 * i + 1)
        return 0

    assert nc % 2 == 0
    lax.fori_loop(0, nc // 2, prep_pair, 0)

    s_scr[...] = jnp.zeros_like(s_scr)
    bodies = []
    for d in range(2):
        qfac = qf_ref if d == 0 else qb_ref

        def body(i, d=d, qfac=qfac):
            c = _chunk_order(i, nc_c, nc, d == 1)
            base = pl.multiple_of(c * CHUNK, CHUNK)
            qd = q_scr[pl.ds(base, CHUNK), :] * qfac[...]
            kt = kt_scr[c]
            for h in range(RET_HEADS):
                p = h // 2
                kt_p = kt[p * LANES:(p + 1) * LANES]
                s_old = s_scr[d, h]
                y = _bdot(jnp.where(head_mask[h % 2], qd[:, p * LANES:(p + 1) * LANES], 0.0), s_old)
                kfac = kfac_ref[d * RET_HEADS + h:d * RET_HEADS + h + 1, :]
                s_scr[d, h] = gtot[h] * s_old + _bdot(kt_p * kfac, v_of(base, h))
                o_scr[d, pl.ds(base, CHUNK), h * RET_V_DIM:(h + 1) * RET_V_DIM] = y

        bodies.append(body)

    def both(i, _):
        bodies[0](i)
        bodies[1](i)
        return 0

    lax.fori_loop(0, nc, both, 0)

    def fin(c, _):
        base = pl.multiple_of(c * CHUNK, CHUNK)
        for h in range(RET_HEADS):
            cols = slice(h * RET_V_DIM, (h + 1) * RET_V_DIM)
            o = (o_scr[0, pl.ds(base, CHUNK), cols] + o_scr[1, pl.ds(base, CHUNK), cols]
                 + o_scr[2, pl.ds(base, CHUNK), cols])
            on = o * lax.rsqrt(jnp.mean(o * o, axis=-1, keepdims=True) + EPS)
            gate = u_ref[pl.ds(base, CHUNK), 2 * nqk + RET_WIDTH + h * RET_V_DIM:
                         2 * nqk + RET_WIDTH + (h + 1) * RET_V_DIM]
            o_ref[pl.ds(base, CHUNK), cols] = (_silu(gate) * on).astype(o_ref.dtype)
        return 0

    lax.fori_loop(0, nc, fin, 0)


def _ret(u, *, n_batch, t_ctx, t_lat):
    t_tot = t_ctx + t_lat
    nc_c, nc = t_ctx // CHUNK, t_tot // CHUNK
    dsym, qf, qb, kfac, gtot, cos_t, sin_t = _ret_consts(t_ctx, t_lat)
    nqk = RET_HEADS * RET_QK_DIM
    return pl.pallas_call(
        functools.partial(_ret_kernel, gtot=gtot, nc_c=nc_c, nc=nc),
        grid=(n_batch,),
        in_specs=[pl.BlockSpec((t_tot, RET_COLS), lambda b: (b, 0)),
                  _const_spec((RET_HEADS, CHUNK, CHUNK)), _const_spec((CHUNK, nqk)), _const_spec((CHUNK, nqk)),
                  _const_spec((2 * RET_HEADS, CHUNK)), _const_spec((t_tot, nqk)), _const_spec((t_tot, nqk))],
        out_specs=pl.BlockSpec((t_tot, RET_WIDTH), lambda b: (b, 0)),
        out_shape=jax.ShapeDtypeStruct((n_batch * t_tot, RET_WIDTH), BF16),
        scratch_shapes=[pltpu.VMEM((3, t_tot, RET_WIDTH), F32),
                        pltpu.VMEM((2, RET_HEADS, LANES, RET_V_DIM), F32),
                        pltpu.VMEM((t_tot, nqk), F32), pltpu.VMEM((nc, nqk, CHUNK), F32)],
        compiler_params=_cparams(1),
        name="retention",
    )(u, dsym, qf, qb, kfac, cos_t, sin_t)


def kernel(x, c, ctx, c_ctx, w_mod, b_mod, norm_pre, norm_post, ffn_w1, ffn_w3, ffn_w2, w_in, w_out,
           ssd_conv_w, ssd_conv_b, ssd_dt_bias, ssd_a_log, ssd_d, ssd_norm_w,
           lru_conv_w, lru_conv_b, lru_wa, lru_ba, lru_wx, lru_bx, lru_lambda,
           hgrn_lb_logits, hgrn_norm_w):
    n_batch, seq, d = x.shape
    ctx_len = ctx.shape[1]
    depth = w_mod.shape[0]
    lat_rows = n_batch * seq
    all_rows = lat_rows + n_batch * ctx_len
    geom = dict(lat_rows=lat_rows, seq=seq, ctx_len=ctx_len, n_batch=n_batch)
    mix = dict(n_batch=n_batch, t_ctx=ctx_len, t_lat=seq)

    c_all = jnp.concatenate([c, c_ctx[None, :], jnp.zeros((8 - n_batch - 1, d), F32)], axis=0)
    mods = _modulation(c_all, w_mod, b_mod).reshape(depth, 8, N_MOD, d)
    w2 = ffn_w2
    w_in_parts = jnp.swapaxes(w_in, 1, 2)
    w_out_b = w_out.astype(BF16)
    gpre = norm_pre.reshape(depth * 3, 1, d)
    gpost = norm_post.reshape(depth * 3, 1, d)
    rowvec = lambda v: v.reshape(depth, 1, -1)
    lane_pad = lambda v: jnp.pad(v.reshape(depth, 1, -1), ((0, 0), (0, 0), (0, LANES - v[0].size)))
    ssd_params = (ssd_conv_w, rowvec(ssd_conv_b), lane_pad(ssd_dt_bias), lane_pad(ssd_a_log),
                  rowvec(jnp.repeat(ssd_d, SSD_HEADDIM, axis=-1)), rowvec(ssd_norm_w))
    lru_wg = jnp.concatenate([_block_diag(lru_wa), _block_diag(lru_wx)], axis=-1).astype(BF16)
    lru_bg = jnp.concatenate([lru_ba.reshape(depth, 2, -1), lru_bx.reshape(depth, 2, -1)], axis=-1)
    lru_params = (lru_conv_w, rowvec(lru_conv_b), lru_wg, lru_bg, lru_lambda)
    hgrn_nw = rowvec(hgrn_norm_w)

    rowgeom = dict(lat_rows=lat_rows, seq=seq, n_batch=n_batch)
    xs, h = _prenorm(x.reshape(lat_rows, d), ctx.reshape(n_batch * ctx_len, d), mods, gpre, **rowgeom)
    for l in range(depth):
        last = l == depth - 1
        g = _ffn_up(h, ffn_w1, ffn_w3, layer=l, k=0, n_rows=all_rows)
        xs, h = _ffn_down(g, xs, mods, gpost, gpre, w2, layer=l, k=0, j=0, nxt=3, nxt_layer=l,
                          n_rows=all_rows, **rowgeom)
        ua, ub, uc, ud = _inproj(h, w_in_parts, layer=l, **geom)
        ya = _ssd(ua, *ssd_params, layer=l, **mix)
        yb = _lru(ub, *lru_params, layer=l, **mix)
        yc = _hgrn(uc, hgrn_lb_logits, hgrn_nw, layer=l, **mix)
        yd = _ret(ud, **mix)
        n_rows = lat_rows if last else all_rows
        xs, h = _outproj(xs, (ya, yb, yc, yd), w_out_b, mods, gpost, gpre, layer=l, n_rows=n_rows, **geom)
        g = _ffn_up(h, ffn_w1, ffn_w3, layer=l, k=1, n_rows=n_rows)
        if last:
            (xs,) = _ffn_down(g, xs, mods, gpost, gpre, w2, layer=l, k=1, j=6, nxt=None, nxt_layer=l,
                              n_rows=n_rows, **rowgeom)
        else:
            xs, h = _ffn_down(g, xs, mods, gpost, gpre, w2, layer=l, k=1, j=6, nxt=0, nxt_layer=l + 1,
                              n_rows=n_rows, **rowgeom)
    return xs.reshape(n_batch, seq, d)
```

```python
import functools

import numpy as np
import jax
import jax.numpy as jnp
from jax import lax
from jax.experimental import pallas as pl
from jax.experimental.pallas import tpu as pltpu

F32 = jnp.float32
BF16 = jnp.bfloat16

N_MOD = 9
CONV_W = 4
EPS = 1e-6
GRID_W = 64

SSD_WIDTH = 512
SSD_HEADDIM = 64
SSD_HEADS = 8
SSD_GROUPS = 2
SSD_STATE = 64
SSD_XBC = SSD_WIDTH + 2 * SSD_GROUPS * SSD_STATE
SSD_COLS = SSD_WIDTH + SSD_XBC + 2 * SSD_HEADS
SSD_PAD = 1408
LRU_WIDTH = 512
LRU_C = 8.0
LRU_COLS = 2 * LRU_WIDTH
HGRN_WIDTH = 512
HGRN_HEADDIM = 128
HGRN_HEADS = 4
HGRN_COLS = 5 * HGRN_WIDTH
RET_WIDTH = 512
RET_V_DIM = 128
RET_HEADS = 4
RET_QK_DIM = 64
RET_COLS = 2 * RET_HEADS * RET_QK_DIM + 2 * RET_WIDTH
ROPE_BASE = 10000.0

LANES = 128
CHUNK = 128
SUB = 16
VMEM_LIMIT = 56 * 1024 * 1024


def _cparams(n_axes):
    return pltpu.CompilerParams(dimension_semantics=("arbitrary",) * n_axes,
                                vmem_limit_bytes=VMEM_LIMIT)


def _bdot(a, b):
    return jnp.dot(a.astype(BF16), b.astype(BF16), preferred_element_type=F32)


def _bdot_nt(a, b):
    return lax.dot_general(a.astype(BF16), b.astype(BF16), (((1,), (1,)), ((), ())),
                           preferred_element_type=F32)


def _split3(x):
    hi = x.astype(BF16)
    r1 = x - hi.astype(F32)
    mid = r1.astype(BF16)
    lo = (r1 - mid.astype(F32)).astype(BF16)
    return hi, mid, lo


def _dot_sel(sel, parts):
    return (jnp.dot(sel, parts[0], preferred_element_type=F32)
            + jnp.dot(sel, parts[1], preferred_element_type=F32)
            + jnp.dot(sel, parts[2], preferred_element_type=F32))


def _sigmoid(x):
    return 1.0 / (1.0 + jnp.exp(-x))


def _silu(x):
    return x * _sigmoid(x)


def _softplus(x):
    return jnp.maximum(x, 0.0) + jnp.log(1.0 + jnp.exp(-jnp.abs(x)))


def _iota(shape, dim):
    return lax.broadcasted_iota(jnp.int32, shape, dim)


def _const_spec(shape):
    return pl.BlockSpec(shape, lambda *_: (0,) * len(shape))


def _layer_spec(shape, layer):
    return pl.BlockSpec((None,) + shape, lambda *_: (layer,) + (0,) * len(shape))


def _mod_kernel(c_ref, w_ref, b_ref, o_ref):
    o_ref[...] = _bdot(_silu(c_ref[...]), w_ref[...]) + b_ref[...]


def _modulation(c_all, w_mod, b_mod):
    depth, d, n = w_mod.shape
    tn = 1024
    return pl.pallas_call(
        _mod_kernel,
        grid=(depth, n // tn),
        in_specs=[pl.BlockSpec((8, d), lambda l, j: (0, 0)),
                  pl.BlockSpec((None, d, tn), lambda l, j: (l, 0, j)),
                  pl.BlockSpec((None, 1, tn), lambda l, j: (l, 0, j))],
        out_specs=pl.BlockSpec((None, 8, tn), lambda l, j: (l, 0, j)),
        out_shape=jax.ShapeDtypeStruct((depth, 8, n), F32),
        compiler_params=_cparams(2),
        name="modulation",
    )(c_all, w_mod, b_mod.reshape(depth, 1, n))


def _mod_row_idx(i, tm, layer, lat_rows, seq, n_batch):
    return (layer, jnp.where(i * tm < lat_rows, (i * tm) // seq, n_batch), 0, 0)


def _mod_norm(x, g, shift, scale):
    return (x * lax.rsqrt(jnp.mean(x * x, axis=-1, keepdims=True) + EPS) * (g * (1.0 + scale))
            + shift).astype(BF16)


def _prenorm_kernel(xl_ref, xc_ref, mod_ref, g_ref, x_ref, h_ref, *, n_split):
    x = jnp.where(pl.program_id(0) < n_split, xl_ref[...], xc_ref[...])
    x_ref[...] = x
    h_ref[...] = _mod_norm(x, g_ref[...], mod_ref[0:1, :], mod_ref[1:2, :])


def _prenorm(xl, xc, mods, gpre, *, lat_rows, seq, n_batch):
    d = xl.shape[1]
    rows = xl.shape[0] + xc.shape[0]
    tm = 512
    n_split = xl.shape[0] // tm
    blk = pl.BlockSpec((tm, d), lambda i: (i, 0))
    return pl.pallas_call(
        functools.partial(_prenorm_kernel, n_split=n_split),
        grid=(rows // tm,),
        in_specs=[pl.BlockSpec((tm, d), lambda i: (jnp.minimum(i, n_split - 1), 0)),
                  pl.BlockSpec((tm, d), lambda i: (jnp.maximum(i - n_split, 0), 0)),
                  pl.BlockSpec((None, None, N_MOD, d), lambda i: _mod_row_idx(i, tm, 0, lat_rows, seq, n_batch)),
                  pl.BlockSpec((None, 1, d), lambda i: (0, 0, 0))],
        out_specs=[blk, blk],
        out_shape=[jax.ShapeDtypeStruct((rows, d), F32), jax.ShapeDtypeStruct((rows, d), BF16)],
        compiler_params=_cparams(1),
        name="prenorm",
    )(xl, xc, mods, gpre)


def _ffn_up_kernel(h_ref, w1_ref, w3_ref, g_ref, w1_scr, w3_scr, *, n_sub):
    @pl.when(pl.program_id(1) == 0)
    def _():
        w1_scr[...] = w1_ref[...].astype(BF16)
        w3_scr[...] = w3_ref[...].astype(BF16)

    rows = h_ref.shape[0] // n_sub
    for r in range(n_sub):
        h = h_ref[r * rows:(r + 1) * rows, :]
        a = jnp.dot(h, w1_scr[...], preferred_element_type=F32)
        b = jnp.dot(h, w3_scr[...], preferred_element_type=F32)
        g_ref[r * rows:(r + 1) * rows, :] = (_silu(a) * b).astype(BF16)


def _ffn_up(h, w1, w3, *, layer, k, n_rows):
    d = h.shape[1]
    dff = w1.shape[-1]
    n_sub = 4
    tm, tf = n_rows // 4, 512
    w_spec = pl.BlockSpec((None, None, d, tf), lambda f, i: (layer, k, 0, f))
    return pl.pallas_call(
        functools.partial(_ffn_up_kernel, n_sub=n_sub),
        grid=(dff // tf, n_rows // tm),
        in_specs=[pl.BlockSpec((tm, d), lambda f, i: (i, 0)), w_spec, w_spec],
        out_specs=pl.BlockSpec((tm, tf), lambda f, i: (i, f)),
        out_shape=jax.ShapeDtypeStruct((n_rows, dff), BF16),
        scratch_shapes=[pltpu.VMEM((d, tf), BF16), pltpu.VMEM((d, tf), BF16)],
        compiler_params=_cparams(2),
        name="ffn_up",
    )(h, w1, w3)


EPI_SUB = 2


def _residual_epilogue(rows, x_ref, y, gate, gpost, mod_ref, gnext_ref, nxt, o_ref, hn_ref):
    xn = x_ref[rows, :] + y * lax.rsqrt(jnp.mean(y * y, axis=-1, keepdims=True) + EPS) * (gate * gpost)
    o_ref[rows, :] = xn
    if nxt is not None:
        hn_ref[rows, :] = _mod_norm(xn, gnext_ref[...], mod_ref[nxt:nxt + 1, :], mod_ref[nxt + 1:nxt + 2, :])


def _ffn_down_kernel(*refs, layer, k, j, nxt, nxt_mod_own):
    g_ref, x_ref, mod_ref, gpost_ref = refs[:4]
    pos = 4
    nmod_ref = mod_ref
    gnext_ref = hn_ref = None
    if nxt is not None:
        if not nxt_mod_own:
            nmod_ref = refs[pos]
            pos += 1
        gnext_ref = refs[pos]
        pos += 1
    w2_hbm, o_ref = refs[pos], refs[pos + 1]
    pos += 2
    if nxt is not None:
        hn_ref = refs[pos]
        pos += 1
    w2_scr, stage_scr, sem = refs[pos], refs[pos + 1], refs[pos + 2]

    @pl.when(pl.program_id(0) == 0)
    def _():
        def consume(r0, chunk):
            w2_scr[pl.ds(r0, chunk.shape[0]), :] = chunk.astype(BF16)

        _stage_rows(w2_hbm.at[layer, k], 0, w2_hbm.shape[2], stage_scr.shape[1], stage_scr, sem, consume)

    n = x_ref.shape[0] // EPI_SUB
    for r in range(EPI_SUB):
        rows = slice(r * n, (r + 1) * n)
        y = jnp.dot(g_ref[rows, :], w2_scr[...], preferred_element_type=F32)
        _residual_epilogue(rows, x_ref, y, 0.5 * mod_ref[j + 2:j + 3, :], gpost_ref[...], nmod_ref, gnext_ref,
                           nxt, o_ref, hn_ref)


def _ffn_down(g, x, mods, gpost, gpre, w2, *, layer, k, j, nxt, nxt_layer, n_rows, lat_rows, seq, n_batch):
    d = x.shape[1]
    dff = g.shape[1]
    tm = 256
    norm_row = layer * 3 + (0 if j == 0 else 2)
    blk = pl.BlockSpec((tm, d), lambda i: (i, 0))
    mod_spec = lambda ly: pl.BlockSpec((None, None, N_MOD, d),
                                       lambda i: _mod_row_idx(i, tm, ly, lat_rows, seq, n_batch))
    in_specs = [pl.BlockSpec((tm, dff), lambda i: (i, 0)), blk, mod_spec(layer),
                pl.BlockSpec((None, 1, d), lambda i: (norm_row, 0, 0))]
    args = [g, x, mods, gpost]
    out_specs, out_shape = [blk], [jax.ShapeDtypeStruct((n_rows, d), F32)]
    nxt_mod_own = nxt_layer == layer
    if nxt is not None:
        if not nxt_mod_own:
            in_specs.append(mod_spec(nxt_layer))
            args.append(mods)
        in_specs.append(pl.BlockSpec((None, 1, d), lambda i: (nxt_layer * 3 + nxt // 3, 0, 0)))
        args.append(gpre)
        out_specs.append(blk)
        out_shape.append(jax.ShapeDtypeStruct((n_rows, d), BF16))
    in_specs.append(pl.BlockSpec(memory_space=pl.ANY))
    args.append(w2)
    return pl.pallas_call(
        functools.partial(_ffn_down_kernel, layer=layer, k=k, j=j, nxt=nxt, nxt_mod_own=nxt_mod_own),
        grid=(n_rows // tm,),
        in_specs=in_specs,
        out_specs=out_specs,
        out_shape=out_shape,
        scratch_shapes=[pltpu.VMEM((dff, d), BF16), pltpu.VMEM((2, 512, d), F32), pltpu.SemaphoreType.DMA((2,))],
        compiler_params=_cparams(1),
        name="ffn_down",
    )(*args)


def _stage_rows(src_hbm, row0, n_rows, rpc, stage_scr, sem, consume):
    n_chunks = n_rows // rpc
    assert n_chunks * rpc == n_rows and rpc <= stage_scr.shape[1]

    def copy(c, slot):
        return pltpu.make_async_copy(src_hbm.at[pl.ds(row0 + c * rpc, rpc)],
                                     stage_scr.at[slot, pl.ds(0, rpc)], sem.at[slot])

    copy(0, 0).start()

    def step(c, _):
        slot = c % 2

        @pl.when(c + 1 < n_chunks)
        def _():
            copy(c + 1, 1 - slot).start()

        copy(c, slot).wait()
        consume(pl.multiple_of(c * rpc, 16), stage_scr[slot, pl.ds(0, rpc), :])
        return 0

    lax.fori_loop(0, n_chunks, step, 0)


def _inproj_kernel(h_ref, w_hbm, oa_ref, ob_ref, oc_ref, od_ref, wa_scr, wb_scr, wc_scr, wd_scr, stage_scr, sem, *,
                   layer, splits):
    w_scr = (wa_scr, wb_scr, wc_scr, wd_scr)

    @pl.when(pl.program_id(0) == 0)
    def _():
        for m, scr in enumerate(w_scr):
            n_rows = splits[m + 1] - splits[m]
            rpc = max(r for r in range(16, stage_scr.shape[1] + 1, 16) if n_rows % r == 0)

            def consume(r, chunk, scr=scr):
                scr[pl.ds(r, chunk.shape[0]), :] = chunk.astype(BF16)

            _stage_rows(w_hbm.at[layer], splits[m], n_rows, rpc, stage_scr, sem, consume)
            if scr.shape[0] > n_rows:
                scr[n_rows:, :] = jnp.zeros((scr.shape[0] - n_rows, scr.shape[1]), BF16)

    h = h_ref[...]
    for w, o_ref in zip(w_scr, (oa_ref, ob_ref, oc_ref, od_ref)):
        o_ref[...] = lax.dot_general(h, w[...], (((1,), (1,)), ((), ())), preferred_element_type=F32)


def _x_to_b_block(i, tm, lat_rows, seq, ctx_len):
    per_b = (seq + ctx_len) // tm
    lat_b = seq // tm
    ctx_b = ctx_len // tm
    n_lat = lat_rows // tm
    lat_idx = (i // lat_b) * per_b + ctx_b + i % lat_b
    ic = i - n_lat
    ctx_idx = (ic // ctx_b) * per_b + ic % ctx_b
    return jnp.where(i < n_lat, lat_idx, ctx_idx)


def _inproj(h, w_in, *, layer, lat_rows, seq, ctx_len, n_batch):
    rows, d = h.shape
    tm = 256
    stage_rows = 512
    splits = (0, SSD_COLS, SSD_COLS + LRU_COLS, SSD_COLS + LRU_COLS + HGRN_COLS, w_in.shape[1])
    widths = [SSD_PAD, LRU_COLS, HGRN_COLS, RET_COLS]

    def out_idx(i):
        return (_x_to_b_block(i, tm, lat_rows, seq, ctx_len), 0)

    return pl.pallas_call(
        functools.partial(_inproj_kernel, layer=layer, splits=splits),
        grid=(rows // tm,),
        in_specs=[pl.BlockSpec((tm, d), lambda i: (i, 0)), pl.BlockSpec(memory_space=pl.ANY)],
        out_specs=[pl.BlockSpec((tm, n), out_idx) for n in widths],
        out_shape=[jax.ShapeDtypeStruct((rows, n), F32) for n in widths],
        scratch_shapes=[pltpu.VMEM((n, d), BF16) for n in widths]
        + [pltpu.VMEM((2, stage_rows, d), F32), pltpu.SemaphoreType.DMA((2,))],
        compiler_params=_cparams(1),
        name="inproj",
    )(h, w_in)


def _outproj_kernel(x_ref, ya_ref, yb_ref, yc_ref, yd_ref, w_ref, mod_ref, gpost_ref, gnext_ref, o_ref, hn_ref):
    w = LRU_WIDTH
    n = x_ref.shape[0] // EPI_SUB
    for r in range(EPI_SUB):
        rows = slice(r * n, (r + 1) * n)
        ycat = jnp.concatenate([ya_ref[rows, :], yb_ref[rows, :], yc_ref[rows, :], yd_ref[rows, :]], axis=1)
        y = jnp.dot(ycat, w_ref[...], preferred_element_type=F32)
        _residual_epilogue(rows, x_ref, y, mod_ref[5:6, :], gpost_ref[...], mod_ref, gnext_ref, 6, o_ref, hn_ref)


def _outproj(x, ys, w, mods, gpost, gpre, *, layer, n_rows, lat_rows, seq, ctx_len, n_batch):
    d = x.shape[1]
    tm = 256
    wy = ys[0].shape[1]
    blk = pl.BlockSpec((tm, d), lambda i: (i, 0))

    def y_idx(i):
        return (_x_to_b_block(i, tm, lat_rows, seq, ctx_len), 0)

    return pl.pallas_call(
        _outproj_kernel,
        grid=(n_rows // tm,),
        in_specs=[blk] + [pl.BlockSpec((tm, wy), y_idx)] * 4
        + [_layer_spec((4 * wy, d), layer),
           pl.BlockSpec((None, None, N_MOD, d), lambda i: _mod_row_idx(i, tm, layer, lat_rows, seq, n_batch)),
           pl.BlockSpec((None, 1, d), lambda i: (layer * 3 + 1, 0, 0)),
           pl.BlockSpec((None, 1, d), lambda i: (layer * 3 + 2, 0, 0))],
        out_specs=[blk, blk],
        out_shape=[jax.ShapeDtypeStruct((n_rows, d), F32), jax.ShapeDtypeStruct((n_rows, d), BF16)],
        compiler_params=_cparams(1),
        name="outproj",
    )(x, *ys, w, mods, gpost, gpre)


def _conv_chunk(u_ref, col0, ncols, base, first, last, t_tot, w_ref, b_ref):
    cols = slice(col0, col0 + ncols)
    prev = u_ref[pl.ds(pl.multiple_of(jnp.maximum(base - 8, 0), 8), 8), cols]
    nxt = u_ref[pl.ds(pl.multiple_of(jnp.minimum(base + CHUNK, t_tot - 8), 8), 8), cols]
    cur = u_ref[pl.ds(base, CHUNK), cols]
    prev = jnp.where(first, 0.0, prev)
    nxt = jnp.where(last, 0.0, nxt)
    win = jnp.concatenate([prev, cur, nxt], axis=0)
    n = CHUNK + 16
    xm1 = pltpu.roll(win, 1, axis=0)[8:8 + CHUNK]
    xp1 = pltpu.roll(win, n - 1, axis=0)[8:8 + CHUNK]
    xp2 = pltpu.roll(win, n - 2, axis=0)[8:8 + CHUNK]
    return (w_ref[0:1, :] * xm1 + w_ref[1:2, :] * cur + w_ref[2:3, :] * xp1 + w_ref[3:4, :] * xp2
            + b_ref[...])


def _chunk_order(i, nc_c, nc, reverse):
    if not reverse:
        return i
    return jnp.where(i < nc_c, nc_c - 1 - i, nc + nc_c - 1 - i)


def _stream_edges(c, nc_c, nc):
    first = jnp.logical_or(c == 0, c == nc_c)
    last = jnp.logical_or(c == nc_c - 1, c == nc - 1)
    return first, last


def _ssd_kernel(u_ref, cw_ref, cb_ref, dtb_ref, alog_ref, dskip_ref, nw_ref, o_ref,
                act_scr, dl_scr, y_scr, s_scr, *, nc_c, nc):
    t_tot = nc * CHUNK
    xbc0 = SSD_WIDTH
    dt0 = SSD_WIDTH + SSD_XBC

    def prep(c, _):
        base = pl.multiple_of(c * CHUNK, CHUNK)
        first, last = _stream_edges(c, nc_c, nc)
        conv = _conv_chunk(u_ref, xbc0, SSD_XBC, base, first, last, t_tot, cw_ref, cb_ref)
        act_scr[pl.ds(base, CHUNK), :] = _silu(conv)
        dt = u_ref[pl.ds(base, CHUNK), dt0:dt0 + LANES]
        dl_scr[pl.ds(base, CHUNK), :] = _softplus(dt + dtb_ref[...])
        return 0

    lax.fori_loop(0, nc, prep, 0)

    a_neg = -jnp.exp(alog_ref[...])
    row = _iota((CHUNK, CHUNK), 0)
    col = _iota((CHUNK, CHUNK), 1)
    lane_lo = _iota((CHUNK, LANES), 1) < SSD_HEADDIM
    grp_mask = [(_iota((CHUNK, LANES), 1) // SSD_STATE) == g for g in range(SSD_GROUPS)]

    s_scr[...] = jnp.zeros_like(s_scr)
    bodies = []
    for d in range(2):
        tri = (row >= col) if d == 0 else (row <= col)
        tri_b = tri.astype(BF16)
        edge = CHUNK - 1 if d == 0 else 0

        def body(i, d=d, tri=tri, tri_b=tri_b, edge=edge):
            c = _chunk_order(i, nc_c, nc, d == 1)
            base = pl.multiple_of(c * CHUNK, CHUNK)
            act = act_scr[pl.ds(base, CHUNK), :]
            bm = act[:, SSD_WIDTH:SSD_WIDTH + LANES]
            cm = act[:, SSD_WIDTH + LANES:SSD_WIDTH + 2 * LANES]
            delta = dl_scr[pl.ds(base, CHUNK), :]
            la = delta * a_neg
            cum_col = _dot_sel(tri_b, _split3(la))
            cum_row = cum_col.T
            delta_row = delta.T
            bt = bm.T
            gmat = [_bdot(jnp.where(grp_mask[g], cm, 0.0), bt) for g in range(SSD_GROUPS)]
            for k in range(SSD_HEADS // 2):
                g = (2 * k) // (SSD_HEADS // SSD_GROUPS)
                x_tile = act[:, k * LANES:(k + 1) * LANES]
                cg = jnp.where(grp_mask[g], cm, 0.0)
                m_list, w_list, e_list, tot_list = [], [], [], []
                for h in (2 * k, 2 * k + 1):
                    j = d * SSD_HEADS + h
                    ccol = cum_col[:, j:j + 1]
                    crow = cum_row[j:j + 1, :]
                    drow = delta_row[j:j + 1, :]
                    dec = jnp.where(tri, jnp.exp(jnp.where(tri, ccol - crow, 0.0)), 0.0)
                    m_list.append(gmat[g] * dec * drow)
                    tot = cum_row[j:j + 1, edge:edge + 1]
                    w_list.append(bt * (drow * jnp.exp(tot - crow)))
                    e_list.append(jnp.exp(ccol))
                    tot_list.append(jnp.exp(tot))
                y2 = _bdot(jnp.concatenate(m_list, axis=0), x_tile)
                u2 = _bdot(jnp.concatenate(w_list, axis=0), x_tile)
                s_old = s_scr[d, :, k * LANES:(k + 1) * LANES]
                y_inter = _bdot(cg, s_old) * jnp.where(lane_lo, e_list[0], e_list[1])
                y_tile = jnp.where(lane_lo, y2[:CHUNK], y2[CHUNK:]) + y_inter
                s_scr[d, :, k * LANES:(k + 1) * LANES] = (
                    s_old * jnp.where(lane_lo, tot_list[0], tot_list[1])
                    + jnp.where(lane_lo, u2[:CHUNK], u2[CHUNK:]))
                if d == 0:
                    y_tile = y_tile + x_tile * dskip_ref[:, k * LANES:(k + 1) * LANES]
                y_scr[d, pl.ds(base, CHUNK), k * LANES:(k + 1) * LANES] = y_tile

        bodies.append(body)

    def both(i, _):
        bodies[0](i)
        bodies[1](i)
        return 0

    lax.fori_loop(0, nc, both, 0)

    def fin(c, _):
        base = pl.multiple_of(c * CHUNK, CHUNK)
        z = u_ref[pl.ds(base, CHUNK), 0:SSD_WIDTH]
        y = (y_scr[0, pl.ds(base, CHUNK), :] + y_scr[1, pl.ds(base, CHUNK), :]) * _silu(z)
        yn = y * lax.rsqrt(jnp.mean(y * y, axis=-1, keepdims=True) + EPS) * nw_ref[...]
        o_ref[pl.ds(base, CHUNK), :] = yn.astype(o_ref.dtype)
        return 0

    lax.fori_loop(0, nc, fin, 0)


def _ssd(u, conv_w, conv_b, dt_bias, a_log, d_skip, norm_w, *, layer, n_batch, t_ctx, t_lat):
    t_tot = t_ctx + t_lat
    nc_c, nc = t_ctx // CHUNK, t_tot // CHUNK
    return pl.pallas_call(
        functools.partial(_ssd_kernel, nc_c=nc_c, nc=nc),
        grid=(n_batch,),
        in_specs=[pl.BlockSpec((t_tot, SSD_PAD), lambda b: (b, 0)),
                  _layer_spec((CONV_W, SSD_XBC), layer), _layer_spec((1, SSD_XBC), layer),
                  _layer_spec((1, LANES), layer), _layer_spec((1, LANES), layer),
                  _layer_spec((1, SSD_WIDTH), layer), _layer_spec((1, SSD_WIDTH), layer)],
        out_specs=pl.BlockSpec((t_tot, SSD_WIDTH), lambda b: (b, 0)),
        out_shape=jax.ShapeDtypeStruct((n_batch * t_tot, SSD_WIDTH), BF16),
        scratch_shapes=[pltpu.VMEM((t_tot, SSD_XBC), F32), pltpu.VMEM((t_tot, LANES), F32),
                        pltpu.VMEM((2, t_tot, SSD_WIDTH), F32), pltpu.VMEM((2, CHUNK, SSD_WIDTH), F32)],
        compiler_params=_cparams(1),
        name="ssd",
    )(u, conv_w, conv_b, dt_bias, a_log, d_skip, norm_w)


def _lru_kernel(u_ref, cw_ref, cb_ref, wg_ref, bg_ref, lam_ref, o_ref, xc_scr, h_scr, *, nc_c, nc):
    t_tot = nc * CHUNK
    w = LRU_WIDTH

    def prep(c, _):
        base = pl.multiple_of(c * CHUNK, CHUNK)
        first, last = _stream_edges(c, nc_c, nc)
        xc_scr[pl.ds(base, CHUNK), :] = _conv_chunk(u_ref, 0, w, base, first, last, t_tot, cw_ref, cb_ref)
        return 0

    lax.fori_loop(0, nc, prep, 0)

    n_tiles = CHUNK // 8
    sub = _iota((n_tiles, 8, w), 1)

    bodies = []
    for d in range(2):
        coef = (-LRU_C * 1.4426950408889634) * _softplus(-lam_ref[d:d + 1, :])

        def body(i, h_prev, d=d, coef=coef):
            c = _chunk_order(i, nc_c, nc, d == 1)
            base = pl.multiple_of(c * CHUNK, CHUNK)
            xc = xc_scr[pl.ds(base, CHUNK), :]
            gates = _bdot(xc, wg_ref[d]) + bg_ref[d:d + 1, :]
            r = _sigmoid(gates[:, :w])
            ig = _sigmoid(gates[:, w:])
            a = jnp.exp2(r * coef)
            z = jnp.maximum(1.0 - a * a, 1e-12)
            b = (z * lax.rsqrt(z)) * (ig * xc)
            a = a.reshape(n_tiles, 8, w)
            b = b.reshape(n_tiles, 8, w)
            for k in (1, 2, 4):
                if d == 0:
                    keep = sub >= k
                    shift = k
                else:
                    keep = sub < 8 - k
                    shift = 8 - k
                a_s = jnp.where(keep, pltpu.roll(a, shift, axis=1), 1.0)
                b_s = jnp.where(keep, pltpu.roll(b, shift, axis=1), 0.0)
                b = a * b_s + b
                a = a * a_s
            tiles = range(n_tiles) if d == 0 else range(n_tiles - 1, -1, -1)
            outs = [None] * n_tiles
            for t in tiles:
                h_t = a[t] * h_prev + b[t]
                outs[t] = h_t
                h_prev = h_t[7:8] if d == 0 else h_t[0:1]
            h_scr[d, pl.ds(base, CHUNK), :] = jnp.concatenate(outs, axis=0)
            return h_prev

        bodies.append(body)

    def both(i, carry):
        return bodies[0](i, carry[0]), bodies[1](i, carry[1])

    zero = jnp.zeros((1, w), F32)
    lax.fori_loop(0, nc, both, (zero, zero))

    def fin(c, _):
        base = pl.multiple_of(c * CHUNK, CHUNK)
        gate = u_ref[pl.ds(base, CHUNK), w:2 * w]
        out = (h_scr[0, pl.ds(base, CHUNK), :] + h_scr[1, pl.ds(base, CHUNK), :]) * jax.nn.gelu(gate)
        o_ref[pl.ds(base, CHUNK), :] = out.astype(o_ref.dtype)
        return 0

    lax.fori_loop(0, nc, fin, 0)


def _block_diag(wb):
    nb, bi, bj = wb.shape[-3:]
    eye = jnp.eye(nb, dtype=wb.dtype)
    return jnp.einsum('...hij,hg->...higj', wb, eye).reshape(wb.shape[:-3] + (nb * bi, nb * bj))


def _lru(u, conv_w, conv_b, wg, bg, lam, *, layer, n_batch, t_ctx, t_lat):
    t_tot = t_ctx + t_lat
    nc_c, nc = t_ctx // CHUNK, t_tot // CHUNK
    w = LRU_WIDTH
    return pl.pallas_call(
        functools.partial(_lru_kernel, nc_c=nc_c, nc=nc),
        grid=(n_batch,),
        in_specs=[pl.BlockSpec((t_tot, LRU_COLS), lambda b: (b, 0)),
                  _layer_spec((CONV_W, w), layer), _layer_spec((1, w), layer),
                  _layer_spec((2, w, 2 * w), layer), _layer_spec((2, 2 * w), layer), _layer_spec((2, w), layer)],
        out_specs=pl.BlockSpec((t_tot, w), lambda b: (b, 0)),
        out_shape=jax.ShapeDtypeStruct((n_batch * t_tot, w), BF16),
        scratch_shapes=[pltpu.VMEM((t_tot, w), F32), pltpu.VMEM((2, t_tot, w), F32)],
        compiler_params=_cparams(1),
        name="rglru",
    )(u, conv_w, conv_b, wg, bg, lam)


def _hgrn_kernel(q_ref, ff_ref, fb_ref, i_ref, g_ref, lbl_ref, nw_ref, esel_ref, o_ref, of_scr, ob_scr, st_scr, *,
                 layer, nc_c, nc, n_heads):
    n_sub = CHUNK // SUB
    row = _iota((CHUNK, CHUNK), 0)
    col = _iota((CHUNK, CHUNK), 1)
    scale = HGRN_HEADDIM ** -0.5
    log2e = 1.4426950408889634

    lbs = []
    for d in range(2):
        logits = lbl_ref[d]
        ex = jnp.exp(logits - jnp.max(logits, axis=0, keepdims=True))
        sm = ex / jnp.sum(ex, axis=0, keepdims=True)
        lbs.append(jnp.sum(sm[0:layer + 1], axis=0, keepdims=True) - sm[0:1])
    causal = [row >= col, row <= col]
    tri_full = [m.astype(BF16) for m in causal]
    valids = [jnp.logical_and(row // SUB == col // SUB, m) for m in causal]
    levels = (2 * SUB, 4 * SUB, CHUNK)
    q_side = [[((row % g) >= g // 2) if d == 0 else ((row % g) < g // 2) for g in levels] for d in range(2)]
    pair_mask = [[jnp.logical_and(row // g == col // g,
                                  jnp.logical_and(q_side[d][n], ((col % g) < g // 2) if d == 0
                                                  else ((col % g) >= g // 2)))
                  for n, g in enumerate(levels)] for d in range(2)]
    st_scr[...] = jnp.zeros_like(st_scr)

    def chunk(d, hh, base):
        cols = slice(hh * LANES, (hh + 1) * LANES)
        lb = lbs[d][:, cols]
        f_ref = ff_ref if d == 0 else fb_ref
        qh = _silu(q_ref[pl.ds(base, CHUNK), cols]) * scale
        f = lb + (1.0 - lb) * _sigmoid(f_ref[pl.ds(base, CHUNK), cols])
        kk = 1.0 - f
        gl = jnp.log(f) * log2e
        v = i_ref[pl.ds(base, CHUNK), cols]
        cum = _dot_sel(tri_full[d], _split3(gl))
        halves = []
        nb = n_sub // 2
        for r0 in range(0, CHUNK, CHUNK // 2):
            rows = slice(r0, r0 + CHUNK // 2)
            q4 = qh[rows].reshape(nb, 2, 8, LANES)
            c4 = cum[rows].reshape(nb, 2, 8, LANES)
            k4 = kk[rows].reshape(nb, 2, 8, LANES)
            zero = jnp.zeros((nb, 8, LANES), F32)
            acc = None
            for s0 in range(0, SUB, 8):
                ps = []
                for s in range(s0, s0 + 8):
                    ts, rs = s // 8, s % 8
                    cs = c4[:, ts, rs:rs + 1, :]
                    ks = k4[:, ts, rs:rs + 1, :]
                    tiles = []
                    for tt in range(2):
                        if tt == ts:
                            e = jnp.exp2(jnp.minimum(c4[:, tt] - cs, 0.0))
                        elif (tt > ts) == (d == 0):
                            e = jnp.exp2(c4[:, tt] - cs)
                        else:
                            tiles.append(zero)
                            continue
                        tiles.append((q4[:, tt] * e) * ks)
                    ps.append(jnp.stack(tiles, axis=1).reshape(CHUNK // 2, LANES).astype(BF16))
                part = jnp.dot(jnp.concatenate(ps, axis=1), esel_ref[s0 * LANES:(s0 + 8) * LANES, :],
                               preferred_element_type=F32)
                acc = part if acc is None else acc + part
            halves.append(acc)
        scores = jnp.where(valids[d], jnp.concatenate(halves, axis=0), 0.0)
        for n, g in enumerate(levels):
            pieces = []
            for g0 in range(0, CHUNK, g):
                b = g0 + g // 2 - 1 if d == 0 else g0 + g // 2
                pieces.append(jnp.exp2(-jnp.abs(cum[g0:g0 + g] - cum[b:b + 1])))
            w = pieces[0] if len(pieces) == 1 else jnp.concatenate(pieces, axis=0)
            xg = (jnp.where(q_side[d][n], qh, kk) * w).astype(BF16)
            scores = scores + jnp.where(pair_mask[d][n], _bdot_nt(xg, xg), 0.0)
        y = _bdot(scores, v)
        edge = CHUNK - 1 if d == 0 else 0
        tot = cum[edge:edge + 1]
        st = st_scr[d, hh]
        y = y + _bdot_nt(qh * jnp.exp2(cum), st)
        st_scr[d, hh] = st * jnp.exp2(tot) + _bdot(v.T, kk * jnp.exp2(tot - cum))
        return y

    def body(i, _):
        for step in range(2):
            for d in range(2):
                c = _chunk_order(2 * i + step, nc_c, nc, d == 1)
                base = pl.multiple_of(c * CHUNK, CHUNK)
                o_scr = of_scr if d == 0 else ob_scr
                for hh in range(n_heads):
                    o_scr[pl.ds(base, CHUNK), hh * LANES:(hh + 1) * LANES] = chunk(d, hh, base)
        return 0

    assert nc % 2 == 0
    lax.fori_loop(0, nc // 2, body, 0)

    def fin(c, _):
        base = pl.multiple_of(c * CHUNK, CHUNK)
        for hh in range(n_heads):
            cols = slice(hh * LANES, (hh + 1) * LANES)
            o = of_scr[pl.ds(base, CHUNK), cols] + ob_scr[pl.ds(base, CHUNK), cols]
            on = o * lax.rsqrt(jnp.mean(o * o, axis=-1, keepdims=True) + EPS) * nw_ref[...]
            out = on * _silu(g_ref[pl.ds(base, CHUNK), cols])
            o_ref[pl.ds(base, CHUNK), cols] = out.astype(o_ref.dtype)
        return 0

    lax.fori_loop(0, nc, fin, 0)


def _hgrn_selector():
    s_of_row = np.arange(SUB * LANES) // LANES
    return jnp.asarray(s_of_row[:, None] == (np.arange(LANES)[None, :] % SUB), dtype=BF16)


def _hgrn(u, lb_logits, norm_w, *, layer, n_batch, t_ctx, t_lat):
    t_tot = t_ctx + t_lat
    nc_c, nc = t_ctx // CHUNK, t_tot // CHUNK
    n_heads = 2
    ng = HGRN_HEADS // n_heads
    wblk = n_heads * LANES
    depth = lb_logits.shape[1]
    col = lambda part: pl.BlockSpec((t_tot, wblk), lambda b, h, part=part: (b, part * ng + h))
    return pl.pallas_call(
        functools.partial(_hgrn_kernel, layer=layer, nc_c=nc_c, nc=nc, n_heads=n_heads),
        grid=(n_batch, ng),
        in_specs=[col(0), col(1), col(2), col(3), col(4),
                  pl.BlockSpec((2, depth, wblk), lambda b, h: (0, 0, h)),
                  _layer_spec((1, LANES), layer),
                  _const_spec((SUB * LANES, LANES))],
        out_specs=pl.BlockSpec((t_tot, wblk), lambda b, h: (b, h)),
        out_shape=jax.ShapeDtypeStruct((n_batch * t_tot, HGRN_WIDTH), BF16),
        scratch_shapes=[pltpu.VMEM((t_tot, wblk), F32), pltpu.VMEM((t_tot, wblk), F32),
                        pltpu.VMEM((2, n_heads, HGRN_HEADDIM, HGRN_HEADDIM), F32)],
        compiler_params=_cparams(2),
        name="hgrn2",
    )(u, u, u, u, u, lb_logits, norm_w, _hgrn_selector())


def _ret_consts(t_ctx, t_lat):
    gam = 1.0 - np.exp2(-5.0 - np.arange(RET_HEADS, dtype=np.float64))
    t = np.arange(CHUNK, dtype=np.float64)
    diff = np.abs(t[:, None] - t[None, :])
    dsym = gam[:, None, None] ** diff[None]
    dsym[:, np.arange(CHUNK), np.arange(CHUNK)] = 2.0
    head_of_lane = np.repeat(np.arange(RET_HEADS), RET_QK_DIM)
    qf = gam[head_of_lane][None, :] ** (t[:, None] + 1.0)
    qb = gam[head_of_lane][None, :] ** (CHUNK - t[:, None])
    kf = gam[:, None] ** (CHUNK - 1.0 - t[None, :])
    kb = gam[:, None] ** t[None, :]
    gtot = gam ** CHUNK
    pos = np.arange(t_lat)
    rows_p, cols_p = pos // GRID_W, pos % GRID_W
    quarter = RET_QK_DIM // 4
    inv = ROPE_BASE ** (-np.arange(quarter, dtype=np.float64) / quarter)
    ang_r = rows_p[:, None] * inv[None, :]
    ang_c = cols_p[:, None] * inv[None, :]
    cos_h = np.concatenate([np.cos(ang_r), np.cos(ang_r), np.cos(ang_c), np.cos(ang_c)], axis=1)
    sin_h = np.concatenate([-np.sin(ang_r), np.sin(ang_r), -np.sin(ang_c), np.sin(ang_c)], axis=1)
    cos_t = np.concatenate([np.ones((t_ctx, RET_QK_DIM)), cos_h], axis=0)
    sin_t = np.concatenate([np.zeros((t_ctx, RET_QK_DIM)), sin_h], axis=0)
    cos_t = np.tile(cos_t, (1, RET_HEADS))
    sin_t = np.tile(sin_t, (1, RET_HEADS))
    f32 = lambda a: jnp.asarray(a, dtype=F32)
    return (f32(dsym), f32(qf), f32(qb), f32(np.concatenate([kf, kb], axis=0)), [float(g) for g in gtot],
            f32(cos_t), f32(sin_t))


def _ret_kernel(u_ref, dsym_ref, qf_ref, qb_ref, kfac_ref, cos_ref, sin_ref, o_ref, o_scr, s_scr, q_scr, kt_scr, *,
                gtot, nc_c, nc):
    nqk = RET_HEADS * RET_QK_DIM
    lane = _iota((CHUNK, nqk), 1)
    low_half = (lane % (RET_QK_DIM // 2)) < (RET_QK_DIM // 4)
    tile_lane = _iota((CHUNK, LANES), 1)
    head_mask = [(tile_lane // RET_QK_DIM) == p for p in range(2)]
    kscale = RET_QK_DIM ** -0.5

    def rope(x, cos, sin):
        swapped = jnp.where(low_half, pltpu.roll(x, nqk - RET_QK_DIM // 4, axis=1),
                            pltpu.roll(x, RET_QK_DIM // 4, axis=1))
        return x * cos + swapped * sin

    def v_of(base, h):
        return u_ref[pl.ds(base, CHUNK), 2 * nqk + h * RET_V_DIM:2 * nqk + (h + 1) * RET_V_DIM]

    def prep(c):
        base = pl.multiple_of(c * CHUNK, CHUNK)
        cos = cos_ref[pl.ds(base, CHUNK), :]
        sin = sin_ref[pl.ds(base, CHUNK), :]
        q = rope(u_ref[pl.ds(base, CHUNK), 0:nqk], cos, sin)
        k = rope(u_ref[pl.ds(base, CHUNK), nqk:2 * nqk], cos, sin) * kscale
        kt = k.T
        q_scr[pl.ds(base, CHUNK), :] = q
        kt_scr[c] = kt
        for h in range(RET_HEADS):
            p = h // 2
            raw = _bdot(jnp.where(head_mask[h % 2], q[:, p * LANES:(p + 1) * LANES], 0.0),
                        kt[p * LANES:(p + 1) * LANES])
            o_scr[2, pl.ds(base, CHUNK), h * RET_V_DIM:(h + 1) * RET_V_DIM] = _bdot(raw * dsym_ref[h], v_of(base, h))

    def prep_pair(i, _):
        prep(2 * i)
        prep(2 * i + 1)
        return 0

    assert nc % 2 == 0
    lax.fori_loop(0, nc // 2, prep_pair, 0)

    s_scr[...] = jnp.zeros_like(s_scr)
    bodies = []
    for d in range(2):
        qfac = qf_ref if d == 0 else qb_ref

        def body(i, d=d, qfac=qfac):
            c = _chunk_order(i, nc_c, nc, d == 1)
            base = pl.multiple_of(c * CHUNK, CHUNK)
            qd = q_scr[pl.ds(base, CHUNK), :] * qfac[...]
            kt = kt_scr[c]
            for h in range(RET_HEADS):
                p = h // 2
                kt_p = kt[p * LANES:(p + 1) * LANES]
                s_old = s_scr[d, h]
                y = _bdot(jnp.where(head_mask[h % 2], qd[:, p * LANES:(p + 1) * LANES], 0.0), s_old)
                kfac = kfac_ref[d * RET_HEADS + h:d * RET_HEADS + h + 1, :]
                s_scr[d, h] = gtot[h] * s_old + _bdot(kt_p * kfac, v_of(base, h))
                o_scr[d, pl.ds(base, CHUNK), h * RET_V_DIM:(h + 1) * RET_V_DIM] = y

        bodies.append(body)

    def both(i, _):
        bodies[0](i)
        bodies[1](i)
        return 0

    lax.fori_loop(0, nc, both, 0)

    def fin(c, _):
        base = pl.multiple_of(c * CHUNK, CHUNK)
        for h in range(RET_HEADS):
            cols = slice(h * RET_V_DIM, (h + 1) * RET_V_DIM)
            o = (o_scr[0, pl.ds(base, CHUNK), cols] + o_scr[1, pl.ds(base, CHUNK), cols]
                 + o_scr[2, pl.ds(base, CHUNK), cols])
            on = o * lax.rsqrt(jnp.mean(o * o, axis=-1, keepdims=True) + EPS)
            gate = u_ref[pl.ds(base, CHUNK), 2 * nqk + RET_WIDTH + h * RET_V_DIM:
                         2 * nqk + RET_WIDTH + (h + 1) * RET_V_DIM]
            o_ref[pl.ds(base, CHUNK), cols] = (_silu(gate) * on).astype(o_ref.dtype)
        return 0

    lax.fori_loop(0, nc, fin, 0)


def _ret(u, *, n_batch, t_ctx, t_lat):
    t_tot = t_ctx + t_lat
    nc_c, nc = t_ctx // CHUNK, t_tot // CHUNK
    dsym, qf, qb, kfac, gtot, cos_t, sin_t = _ret_consts(t_ctx, t_lat)
    nqk = RET_HEADS * RET_QK_DIM
    return pl.pallas_call(
        functools.partial(_ret_kernel, gtot=gtot, nc_c=nc_c, nc=nc),
        grid=(n_batch,),
        in_specs=[pl.BlockSpec((t_tot, RET_COLS), lambda b: (b, 0)),
                  _const_spec((RET_HEADS, CHUNK, CHUNK)), _const_spec((CHUNK, nqk)), _const_spec((CHUNK, nqk)),
                  _const_spec((2 * RET_HEADS, CHUNK)), _const_spec((t_tot, nqk)), _const_spec((t_tot, nqk))],
        out_specs=pl.BlockSpec((t_tot, RET_WIDTH), lambda b: (b, 0)),
        out_shape=jax.ShapeDtypeStruct((n_batch * t_tot, RET_WIDTH), BF16),
        scratch_shapes=[pltpu.VMEM((3, t_tot, RET_WIDTH), F32),
                        pltpu.VMEM((2, RET_HEADS, LANES, RET_V_DIM), F32),
                        pltpu.VMEM((t_tot, nqk), F32), pltpu.VMEM((nc, nqk, CHUNK), F32)],
        compiler_params=_cparams(1),
        name="retention",
    )(u, dsym, qf, qb, kfac, cos_t, sin_t)


def kernel(x, c, ctx, c_ctx, w_mod, b_mod, norm_pre, norm_post, ffn_w1, ffn_w3, ffn_w2, w_in, w_out,
           ssd_conv_w, ssd_conv_b, ssd_dt_bias, ssd_a_log, ssd_d, ssd_norm_w,
           lru_conv_w, lru_conv_b, lru_wa, lru_ba, lru_wx, lru_bx, lru_lambda,
           hgrn_lb_logits, hgrn_norm_w):
    n_batch, seq, d = x.shape
    ctx_len = ctx.shape[1]
    depth = w_mod.shape[0]
    lat_rows = n_batch * seq
    all_rows = lat_rows + n_batch * ctx_len
    geom = dict(lat_rows=lat_rows, seq=seq, ctx_len=ctx_len, n_batch=n_batch)
    mix = dict(n_batch=n_batch, t_ctx=ctx_len, t_lat=seq)

    c_all = jnp.concatenate([c, c_ctx[None, :], jnp.zeros((8 - n_batch - 1, d), F32)], axis=0)
    mods = _modulation(c_all, w_mod, b_mod).reshape(depth, 8, N_MOD, d)
    w2 = ffn_w2
    w_in_parts = jnp.swapaxes(w_in, 1, 2)
    w_out_b = w_out.astype(BF16)
    gpre = norm_pre.reshape(depth * 3, 1, d)
    gpost = norm_post.reshape(depth * 3, 1, d)
    rowvec = lambda v: v.reshape(depth, 1, -1)
    lane_pad = lambda v: jnp.pad(v.reshape(depth, 1, -1), ((0, 0), (0, 0), (0, LANES - v[0].size)))
    ssd_params = (ssd_conv_w, rowvec(ssd_conv_b), lane_pad(ssd_dt_bias), lane_pad(ssd_a_log),
                  rowvec(jnp.repeat(ssd_d, SSD_HEADDIM, axis=-1)), rowvec(ssd_norm_w))
    lru_wg = jnp.concatenate([_block_diag(lru_wa), _block_diag(lru_wx)], axis=-1).astype(BF16)
    lru_bg = jnp.concatenate([lru_ba.reshape(depth, 2, -1), lru_bx.reshape(depth, 2, -1)], axis=-1)
    lru_params = (lru_conv_w, rowvec(lru_conv_b), lru_wg, lru_bg, lru_lambda)
    hgrn_nw = rowvec(hgrn_norm_w)

    rowgeom = dict(lat_rows=lat_rows, seq=seq, n_batch=n_batch)
    xs, h = _prenorm(x.reshape(lat_rows, d), ctx.reshape(n_batch * ctx_len, d), mods, gpre, **rowgeom)
    for l in range(depth):
        last = l == depth - 1
        g = _ffn_up(h, ffn_w1, ffn_w3, layer=l, k=0, n_rows=all_rows)
        xs, h = _ffn_down(g, xs, mods, gpost, gpre, w2, layer=l, k=0, j=0, nxt=3, nxt_layer=l,
                          n_rows=all_rows, **rowgeom)
        ua, ub, uc, ud = _inproj(h, w_in_parts, layer=l, **geom)
        ya = _ssd(ua, *ssd_params, layer=l, **mix)
        yb = _lru(ub, *lru_params, layer=l, **mix)
        yc = _hgrn(uc, hgrn_lb_logits, hgrn_nw, layer=l, **mix)
        yd = _ret(ud, **mix)
        n_rows = lat_rows if last else all_rows
        xs, h = _outproj(xs, (ya, yb, yc, yd), w_out_b, mods, gpost, gpre, layer=l, n_rows=n_rows, **geom)
        g = _ffn_up(h, ffn_w1, ffn_w3, layer=l, k=1, n_rows=n_rows)
        if last:
            (xs,) = _ffn_down(g, xs, mods, gpost, gpre, w2, layer=l, k=1, j=6, nxt=None, nxt_layer=l,
                              n_rows=n_rows, **rowgeom)
        else:
            xs, h = _ffn_down(g, xs, mods, gpost, gpre, w2, layer=l, k=1, j=6, nxt=0, nxt_layer=l + 1,
                              n_rows=n_rows, **rowgeom)
    return xs.reshape(n_batch, seq, d)
```

```python
import functools

import numpy as np
import jax
import jax.numpy as jnp
from jax import lax
from jax.experimental import pallas as pl
from jax.experimental.pallas import tpu as pltpu

F32 = jnp.float32
BF16 = jnp.bfloat16

N_MOD = 9
CONV_W = 4
EPS = 1e-6
GRID_W = 64

SSD_WIDTH = 512
SSD_HEADDIM = 64
SSD_HEADS = 8
SSD_GROUPS = 2
SSD_STATE = 64
SSD_XBC = SSD_WIDTH + 2 * SSD_GROUPS * SSD_STATE
SSD_COLS = SSD_WIDTH + SSD_XBC + 2 * SSD_HEADS
SSD_PAD = 1408
LRU_WIDTH = 512
LRU_C = 8.0
LRU_COLS = 2 * LRU_WIDTH
HGRN_WIDTH = 512
HGRN_HEADDIM = 128
HGRN_HEADS = 4
HGRN_COLS = 5 * HGRN_WIDTH
RET_WIDTH = 512
RET_V_DIM = 128
RET_HEADS = 4
RET_QK_DIM = 64
RET_COLS = 2 * RET_HEADS * RET_QK_DIM + 2 * RET_WIDTH
ROPE_BASE = 10000.0

LANES = 128
CHUNK = 128
SUB = 8
VMEM_LIMIT = 56 * 1024 * 1024


def _cparams(n_axes):
    return pltpu.CompilerParams(dimension_semantics=("arbitrary",) * n_axes,
                                vmem_limit_bytes=VMEM_LIMIT)


def _bdot(a, b):
    return jnp.dot(a.astype(BF16), b.astype(BF16), preferred_element_type=F32)


def _bdot_nt(a, b):
    return lax.dot_general(a.astype(BF16), b.astype(BF16), (((1,), (1,)), ((), ())),
                           preferred_element_type=F32)


def _split3(x):
    hi = x.astype(BF16)
    r1 = x - hi.astype(F32)
    mid = r1.astype(BF16)
    lo = (r1 - mid.astype(F32)).astype(BF16)
    return hi, mid, lo


def _dot_sel(sel, parts):
    return (jnp.dot(sel, parts[0], preferred_element_type=F32)
            + jnp.dot(sel, parts[1], preferred_element_type=F32)
            + jnp.dot(sel, parts[2], preferred_element_type=F32))


def _sigmoid(x):
    return 1.0 / (1.0 + jnp.exp(-x))


def _silu(x):
    return x * _sigmoid(x)


def _softplus(x):
    return jnp.maximum(x, 0.0) + jnp.log(1.0 + jnp.exp(-jnp.abs(x)))


def _iota(shape, dim):
    return lax.broadcasted_iota(jnp.int32, shape, dim)


def _const_spec(shape):
    return pl.BlockSpec(shape, lambda *_: (0,) * len(shape))


def _layer_spec(shape, layer):
    return pl.BlockSpec((None,) + shape, lambda *_: (layer,) + (0,) * len(shape))


def _mod_kernel(c_ref, w_ref, b_ref, o_ref):
    o_ref[...] = _bdot(_silu(c_ref[...]), w_ref[...]) + b_ref[...]


def _modulation(c_all, w_mod, b_mod):
    depth, d, n = w_mod.shape
    tn = 1024
    return pl.pallas_call(
        _mod_kernel,
        grid=(depth, n // tn),
        in_specs=[pl.BlockSpec((8, d), lambda l, j: (0, 0)),
                  pl.BlockSpec((None, d, tn), lambda l, j: (l, 0, j)),
                  pl.BlockSpec((None, 1, tn), lambda l, j: (l, 0, j))],
        out_specs=pl.BlockSpec((None, 8, tn), lambda l, j: (l, 0, j)),
        out_shape=jax.ShapeDtypeStruct((depth, 8, n), F32),
        compiler_params=_cparams(2),
        name="modulation",
    )(c_all, w_mod, b_mod.reshape(depth, 1, n))


def _mod_row_idx(i, tm, layer, lat_rows, seq, n_batch):
    return (layer, jnp.where(i * tm < lat_rows, (i * tm) // seq, n_batch), 0, 0)


def _mod_norm(x, g, shift, scale):
    return (x * lax.rsqrt(jnp.mean(x * x, axis=-1, keepdims=True) + EPS) * (g * (1.0 + scale))
            + shift).astype(BF16)


def _prenorm_kernel(xl_ref, xc_ref, mod_ref, g_ref, x_ref, h_ref, *, n_split):
    x = jnp.where(pl.program_id(0) < n_split, xl_ref[...], xc_ref[...])
    x_ref[...] = x
    h_ref[...] = _mod_norm(x, g_ref[...], mod_ref[0:1, :], mod_ref[1:2, :])


def _prenorm(xl, xc, mods, gpre, *, lat_rows, seq, n_batch):
    d = xl.shape[1]
    rows = xl.shape[0] + xc.shape[0]
    tm = 512
    n_split = xl.shape[0] // tm
    blk = pl.BlockSpec((tm, d), lambda i: (i, 0))
    return pl.pallas_call(
        functools.partial(_prenorm_kernel, n_split=n_split),
        grid=(rows // tm,),
        in_specs=[pl.BlockSpec((tm, d), lambda i: (jnp.minimum(i, n_split - 1), 0)),
                  pl.BlockSpec((tm, d), lambda i: (jnp.maximum(i - n_split, 0), 0)),
                  pl.BlockSpec((None, None, N_MOD, d), lambda i: _mod_row_idx(i, tm, 0, lat_rows, seq, n_batch)),
                  pl.BlockSpec((None, 1, d), lambda i: (0, 0, 0))],
        out_specs=[blk, blk],
        out_shape=[jax.ShapeDtypeStruct((rows, d), F32), jax.ShapeDtypeStruct((rows, d), BF16)],
        compiler_params=_cparams(1),
        name="prenorm",
    )(xl, xc, mods, gpre)


def _ffn_up_kernel(h_ref, w1_ref, w3_ref, g_ref, w1_scr, w3_scr, *, n_sub):
    @pl.when(pl.program_id(1) == 0)
    def _():
        w1_scr[...] = w1_ref[...].astype(BF16)
        w3_scr[...] = w3_ref[...].astype(BF16)

    rows = h_ref.shape[0] // n_sub
    for r in range(n_sub):
        h = h_ref[r * rows:(r + 1) * rows, :]
        a = jnp.dot(h, w1_scr[...], preferred_element_type=F32)
        b = jnp.dot(h, w3_scr[...], preferred_element_type=F32)
        g_ref[r * rows:(r + 1) * rows, :] = (_silu(a) * b).astype(BF16)


def _ffn_up(h, w1, w3, *, layer, k, n_rows):
    d = h.shape[1]
    dff = w1.shape[-1]
    n_sub = 4
    tm, tf = n_rows // 4, 512
    w_spec = pl.BlockSpec((None, None, d, tf), lambda f, i: (layer, k, 0, f))
    return pl.pallas_call(
        functools.partial(_ffn_up_kernel, n_sub=n_sub),
        grid=(dff // tf, n_rows // tm),
        in_specs=[pl.BlockSpec((tm, d), lambda f, i: (i, 0)), w_spec, w_spec],
        out_specs=pl.BlockSpec((tm, tf), lambda f, i: (i, f)),
        out_shape=jax.ShapeDtypeStruct((n_rows, dff), BF16),
        scratch_shapes=[pltpu.VMEM((d, tf), BF16), pltpu.VMEM((d, tf), BF16)],
        compiler_params=_cparams(2),
        name="ffn_up",
    )(h, w1, w3)


EPI_ROWS = 128


def _residual_epilogue(rows, x_ref, y, gate, gpost, mod_ref, gnext_ref, nxt, o_ref, hn_ref):
    xn = x_ref[rows, :] + y * lax.rsqrt(jnp.mean(y * y, axis=-1, keepdims=True) + EPS) * (gate * gpost)
    o_ref[rows, :] = xn
    if nxt is not None:
        hn_ref[rows, :] = _mod_norm(xn, gnext_ref[...], mod_ref[nxt:nxt + 1, :], mod_ref[nxt + 1:nxt + 2, :])


def _ffn_down_kernel(*refs, layer, k, j, nxt, nxt_mod_own):
    g_ref, x_ref, mod_ref, gpost_ref = refs[:4]
    pos = 4
    nmod_ref = mod_ref
    gnext_ref = hn_ref = None
    if nxt is not None:
        if not nxt_mod_own:
            nmod_ref = refs[pos]
            pos += 1
        gnext_ref = refs[pos]
        pos += 1
    w2_hbm, o_ref = refs[pos], refs[pos + 1]
    pos += 2
    if nxt is not None:
        hn_ref = refs[pos]
        pos += 1
    w2_scr, stage_scr, sem = refs[pos], refs[pos + 1], refs[pos + 2]

    @pl.when(pl.program_id(0) == 0)
    def _():
        def consume(r0, chunk):
            w2_scr[pl.ds(r0, chunk.shape[0]), :] = chunk.astype(BF16)

        _stage_rows(w2_hbm.at[layer, k], 0, w2_hbm.shape[2], stage_scr.shape[1], stage_scr, sem, consume)

    n = EPI_ROWS
    for r in range(x_ref.shape[0] // n):
        rows = slice(r * n, (r + 1) * n)
        y = jnp.dot(g_ref[rows, :], w2_scr[...], preferred_element_type=F32)
        _residual_epilogue(rows, x_ref, y, 0.5 * mod_ref[j + 2:j + 3, :], gpost_ref[...], nmod_ref, gnext_ref,
                           nxt, o_ref, hn_ref)


def _ffn_down(g, x, mods, gpost, gpre, w2, *, layer, k, j, nxt, nxt_layer, n_rows, lat_rows, seq, n_batch):
    d = x.shape[1]
    dff = g.shape[1]
    tm = 256
    norm_row = layer * 3 + (0 if j == 0 else 2)
    blk = pl.BlockSpec((tm, d), lambda i: (i, 0))
    mod_spec = lambda ly: pl.BlockSpec((None, None, N_MOD, d),
                                       lambda i: _mod_row_idx(i, tm, ly, lat_rows, seq, n_batch))
    in_specs = [pl.BlockSpec((tm, dff), lambda i: (i, 0)), blk, mod_spec(layer),
                pl.BlockSpec((None, 1, d), lambda i: (norm_row, 0, 0))]
    args = [g, x, mods, gpost]
    out_specs, out_shape = [blk], [jax.ShapeDtypeStruct((n_rows, d), F32)]
    nxt_mod_own = nxt_layer == layer
    if nxt is not None:
        if not nxt_mod_own:
            in_specs.append(mod_spec(nxt_layer))
            args.append(mods)
        in_specs.append(pl.BlockSpec((None, 1, d), lambda i: (nxt_layer * 3 + nxt // 3, 0, 0)))
        args.append(gpre)
        out_specs.append(blk)
        out_shape.append(jax.ShapeDtypeStruct((n_rows, d), BF16))
    in_specs.append(pl.BlockSpec(memory_space=pl.ANY))
    args.append(w2)
    return pl.pallas_call(
        functools.partial(_ffn_down_kernel, layer=layer, k=k, j=j, nxt=nxt, nxt_mod_own=nxt_mod_own),
        grid=(n_rows // tm,),
        in_specs=in_specs,
        out_specs=out_specs,
        out_shape=out_shape,
        scratch_shapes=[pltpu.VMEM((dff, d), BF16), pltpu.VMEM((2, 512, d), F32), pltpu.SemaphoreType.DMA((2,))],
        compiler_params=_cparams(1),
        name="ffn_down",
    )(*args)


def _stage_rows(src_hbm, row0, n_rows, rpc, stage_scr, sem, consume):
    n_chunks = n_rows // rpc
    assert n_chunks * rpc == n_rows and rpc <= stage_scr.shape[1]

    def copy(c, slot):
        return pltpu.make_async_copy(src_hbm.at[pl.ds(row0 + c * rpc, rpc)],
                                     stage_scr.at[slot, pl.ds(0, rpc)], sem.at[slot])

    copy(0, 0).start()

    def step(c, _):
        slot = c % 2

        @pl.when(c + 1 < n_chunks)
        def _():
            copy(c + 1, 1 - slot).start()

        copy(c, slot).wait()
        consume(pl.multiple_of(c * rpc, 16), stage_scr[slot, pl.ds(0, rpc), :])
        return 0

    lax.fori_loop(0, n_chunks, step, 0)


def _inproj_kernel(h_ref, w_hbm, oa_ref, ob_ref, oc_ref, od_ref, wa_scr, wb_scr, wc_scr, wd_scr, stage_scr, sem, *,
                   layer, splits):
    w_scr = (wa_scr, wb_scr, wc_scr, wd_scr)

    @pl.when(pl.program_id(0) == 0)
    def _():
        for m, scr in enumerate(w_scr):
            n_rows = splits[m + 1] - splits[m]
            rpc = max(r for r in range(16, stage_scr.shape[1] + 1, 16) if n_rows % r == 0)

            def consume(r, chunk, scr=scr):
                scr[pl.ds(r, chunk.shape[0]), :] = chunk.astype(BF16)

            _stage_rows(w_hbm.at[layer], splits[m], n_rows, rpc, stage_scr, sem, consume)
            if scr.shape[0] > n_rows:
                scr[n_rows:, :] = jnp.zeros((scr.shape[0] - n_rows, scr.shape[1]), BF16)

    h = h_ref[...]
    for w, o_ref in zip(w_scr, (oa_ref, ob_ref, oc_ref, od_ref)):
        o_ref[...] = lax.dot_general(h, w[...], (((1,), (1,)), ((), ())), preferred_element_type=F32)


def _x_to_b_block(i, tm, lat_rows, seq, ctx_len):
    per_b = (seq + ctx_len) // tm
    lat_b = seq // tm
    ctx_b = ctx_len // tm
    n_lat = lat_rows // tm
    lat_idx = (i // lat_b) * per_b + ctx_b + i % lat_b
    ic = i - n_lat
    ctx_idx = (ic // ctx_b) * per_b + ic % ctx_b
    return jnp.where(i < n_lat, lat_idx, ctx_idx)


def _inproj(h, w_in, *, layer, lat_rows, seq, ctx_len, n_batch):
    rows, d = h.shape
    tm = 256
    stage_rows = 512
    splits = (0, SSD_COLS, SSD_COLS + LRU_COLS, SSD_COLS + LRU_COLS + HGRN_COLS, w_in.shape[1])
    widths = [SSD_PAD, LRU_COLS, HGRN_COLS, RET_COLS]

    def out_idx(i):
        return (_x_to_b_block(i, tm, lat_rows, seq, ctx_len), 0)

    return pl.pallas_call(
        functools.partial(_inproj_kernel, layer=layer, splits=splits),
        grid=(rows // tm,),
        in_specs=[pl.BlockSpec((tm, d), lambda i: (i, 0)), pl.BlockSpec(memory_space=pl.ANY)],
        out_specs=[pl.BlockSpec((tm, n), out_idx) for n in widths],
        out_shape=[jax.ShapeDtypeStruct((rows, n), F32) for n in widths],
        scratch_shapes=[pltpu.VMEM((n, d), BF16) for n in widths]
        + [pltpu.VMEM((2, stage_rows, d), F32), pltpu.SemaphoreType.DMA((2,))],
        compiler_params=_cparams(1),
        name="inproj",
    )(h, w_in)


def _outproj_kernel(x_ref, ya_ref, yb_ref, yc_ref, yd_ref, w_ref, mod_ref, gpost_ref, gnext_ref, o_ref, hn_ref):
    w = LRU_WIDTH
    n = EPI_ROWS
    for r in range(x_ref.shape[0] // n):
        rows = slice(r * n, (r + 1) * n)
        ycat = jnp.concatenate([ya_ref[rows, :], yb_ref[rows, :], yc_ref[rows, :], yd_ref[rows, :]], axis=1)
        y = jnp.dot(ycat, w_ref[...], preferred_element_type=F32)
        _residual_epilogue(rows, x_ref, y, mod_ref[5:6, :], gpost_ref[...], mod_ref, gnext_ref, 6, o_ref, hn_ref)


def _outproj(x, ys, w, mods, gpost, gpre, *, layer, n_rows, lat_rows, seq, ctx_len, n_batch):
    d = x.shape[1]
    tm = 256
    wy = ys[0].shape[1]
    blk = pl.BlockSpec((tm, d), lambda i: (i, 0))

    def y_idx(i):
        return (_x_to_b_block(i, tm, lat_rows, seq, ctx_len), 0)

    return pl.pallas_call(
        _outproj_kernel,
        grid=(n_rows // tm,),
        in_specs=[blk] + [pl.BlockSpec((tm, wy), y_idx)] * 4
        + [_layer_spec((4 * wy, d), layer),
           pl.BlockSpec((None, None, N_MOD, d), lambda i: _mod_row_idx(i, tm, layer, lat_rows, seq, n_batch)),
           pl.BlockSpec((None, 1, d), lambda i: (layer * 3 + 1, 0, 0)),
           pl.BlockSpec((None, 1, d), lambda i: (layer * 3 + 2, 0, 0))],
        out_specs=[blk, blk],
        out_shape=[jax.ShapeDtypeStruct((n_rows, d), F32), jax.ShapeDtypeStruct((n_rows, d), BF16)],
        compiler_params=_cparams(1),
        name="outproj",
    )(x, *ys, w, mods, gpost, gpre)


def _conv_chunk(u_ref, col0, ncols, base, first, last, t_tot, w_ref, b_ref):
    cols = slice(col0, col0 + ncols)
    prev = u_ref[pl.ds(pl.multiple_of(jnp.maximum(base - 8, 0), 8), 8), cols]
    nxt = u_ref[pl.ds(pl.multiple_of(jnp.minimum(base + CHUNK, t_tot - 8), 8), 8), cols]
    cur = u_ref[pl.ds(base, CHUNK), cols]
    prev = jnp.where(first, 0.0, prev)
    nxt = jnp.where(last, 0.0, nxt)
    win = jnp.concatenate([prev, cur, nxt], axis=0)
    n = CHUNK + 16
    xm1 = pltpu.roll(win, 1, axis=0)[8:8 + CHUNK]
    xp1 = pltpu.roll(win, n - 1, axis=0)[8:8 + CHUNK]
    xp2 = pltpu.roll(win, n - 2, axis=0)[8:8 + CHUNK]
    return (w_ref[0:1, :] * xm1 + w_ref[1:2, :] * cur + w_ref[2:3, :] * xp1 + w_ref[3:4, :] * xp2
            + b_ref[...])


def _chunk_order(i, nc_c, nc, reverse):
    if not reverse:
        return i
    return jnp.where(i < nc_c, nc_c - 1 - i, nc + nc_c - 1 - i)


def _stream_edges(c, nc_c, nc):
    first = jnp.logical_or(c == 0, c == nc_c)
    last = jnp.logical_or(c == nc_c - 1, c == nc - 1)
    return first, last


def _ssd_kernel(u_ref, cw_ref, cb_ref, dtb_ref, alog_ref, dskip_ref, nw_ref, o_ref,
                act_scr, dl_scr, y_scr, s_scr, *, nc_c, nc):
    t_tot = nc * CHUNK
    xbc0 = SSD_WIDTH
    dt0 = SSD_WIDTH + SSD_XBC

    def prep(c, _):
        base = pl.multiple_of(c * CHUNK, CHUNK)
        first, last = _stream_edges(c, nc_c, nc)
        conv = _conv_chunk(u_ref, xbc0, SSD_XBC, base, first, last, t_tot, cw_ref, cb_ref)
        act_scr[pl.ds(base, CHUNK), :] = _silu(conv)
        dt = u_ref[pl.ds(base, CHUNK), dt0:dt0 + LANES]
        dl_scr[pl.ds(base, CHUNK), :] = _softplus(dt + dtb_ref[...])
        return 0

    lax.fori_loop(0, nc, prep, 0)

    a_neg = -jnp.exp(alog_ref[...])
    row = _iota((CHUNK, CHUNK), 0)
    col = _iota((CHUNK, CHUNK), 1)
    lane_lo = _iota((CHUNK, LANES), 1) < SSD_HEADDIM
    grp_mask = [(_iota((CHUNK, LANES), 1) // SSD_STATE) == g for g in range(SSD_GROUPS)]

    s_scr[...] = jnp.zeros_like(s_scr)
    bodies = []
    for d in range(2):
        tri = (row >= col) if d == 0 else (row <= col)
        tri_b = tri.astype(BF16)
        edge = CHUNK - 1 if d == 0 else 0

        def body(i, d=d, tri=tri, tri_b=tri_b, edge=edge):
            c = _chunk_order(i, nc_c, nc, d == 1)
            base = pl.multiple_of(c * CHUNK, CHUNK)
            act = act_scr[pl.ds(base, CHUNK), :]
            bm = act[:, SSD_WIDTH:SSD_WIDTH + LANES]
            cm = act[:, SSD_WIDTH + LANES:SSD_WIDTH + 2 * LANES]
            delta = dl_scr[pl.ds(base, CHUNK), :]
            la = delta * a_neg
            cum_col = _dot_sel(tri_b, _split3(la))
            cum_row = cum_col.T
            delta_row = delta.T
            bt = bm.T
            gmat = [_bdot(jnp.where(grp_mask[g], cm, 0.0), bt) for g in range(SSD_GROUPS)]
            for k in range(SSD_HEADS // 2):
                g = (2 * k) // (SSD_HEADS // SSD_GROUPS)
                x_tile = act[:, k * LANES:(k + 1) * LANES]
                cg = jnp.where(grp_mask[g], cm, 0.0)
                m_list, w_list, e_list, tot_list = [], [], [], []
                for h in (2 * k, 2 * k + 1):
                    j = d * SSD_HEADS + h
                    ccol = cum_col[:, j:j + 1]
                    crow = cum_row[j:j + 1, :]
                    drow = delta_row[j:j + 1, :]
                    dec = jnp.where(tri, jnp.exp(jnp.where(tri, ccol - crow, 0.0)), 0.0)
                    m_list.append(gmat[g] * dec * drow)
                    tot = cum_row[j:j + 1, edge:edge + 1]
                    w_list.append(bt * (drow * jnp.exp(tot - crow)))
                    e_list.append(jnp.exp(ccol))
                    tot_list.append(jnp.exp(tot))
                y2 = _bdot(jnp.concatenate(m_list, axis=0), x_tile)
                u2 = _bdot(jnp.concatenate(w_list, axis=0), x_tile)
                s_old = s_scr[d, :, k * LANES:(k + 1) * LANES]
                y_inter = _bdot(cg, s_old) * jnp.where(lane_lo, e_list[0], e_list[1])
                y_tile = jnp.where(lane_lo, y2[:CHUNK], y2[CHUNK:]) + y_inter
                s_scr[d, :, k * LANES:(k + 1) * LANES] = (
                    s_old * jnp.where(lane_lo, tot_list[0], tot_list[1])
                    + jnp.where(lane_lo, u2[:CHUNK], u2[CHUNK:]))
                if d == 0:
                    y_tile = y_tile + x_tile * dskip_ref[:, k * LANES:(k + 1) * LANES]
                y_scr[d, pl.ds(base, CHUNK), k * LANES:(k + 1) * LANES] = y_tile

        bodies.append(body)

    def both(i, _):
        for step in range(2):
            bodies[0](2 * i + step)
            bodies[1](2 * i + step)
        return 0

    assert nc % 2 == 0
    lax.fori_loop(0, nc // 2, both, 0)

    def fin(c, _):
        base = pl.multiple_of(c * CHUNK, CHUNK)
        z = u_ref[pl.ds(base, CHUNK), 0:SSD_WIDTH]
        y = (y_scr[0, pl.ds(base, CHUNK), :] + y_scr[1, pl.ds(base, CHUNK), :]) * _silu(z)
        yn = y * lax.rsqrt(jnp.mean(y * y, axis=-1, keepdims=True) + EPS) * nw_ref[...]
        o_ref[pl.ds(base, CHUNK), :] = yn.astype(o_ref.dtype)
        return 0

    lax.fori_loop(0, nc, fin, 0)


def _ssd(u, conv_w, conv_b, dt_bias, a_log, d_skip, norm_w, *, layer, n_batch, t_ctx, t_lat):
    t_tot = t_ctx + t_lat
    nc_c, nc = t_ctx // CHUNK, t_tot // CHUNK
    return pl.pallas_call(
        functools.partial(_ssd_kernel, nc_c=nc_c, nc=nc),
        grid=(n_batch,),
        in_specs=[pl.BlockSpec((t_tot, SSD_PAD), lambda b: (b, 0)),
                  _layer_spec((CONV_W, SSD_XBC), layer), _layer_spec((1, SSD_XBC), layer),
                  _layer_spec((1, LANES), layer), _layer_spec((1, LANES), layer),
                  _layer_spec((1, SSD_WIDTH), layer), _layer_spec((1, SSD_WIDTH), layer)],
        out_specs=pl.BlockSpec((t_tot, SSD_WIDTH), lambda b: (b, 0)),
        out_shape=jax.ShapeDtypeStruct((n_batch * t_tot, SSD_WIDTH), BF16),
        scratch_shapes=[pltpu.VMEM((t_tot, SSD_XBC), F32), pltpu.VMEM((t_tot, LANES), F32),
                        pltpu.VMEM((2, t_tot, SSD_WIDTH), F32), pltpu.VMEM((2, CHUNK, SSD_WIDTH), F32)],
        compiler_params=_cparams(1),
        name="ssd",
    )(u, conv_w, conv_b, dt_bias, a_log, d_skip, norm_w)


def _lru_kernel(u_ref, cw_ref, cb_ref, wg_ref, bg_ref, lam_ref, o_ref, xc_scr, h_scr, *, nc_c, nc):
    t_tot = nc * CHUNK
    w = LRU_WIDTH

    def prep(c, _):
        base = pl.multiple_of(c * CHUNK, CHUNK)
        first, last = _stream_edges(c, nc_c, nc)
        xc_scr[pl.ds(base, CHUNK), :] = _conv_chunk(u_ref, 0, w, base, first, last, t_tot, cw_ref, cb_ref)
        return 0

    lax.fori_loop(0, nc, prep, 0)

    n_tiles = CHUNK // 8
    sub = _iota((n_tiles, 8, w), 1)

    bodies = []
    for d in range(2):
        coef = (-LRU_C * 1.4426950408889634) * _softplus(-lam_ref[d:d + 1, :])

        def body(i, h_prev, d=d, coef=coef):
            c = _chunk_order(i, nc_c, nc, d == 1)
            base = pl.multiple_of(c * CHUNK, CHUNK)
            xc = xc_scr[pl.ds(base, CHUNK), :]
            gates = _bdot(xc, wg_ref[d]) + bg_ref[d:d + 1, :]
            r = _sigmoid(gates[:, :w])
            ig = _sigmoid(gates[:, w:])
            a = jnp.exp2(r * coef)
            z = jnp.maximum(1.0 - a * a, 1e-12)
            b = (z * lax.rsqrt(z)) * (ig * xc)
            a = a.reshape(n_tiles, 8, w)
            b = b.reshape(n_tiles, 8, w)
            for k in (1, 2, 4):
                if d == 0:
                    keep = sub >= k
                    shift = k
                else:
                    keep = sub < 8 - k
                    shift = 8 - k
                a_s = jnp.where(keep, pltpu.roll(a, shift, axis=1), 1.0)
                b_s = jnp.where(keep, pltpu.roll(b, shift, axis=1), 0.0)
                b = a * b_s + b
                a = a * a_s
            tiles = range(n_tiles) if d == 0 else range(n_tiles - 1, -1, -1)
            outs = [None] * n_tiles
            for t in tiles:
                h_t = a[t] * h_prev + b[t]
                outs[t] = h_t
                h_prev = h_t[7:8] if d == 0 else h_t[0:1]
            h_scr[d, pl.ds(base, CHUNK), :] = jnp.concatenate(outs, axis=0)
            return h_prev

        bodies.append(body)

    def both(i, carry):
        hf, hb = carry
        for step in range(2):
            hf = bodies[0](2 * i + step, hf)
            hb = bodies[1](2 * i + step, hb)
        return hf, hb

    assert nc % 2 == 0
    zero = jnp.zeros((1, w), F32)
    lax.fori_loop(0, nc // 2, both, (zero, zero))

    def fin(c, _):
        base = pl.multiple_of(c * CHUNK, CHUNK)
        gate = u_ref[pl.ds(base, CHUNK), w:2 * w]
        out = (h_scr[0, pl.ds(base, CHUNK), :] + h_scr[1, pl.ds(base, CHUNK), :]) * jax.nn.gelu(gate)
        o_ref[pl.ds(base, CHUNK), :] = out.astype(o_ref.dtype)
        return 0

    lax.fori_loop(0, nc, fin, 0)


def _block_diag(wb):
    nb, bi, bj = wb.shape[-3:]
    eye = jnp.eye(nb, dtype=wb.dtype)
    return jnp.einsum('...hij,hg->...higj', wb, eye).reshape(wb.shape[:-3] + (nb * bi, nb * bj))


def _lru(u, conv_w, conv_b, wg, bg, lam, *, layer, n_batch, t_ctx, t_lat):
    t_tot = t_ctx + t_lat
    nc_c, nc = t_ctx // CHUNK, t_tot // CHUNK
    w = LRU_WIDTH
    return pl.pallas_call(
        functools.partial(_lru_kernel, nc_c=nc_c, nc=nc),
        grid=(n_batch,),
        in_specs=[pl.BlockSpec((t_tot, LRU_COLS), lambda b: (b, 0)),
                  _layer_spec((CONV_W, w), layer), _layer_spec((1, w), layer),
                  _layer_spec((2, w, 2 * w), layer), _layer_spec((2, 2 * w), layer), _layer_spec((2, w), layer)],
        out_specs=pl.BlockSpec((t_tot, w), lambda b: (b, 0)),
        out_shape=jax.ShapeDtypeStruct((n_batch * t_tot, w), BF16),
        scratch_shapes=[pltpu.VMEM((t_tot, w), F32), pltpu.VMEM((2, t_tot, w), F32)],
        compiler_params=_cparams(1),
        name="rglru",
    )(u, conv_w, conv_b, wg, bg, lam)


def _hgrn_kernel(q_ref, ff_ref, fb_ref, i_ref, g_ref, lbl_ref, nw_ref, esel_ref, o_ref, of_scr, ob_scr, st_scr, *,
                 layer, nc_c, nc, n_heads):
    n_sub = CHUNK // SUB
    row = _iota((CHUNK, CHUNK), 0)
    col = _iota((CHUNK, CHUNK), 1)
    scale = HGRN_HEADDIM ** -0.5
    log2e = 1.4426950408889634

    lbs = []
    for d in range(2):
        logits = lbl_ref[d]
        ex = jnp.exp(logits - jnp.max(logits, axis=0, keepdims=True))
        sm = ex / jnp.sum(ex, axis=0, keepdims=True)
        lbs.append(jnp.sum(sm[0:layer + 1], axis=0, keepdims=True) - sm[0:1])
    causal = [row >= col, row <= col]
    tri_full = [m.astype(BF16) for m in causal]
    valids = [jnp.logical_and(row // SUB == col // SUB, m) for m in causal]
    levels = tuple(SUB << n for n in range(1, (CHUNK // SUB).bit_length()))
    tpb = SUB // 8
    q_side = [[((row % g) >= g // 2) if d == 0 else ((row % g) < g // 2) for g in levels] for d in range(2)]
    pair_mask = [[jnp.logical_and(row // g == col // g,
                                  jnp.logical_and(q_side[d][n], ((col % g) < g // 2) if d == 0
                                                  else ((col % g) >= g // 2)))
                  for n, g in enumerate(levels)] for d in range(2)]
    st_scr[...] = jnp.zeros_like(st_scr)

    def chunk(d, hh, base):
        cols = slice(hh * LANES, (hh + 1) * LANES)
        lb = lbs[d][:, cols]
        f_ref = ff_ref if d == 0 else fb_ref
        qh = _silu(q_ref[pl.ds(base, CHUNK), cols]) * scale
        f = lb + (1.0 - lb) * _sigmoid(f_ref[pl.ds(base, CHUNK), cols])
        kk = 1.0 - f
        gl = jnp.log(f) * log2e
        v = i_ref[pl.ds(base, CHUNK), cols]
        cum = _dot_sel(tri_full[d], _split3(gl))
        halves = []
        nb = n_sub // 2
        for r0 in range(0, CHUNK, CHUNK // 2):
            rows = slice(r0, r0 + CHUNK // 2)
            q4 = qh[rows].reshape(nb, tpb, 8, LANES)
            c4 = cum[rows].reshape(nb, tpb, 8, LANES)
            k4 = kk[rows].reshape(nb, tpb, 8, LANES)
            zero = jnp.zeros((nb, 8, LANES), F32)
            acc = None
            for s0 in range(0, SUB, 8):
                ps = []
                for s in range(s0, s0 + 8):
                    ts, rs = s // 8, s % 8
                    cs = c4[:, ts, rs:rs + 1, :]
                    ks = k4[:, ts, rs:rs + 1, :]
                    tiles = []
                    for tt in range(tpb):
                        if tt == ts:
                            e = jnp.exp2(jnp.minimum(c4[:, tt] - cs, 0.0))
                        elif (tt > ts) == (d == 0):
                            e = jnp.exp2(c4[:, tt] - cs)
                        else:
                            tiles.append(zero)
                            continue
                        tiles.append((q4[:, tt] * e) * ks)
                    ps.append(jnp.stack(tiles, axis=1).reshape(CHUNK // 2, LANES).astype(BF16))
                part = jnp.dot(jnp.concatenate(ps, axis=1), esel_ref[s0 * LANES:(s0 + 8) * LANES, :],
                               preferred_element_type=F32)
                acc = part if acc is None else acc + part
            halves.append(acc)
        scores = jnp.where(valids[d], jnp.concatenate(halves, axis=0), 0.0)
        for n, g in enumerate(levels):
            pieces = []
            for g0 in range(0, CHUNK, g):
                b = g0 + g // 2 - 1 if d == 0 else g0 + g // 2
                pieces.append(jnp.exp2(-jnp.abs(cum[g0:g0 + g] - cum[b:b + 1])))
            w = pieces[0] if len(pieces) == 1 else jnp.concatenate(pieces, axis=0)
            xg = (jnp.where(q_side[d][n], qh, kk) * w).astype(BF16)
            scores = scores + jnp.where(pair_mask[d][n], _bdot_nt(xg, xg), 0.0)
        y = _bdot(scores, v)
        edge = CHUNK - 1 if d == 0 else 0
        tot = cum[edge:edge + 1]
        st = st_scr[d, hh]
        y = y + _bdot_nt(qh * jnp.exp2(cum), st)
        st_scr[d, hh] = st * jnp.exp2(tot) + _bdot(v.T, kk * jnp.exp2(tot - cum))
        return y

    def body(i, _):
        for step in range(2):
            for d in range(2):
                c = _chunk_order(2 * i + step, nc_c, nc, d == 1)
                base = pl.multiple_of(c * CHUNK, CHUNK)
                o_scr = of_scr if d == 0 else ob_scr
                for hh in range(n_heads):
                    o_scr[pl.ds(base, CHUNK), hh * LANES:(hh + 1) * LANES] = chunk(d, hh, base)
        return 0

    assert nc % 2 == 0
    lax.fori_loop(0, nc // 2, body, 0)

    def fin(c, _):
        base = pl.multiple_of(c * CHUNK, CHUNK)
        for hh in range(n_heads):
            cols = slice(hh * LANES, (hh + 1) * LANES)
            o = of_scr[pl.ds(base, CHUNK), cols] + ob_scr[pl.ds(base, CHUNK), cols]
            on = o * lax.rsqrt(jnp.mean(o * o, axis=-1, keepdims=True) + EPS) * nw_ref[...]
            out = on * _silu(g_ref[pl.ds(base, CHUNK), cols])
            o_ref[pl.ds(base, CHUNK), cols] = out.astype(o_ref.dtype)
        return 0

    lax.fori_loop(0, nc, fin, 0)


def _hgrn_selector():
    s_of_row = np.arange(SUB * LANES) // LANES
    return jnp.asarray(s_of_row[:, None] == (np.arange(LANES)[None, :] % SUB), dtype=BF16)


def _hgrn(u, lb_logits, norm_w, *, layer, n_batch, t_ctx, t_lat):
    t_tot = t_ctx + t_lat
    nc_c, nc = t_ctx // CHUNK, t_tot // CHUNK
    n_heads = 2
    ng = HGRN_HEADS // n_heads
    wblk = n_heads * LANES
    depth = lb_logits.shape[1]
    col = lambda part: pl.BlockSpec((t_tot, wblk), lambda b, h, part=part: (b, part * ng + h))
    return pl.pallas_call(
        functools.partial(_hgrn_kernel, layer=layer, nc_c=nc_c, nc=nc, n_heads=n_heads),
        grid=(n_batch, ng),
        in_specs=[col(0), col(1), col(2), col(3), col(4),
                  pl.BlockSpec((2, depth, wblk), lambda b, h: (0, 0, h)),
                  _layer_spec((1, LANES), layer),
                  _const_spec((SUB * LANES, LANES))],
        out_specs=pl.BlockSpec((t_tot, wblk), lambda b, h: (b, h)),
        out_shape=jax.ShapeDtypeStruct((n_batch * t_tot, HGRN_WIDTH), BF16),
        scratch_shapes=[pltpu.VMEM((t_tot, wblk), F32), pltpu.VMEM((t_tot, wblk), F32),
                        pltpu.VMEM((2, n_heads, HGRN_HEADDIM, HGRN_HEADDIM), F32)],
        compiler_params=_cparams(2),
        name="hgrn2",
    )(u, u, u, u, u, lb_logits, norm_w, _hgrn_selector())


def _ret_consts(t_ctx, t_lat):
    gam = 1.0 - np.exp2(-5.0 - np.arange(RET_HEADS, dtype=np.float64))
    t = np.arange(CHUNK, dtype=np.float64)
    diff = np.abs(t[:, None] - t[None, :])
    dsym = gam[:, None, None] ** diff[None]
    dsym[:, np.arange(CHUNK), np.arange(CHUNK)] = 2.0
    head_of_lane = np.repeat(np.arange(RET_HEADS), RET_QK_DIM)
    qf = gam[head_of_lane][None, :] ** (t[:, None] + 1.0)
    qb = gam[head_of_lane][None, :] ** (CHUNK - t[:, None])
    kf = gam[:, None] ** (CHUNK - 1.0 - t[None, :])
    kb = gam[:, None] ** t[None, :]
    gtot = gam ** CHUNK
    pos = np.arange(t_lat)
    rows_p, cols_p = pos // GRID_W, pos % GRID_W
    quarter = RET_QK_DIM // 4
    inv = ROPE_BASE ** (-np.arange(quarter, dtype=np.float64) / quarter)
    ang_r = rows_p[:, None] * inv[None, :]
    ang_c = cols_p[:, None] * inv[None, :]
    cos_h = np.concatenate([np.cos(ang_r), np.cos(ang_r), np.cos(ang_c), np.cos(ang_c)], axis=1)
    sin_h = np.concatenate([-np.sin(ang_r), np.sin(ang_r), -np.sin(ang_c), np.sin(ang_c)], axis=1)
    cos_t = np.concatenate([np.ones((t_ctx, RET_QK_DIM)), cos_h], axis=0)
    sin_t = np.concatenate([np.zeros((t_ctx, RET_QK_DIM)), sin_h], axis=0)
    cos_t = np.tile(cos_t, (1, RET_HEADS))
    sin_t = np.tile(sin_t, (1, RET_HEADS))
    f32 = lambda a: jnp.asarray(a, dtype=F32)
    return (f32(dsym), f32(qf), f32(qb), f32(np.concatenate([kf, kb], axis=0)), [float(g) for g in gtot],
            f32(cos_t), f32(sin_t))


def _ret_kernel(u_ref, dsym_ref, qf_ref, qb_ref, kfac_ref, cos_ref, sin_ref, o_ref, o_scr, s_scr, q_scr, kt_scr, *,
                gtot, nc_c, nc):
    nqk = RET_HEADS * RET_QK_DIM
    lane = _iota((CHUNK, nqk), 1)
    low_half = (lane % (RET_QK_DIM // 2)) < (RET_QK_DIM // 4)
    tile_lane = _iota((CHUNK, LANES), 1)
    head_mask = [(tile_lane // RET_QK_DIM) == p for p in range(2)]
    kscale = RET_QK_DIM ** -0.5

    def rope(x, cos, sin):
        swapped = jnp.where(low_half, pltpu.roll(x, nqk - RET_QK_DIM // 4, axis=1),
                            pltpu.roll(x, RET_QK_DIM // 4, axis=1))
        return x * cos + swapped * sin

    def v_of(base, h):
        return u_ref[pl.ds(base, CHUNK), 2 * nqk + h * RET_V_DIM:2 * nqk + (h + 1) * RET_V_DIM]

    def prep(c):
        base = pl.multiple_of(c * CHUNK, CHUNK)
        cos = cos_ref[pl.ds(base, CHUNK), :]
        sin = sin_ref[pl.ds(base, CHUNK), :]
        q = rope(u_ref[pl.ds(base, CHUNK), 0:nqk], cos, sin)
        k = rope(u_ref[pl.ds(base, CHUNK), nqk:2 * nqk], cos, sin) * kscale
        kt = k.T
        q_scr[pl.ds(base, CHUNK), :] = q
        kt_scr[c] = kt
        for h in range(RET_HEADS):
            p = h // 2
            raw = _bdot(jnp.where(head_mask[h % 2], q[:, p * LANES:(p + 1) * LANES], 0.0),
                        kt[p * LANES:(p + 1) * LANES])
            o_scr[2, pl.ds(base, CHUNK), h * RET_V_DIM:(h + 1) * RET_V_DIM] = _bdot(raw * dsym_ref[h], v_of(base, h))

    def prep_pair(i, _):
        prep(2 * i)
        prep(2 * i + 1)
        return 0

    assert nc % 2 == 0
    lax.fori_loop(0, nc // 2, prep_pair, 0)

    s_scr[...] = jnp.zeros_like(s_scr)
    bodies = []
    for d in range(2):
        qfac = qf_ref if d == 0 else qb_ref

        def body(i, d=d, qfac=qfac):
            c = _chunk_order(i, nc_c, nc, d == 1)
            base = pl.multiple_of(c * CHUNK, CHUNK)
            qd = q_scr[pl.ds(base, CHUNK), :] * qfac[...]
            kt = kt_scr[c]
            for h in range(RET_HEADS):
                p = h // 2
                kt_p = kt[p * LANES:(p + 1) * LANES]
                s_old = s_scr[d, h]
                y = _bdot(jnp.where(head_mask[h % 2], qd[:, p * LANES:(p + 1) * LANES], 0.0), s_old)
                kfac = kfac_ref[d * RET_HEADS + h:d * RET_HEADS + h + 1, :]
                s_scr[d, h] = gtot[h] * s_old + _bdot(kt_p * kfac, v_of(base, h))
                o_scr[d, pl.ds(base, CHUNK), h * RET_V_DIM:(h + 1) * RET_V_DIM] = y

        bodies.append(body)

    def both(i, _):
        for step in range(2):
            bodies[0](2 * i + step)
            bodies[1](2 * i + step)
        return 0

    assert nc % 2 == 0
    lax.fori_loop(0, nc // 2, both, 0)

    def fin(c, _):
        base = pl.multiple_of(c * CHUNK, CHUNK)
        for h in range(RET_HEADS):
            cols = slice(h * RET_V_DIM, (h + 1) * RET_V_DIM)
            o = (o_scr[0, pl.ds(base, CHUNK), cols] + o_scr[1, pl.ds(base, CHUNK), cols]
                 + o_scr[2, pl.ds(base, CHUNK), cols])
            on = o * lax.rsqrt(jnp.mean(o * o, axis=-1, keepdims=True) + EPS)
            gate = u_ref[pl.ds(base, CHUNK), 2 * nqk + RET_WIDTH + h * RET_V_DIM:
                         2 * nqk + RET_WIDTH + (h + 1) * RET_V_DIM]
            o_ref[pl.ds(base, CHUNK), cols] = (_silu(gate) * on).astype(o_ref.dtype)
        return 0

    lax.fori_loop(0, nc, fin, 0)


def _ret(u, *, n_batch, t_ctx, t_lat):
    t_tot = t_ctx + t_lat
    nc_c, nc = t_ctx // CHUNK, t_tot // CHUNK
    dsym, qf, qb, kfac, gtot, cos_t, sin_t = _ret_consts(t_ctx, t_lat)
    nqk = RET_HEADS * RET_QK_DIM
    return pl.pallas_call(
        functools.partial(_ret_kernel, gtot=gtot, nc_c=nc_c, nc=nc),
        grid=(n_batch,),
        in_specs=[pl.BlockSpec((t_tot, RET_COLS), lambda b: (b, 0)),
                  _const_spec((RET_HEADS, CHUNK, CHUNK)), _const_spec((CHUNK, nqk)), _const_spec((CHUNK, nqk)),
                  _const_spec((2 * RET_HEADS, CHUNK)), _const_spec((t_tot, nqk)), _const_spec((t_tot, nqk))],
        out_specs=pl.BlockSpec((t_tot, RET_WIDTH), lambda b: (b, 0)),
        out_shape=jax.ShapeDtypeStruct((n_batch * t_tot, RET_WIDTH), BF16),
        scratch_shapes=[pltpu.VMEM((3, t_tot, RET_WIDTH), F32),
                        pltpu.VMEM((2, RET_HEADS, LANES, RET_V_DIM), F32),
                        pltpu.VMEM((t_tot, nqk), F32), pltpu.VMEM((nc, nqk, CHUNK), F32)],
        compiler_params=_cparams(1),
        name="retention",
    )(u, dsym, qf, qb, kfac, cos_t, sin_t)


def kernel(x, c, ctx, c_ctx, w_mod, b_mod, norm_pre, norm_post, ffn_w1, ffn_w3, ffn_w2, w_in, w_out,
           ssd_conv_w, ssd_conv_b, ssd_dt_bias, ssd_a_log, ssd_d, ssd_norm_w,
           lru_conv_w, lru_conv_b, lru_wa, lru_ba, lru_wx, lru_bx, lru_lambda,
           hgrn_lb_logits, hgrn_norm_w):
    n_batch, seq, d = x.shape
    ctx_len = ctx.shape[1]
    depth = w_mod.shape[0]
    lat_rows = n_batch * seq
    all_rows = lat_rows + n_batch * ctx_len
    geom = dict(lat_rows=lat_rows, seq=seq, ctx_len=ctx_len, n_batch=n_batch)
    mix = dict(n_batch=n_batch, t_ctx=ctx_len, t_lat=seq)

    c_all = jnp.concatenate([c, c_ctx[None, :], jnp.zeros((8 - n_batch - 1, d), F32)], axis=0)
    mods = _modulation(c_all, w_mod, b_mod).reshape(depth, 8, N_MOD, d)
    w2 = ffn_w2
    w_in_parts = jnp.swapaxes(w_in, 1, 2)
    w_out_b = w_out.astype(BF16)
    gpre = norm_pre.reshape(depth * 3, 1, d)
    gpost = norm_post.reshape(depth * 3, 1, d)
    rowvec = lambda v: v.reshape(depth, 1, -1)
    lane_pad = lambda v: jnp.pad(v.reshape(depth, 1, -1), ((0, 0), (0, 0), (0, LANES - v[0].size)))
    ssd_params = (ssd_conv_w, rowvec(ssd_conv_b), lane_pad(ssd_dt_bias), lane_pad(ssd_a_log),
                  rowvec(jnp.repeat(ssd_d, SSD_HEADDIM, axis=-1)), rowvec(ssd_norm_w))
    lru_wg = jnp.concatenate([_block_diag(lru_wa), _block_diag(lru_wx)], axis=-1).astype(BF16)
    lru_bg = jnp.concatenate([lru_ba.reshape(depth, 2, -1), lru_bx.reshape(depth, 2, -1)], axis=-1)
    lru_params = (lru_conv_w, rowvec(lru_conv_b), lru_wg, lru_bg, lru_lambda)
    hgrn_nw = rowvec(hgrn_norm_w)

    rowgeom = dict(lat_rows=lat_rows, seq=seq, n_batch=n_batch)
    xs, h = _prenorm(x.reshape(lat_rows, d), ctx.reshape(n_batch * ctx_len, d), mods, gpre, **rowgeom)
    for l in range(depth):
        last = l == depth - 1
        g = _ffn_up(h, ffn_w1, ffn_w3, layer=l, k=0, n_rows=all_rows)
        xs, h = _ffn_down(g, xs, mods, gpost, gpre, w2, layer=l, k=0, j=0, nxt=3, nxt_layer=l,
                          n_rows=all_rows, **rowgeom)
        ua, ub, uc, ud = _inproj(h, w_in_parts, layer=l, **geom)
        ya = _ssd(ua, *ssd_params, layer=l, **mix)
        yb = _lru(ub, *lru_params, layer=l, **mix)
        yc = _hgrn(uc, hgrn_lb_logits, hgrn_nw, layer=l, **mix)
        yd = _ret(ud, **mix)
        n_rows = lat_rows if last else all_rows
        xs, h = _outproj(xs, (ya, yb, yc, yd), w_out_b, mods, gpost, gpre, layer=l, n_rows=n_rows, **geom)
        g = _ffn_up(h, ffn_w1, ffn_w3, layer=l, k=1, n_rows=n_rows)
        if last:
            (xs,) = _ffn_down(g, xs, mods, gpost, gpre, w2, layer=l, k=1, j=6, nxt=None, nxt_layer=l,
                              n_rows=n_rows, **rowgeom)
        else:
            xs, h = _ffn_down(g, xs, mods, gpost, gpre, w2, layer=l, k=1, j=6, nxt=0, nxt_layer=l + 1,
                              n_rows=n_rows, **rowgeom)
    return xs.reshape(n_batch, seq, d)
```

```python
import functools

import numpy as np
import jax
import jax.numpy as jnp
from jax import lax
from jax.experimental import pallas as pl
from jax.experimental.pallas import tpu as pltpu

F32 = jnp.float32
BF16 = jnp.bfloat16

N_MOD = 9
CONV_W = 4
EPS = 1e-6
GRID_W = 64

SSD_WIDTH = 512
SSD_HEADDIM = 64
SSD_HEADS = 8
SSD_GROUPS = 2
SSD_STATE = 64
SSD_XBC = SSD_WIDTH + 2 * SSD_GROUPS * SSD_STATE
SSD_COLS = SSD_WIDTH + SSD_XBC + 2 * SSD_HEADS
SSD_PAD = 1408
LRU_WIDTH = 512
LRU_C = 8.0
LRU_COLS = 2 * LRU_WIDTH
HGRN_WIDTH = 512
HGRN_HEADDIM = 128
HGRN_HEADS = 4
HGRN_COLS = 5 * HGRN_WIDTH
RET_WIDTH = 512
RET_V_DIM = 128
RET_HEADS = 4
RET_QK_DIM = 64
RET_COLS = 2 * RET_HEADS * RET_QK_DIM + 2 * RET_WIDTH
ROPE_BASE = 10000.0

LANES = 128
CHUNK = 128
SUB = 8
VMEM_LIMIT = 56 * 1024 * 1024


def _cparams(n_axes):
    return pltpu.CompilerParams(dimension_semantics=("arbitrary",) * n_axes,
                                vmem_limit_bytes=VMEM_LIMIT)


def _bdot(a, b):
    return jnp.dot(a.astype(BF16), b.astype(BF16), preferred_element_type=F32)


def _bdot_nt(a, b):
    return lax.dot_general(a.astype(BF16), b.astype(BF16), (((1,), (1,)), ((), ())),
                           preferred_element_type=F32)


def _split3(x):
    hi = x.astype(BF16)
    r1 = x - hi.astype(F32)
    mid = r1.astype(BF16)
    lo = (r1 - mid.astype(F32)).astype(BF16)
    return hi, mid, lo


def _dot_sel(sel, parts):
    return (jnp.dot(sel, parts[0], preferred_element_type=F32)
            + jnp.dot(sel, parts[1], preferred_element_type=F32)
            + jnp.dot(sel, parts[2], preferred_element_type=F32))


def _sigmoid(x):
    return 1.0 / (1.0 + jnp.exp(-x))


def _silu(x):
    return x * _sigmoid(x)


def _softplus(x):
    return jnp.maximum(x, 0.0) + jnp.log(1.0 + jnp.exp(-jnp.abs(x)))


def _iota(shape, dim):
    return lax.broadcasted_iota(jnp.int32, shape, dim)


def _const_spec(shape):
    return pl.BlockSpec(shape, lambda *_: (0,) * len(shape))


def _layer_spec(shape, layer):
    return pl.BlockSpec((None,) + shape, lambda *_: (layer,) + (0,) * len(shape))


def _mod_kernel(c_ref, w_ref, b_ref, o_ref):
    o_ref[...] = _bdot(_silu(c_ref[...]), w_ref[...]) + b_ref[...]


def _modulation(c_all, w_mod, b_mod):
    depth, d, n = w_mod.shape
    tn = 2048
    return pl.pallas_call(
        _mod_kernel,
        grid=(depth, n // tn),
        in_specs=[pl.BlockSpec((8, d), lambda l, j: (0, 0)),
                  pl.BlockSpec((None, d, tn), lambda l, j: (l, 0, j)),
                  pl.BlockSpec((None, 1, tn), lambda l, j: (l, 0, j))],
        out_specs=pl.BlockSpec((None, 8, tn), lambda l, j: (l, 0, j)),
        out_shape=jax.ShapeDtypeStruct((depth, 8, n), F32),
        compiler_params=_cparams(2),
        name="modulation",
    )(c_all, w_mod, b_mod.reshape(depth, 1, n))


def _mod_row_idx(i, tm, layer, lat_rows, seq, n_batch):
    return (layer, jnp.where(i * tm < lat_rows, (i * tm) // seq, n_batch), 0, 0)


def _mod_norm(x, g, shift, scale):
    return (x * lax.rsqrt(jnp.mean(x * x, axis=-1, keepdims=True) + EPS) * (g * (1.0 + scale))
            + shift).astype(BF16)


def _prenorm_kernel(xl_ref, xc_ref, mod_ref, g_ref, h_ref, *, n_split):
    x = jnp.where(pl.program_id(0) < n_split, xl_ref[...], xc_ref[...])
    h_ref[...] = _mod_norm(x, g_ref[...], mod_ref[0:1, :], mod_ref[1:2, :])


def _prenorm(xl, xc, mods, gpre, *, lat_rows, seq, n_batch):
    d = xl.shape[1]
    rows = xl.shape[0] + xc.shape[0]
    tm = 512
    n_split = xl.shape[0] // tm
    return pl.pallas_call(
        functools.partial(_prenorm_kernel, n_split=n_split),
        grid=(rows // tm,),
        in_specs=[pl.BlockSpec((tm, d), lambda i: (jnp.minimum(i, n_split - 1), 0)),
                  pl.BlockSpec((tm, d), lambda i: (jnp.maximum(i - n_split, 0), 0)),
                  pl.BlockSpec((None, None, N_MOD, d), lambda i: _mod_row_idx(i, tm, 0, lat_rows, seq, n_batch)),
                  pl.BlockSpec((None, 1, d), lambda i: (0, 0, 0))],
        out_specs=pl.BlockSpec((tm, d), lambda i: (i, 0)),
        out_shape=jax.ShapeDtypeStruct((rows, d), BF16),
        compiler_params=_cparams(1),
        name="prenorm",
    )(xl, xc, mods, gpre)


def _ffn_up_kernel(h_ref, w1_ref, w3_ref, g_ref, w1_scr, w3_scr, *, n_sub):
    @pl.when(pl.program_id(1) == 0)
    def _():
        w1_scr[...] = w1_ref[...].astype(BF16)
        w3_scr[...] = w3_ref[...].astype(BF16)

    rows = h_ref.shape[0] // n_sub
    for r in range(n_sub):
        h = h_ref[r * rows:(r + 1) * rows, :]
        a = jnp.dot(h, w1_scr[...], preferred_element_type=F32)
        b = jnp.dot(h, w3_scr[...], preferred_element_type=F32)
        g_ref[r * rows:(r + 1) * rows, :] = (_silu(a) * b).astype(BF16)


def _ffn_up(h, w1, w3, *, layer, k, n_rows):
    d = h.shape[1]
    dff = w1.shape[-1]
    n_sub = 4
    tm, tf = n_rows // 4, 512
    w_spec = pl.BlockSpec((None, None, d, tf), lambda f, i: (layer, k, 0, f))
    return pl.pallas_call(
        functools.partial(_ffn_up_kernel, n_sub=n_sub),
        grid=(dff // tf, n_rows // tm),
        in_specs=[pl.BlockSpec((tm, d), lambda f, i: (i, 0)), w_spec, w_spec],
        out_specs=pl.BlockSpec((tm, tf), lambda f, i: (i, f)),
        out_shape=jax.ShapeDtypeStruct((n_rows, dff), BF16),
        scratch_shapes=[pltpu.VMEM((d, tf), BF16), pltpu.VMEM((d, tf), BF16)],
        compiler_params=_cparams(2),
        name="ffn_up",
    )(h, w1, w3)


EPI_ROWS = 128


def _residual_epilogue(rows, x, y, gate, gpost, mod_ref, gnext_ref, nxt, o_ref, hn_ref):
    xn = x + y * lax.rsqrt(jnp.mean(y * y, axis=-1, keepdims=True) + EPS) * (gate * gpost)
    o_ref[rows, :] = xn
    if nxt is not None:
        hn_ref[rows, :] = _mod_norm(xn, gnext_ref[...], mod_ref[nxt:nxt + 1, :], mod_ref[nxt + 1:nxt + 2, :])


def _ffn_down_kernel(*refs, layer, k, j, nxt, nxt_mod_own, n_split):
    g_ref = refs[0]
    n_x = 1 if n_split is None else 2
    x_refs = refs[1:1 + n_x]
    mod_ref, gpost_ref = refs[1 + n_x:3 + n_x]
    pos = 3 + n_x
    nmod_ref = mod_ref
    gnext_ref = hn_ref = None
    if nxt is not None:
        if not nxt_mod_own:
            nmod_ref = refs[pos]
            pos += 1
        gnext_ref = refs[pos]
        pos += 1
    w2_hbm, o_ref = refs[pos], refs[pos + 1]
    pos += 2
    if nxt is not None:
        hn_ref = refs[pos]
        pos += 1
    w2_scr, stage_scr, sem = refs[pos], refs[pos + 1], refs[pos + 2]

    @pl.when(pl.program_id(0) == 0)
    def _():
        def consume(r0, chunk):
            w2_scr[pl.ds(r0, chunk.shape[0]), :] = chunk.astype(BF16)

        _stage_rows(w2_hbm.at[layer, k], 0, w2_hbm.shape[2], stage_scr.shape[1], stage_scr, sem, consume)

    n = EPI_ROWS
    for r in range(g_ref.shape[0] // n):
        rows = slice(r * n, (r + 1) * n)
        y = jnp.dot(g_ref[rows, :], w2_scr[...], preferred_element_type=F32)
        if n_split is None:
            x = x_refs[0][rows, :]
        else:
            x = jnp.where(pl.program_id(0) < n_split, x_refs[0][rows, :], x_refs[1][rows, :])
        _residual_epilogue(rows, x, y, 0.5 * mod_ref[j + 2:j + 3, :], gpost_ref[...], nmod_ref, gnext_ref,
                           nxt, o_ref, hn_ref)


def _ffn_down(g, x, mods, gpost, gpre, w2, *, layer, k, j, nxt, nxt_layer, n_rows, lat_rows, seq, n_batch):
    xs = x if isinstance(x, tuple) else (x,)
    d = xs[0].shape[1]
    dff = g.shape[1]
    tm = 256
    norm_row = layer * 3 + (0 if j == 0 else 2)
    blk = pl.BlockSpec((tm, d), lambda i: (i, 0))
    mod_spec = lambda ly: pl.BlockSpec((None, None, N_MOD, d),
                                       lambda i: _mod_row_idx(i, tm, ly, lat_rows, seq, n_batch))
    if len(xs) == 1:
        n_split = None
        x_specs = [blk]
    else:
        n_split = xs[0].shape[0] // tm
        x_specs = [pl.BlockSpec((tm, d), lambda i: (jnp.minimum(i, n_split - 1), 0)),
                   pl.BlockSpec((tm, d), lambda i: (jnp.maximum(i - n_split, 0), 0))]
    in_specs = [pl.BlockSpec((tm, dff), lambda i: (i, 0))] + x_specs + [
        mod_spec(layer), pl.BlockSpec((None, 1, d), lambda i: (norm_row, 0, 0))]
    args = [g, *xs, mods, gpost]
    out_specs, out_shape = [blk], [jax.ShapeDtypeStruct((n_rows, d), F32)]
    nxt_mod_own = nxt_layer == layer
    if nxt is not None:
        if not nxt_mod_own:
            in_specs.append(mod_spec(nxt_layer))
            args.append(mods)
        in_specs.append(pl.BlockSpec((None, 1, d), lambda i: (nxt_layer * 3 + nxt // 3, 0, 0)))
        args.append(gpre)
        out_specs.append(blk)
        out_shape.append(jax.ShapeDtypeStruct((n_rows, d), BF16))
    in_specs.append(pl.BlockSpec(memory_space=pl.ANY))
    args.append(w2)
    return pl.pallas_call(
        functools.partial(_ffn_down_kernel, layer=layer, k=k, j=j, nxt=nxt, nxt_mod_own=nxt_mod_own,
                          n_split=n_split),
        grid=(n_rows // tm,),
        in_specs=in_specs,
        out_specs=out_specs,
        out_shape=out_shape,
        scratch_shapes=[pltpu.VMEM((dff, d), BF16), pltpu.VMEM((2, 512, d), F32), pltpu.SemaphoreType.DMA((2,))],
        compiler_params=_cparams(1),
        name="ffn_down",
    )(*args)


def _stage_rows(src_hbm, row0, n_rows, rpc, stage_scr, sem, consume):
    n_chunks = n_rows // rpc
    assert n_chunks * rpc == n_rows and rpc <= stage_scr.shape[1]

    def copy(c, slot):
        return pltpu.make_async_copy(src_hbm.at[pl.ds(row0 + c * rpc, rpc)],
                                     stage_scr.at[slot, pl.ds(0, rpc)], sem.at[slot])

    copy(0, 0).start()

    def step(c, _):
        slot = c % 2

        @pl.when(c + 1 < n_chunks)
        def _():
            copy(c + 1, 1 - slot).start()

        copy(c, slot).wait()
        consume(pl.multiple_of(c * rpc, 16), stage_scr[slot, pl.ds(0, rpc), :])
        return 0

    lax.fori_loop(0, n_chunks, step, 0)


def _inproj_kernel(h_ref, w_hbm, oa_ref, ob_ref, oc_ref, od_ref, wa_scr, wb_scr, wc_scr, wd_scr, stage_scr, sem, *,
                   layer, splits):
    w_scr = (wa_scr, wb_scr, wc_scr, wd_scr)

    @pl.when(pl.program_id(0) == 0)
    def _():
        for m, scr in enumerate(w_scr):
            n_rows = splits[m + 1] - splits[m]
            rpc = max(r for r in range(16, stage_scr.shape[1] + 1, 16) if n_rows % r == 0)

            def consume(r, chunk, scr=scr):
                scr[pl.ds(r, chunk.shape[0]), :] = chunk.astype(BF16)

            _stage_rows(w_hbm.at[layer], splits[m], n_rows, rpc, stage_scr, sem, consume)
            if scr.shape[0] > n_rows:
                scr[n_rows:, :] = jnp.zeros((scr.shape[0] - n_rows, scr.shape[1]), BF16)

    h = h_ref[...]
    for w, o_ref in zip(w_scr, (oa_ref, ob_ref, oc_ref, od_ref)):
        o_ref[...] = lax.dot_general(h, w[...], (((1,), (1,)), ((), ())), preferred_element_type=F32)


def _x_to_b_block(i, tm, lat_rows, seq, ctx_len):
    per_b = (seq + ctx_len) // tm
    lat_b = seq // tm
    ctx_b = ctx_len // tm
    n_lat = lat_rows // tm
    lat_idx = (i // lat_b) * per_b + ctx_b + i % lat_b
    ic = i - n_lat
    ctx_idx = (ic // ctx_b) * per_b + ic % ctx_b
    return jnp.where(i < n_lat, lat_idx, ctx_idx)


def _inproj(h, w_in, *, layer, lat_rows, seq, ctx_len, n_batch):
    rows, d = h.shape
    tm = 256
    stage_rows = 512
    splits = (0, SSD_COLS, SSD_COLS + LRU_COLS, SSD_COLS + LRU_COLS + HGRN_COLS, w_in.shape[1])
    widths = [SSD_PAD, LRU_COLS, HGRN_COLS, RET_COLS]

    def out_idx(i):
        return (_x_to_b_block(i, tm, lat_rows, seq, ctx_len), 0)

    return pl.pallas_call(
        functools.partial(_inproj_kernel, layer=layer, splits=splits),
        grid=(rows // tm,),
        in_specs=[pl.BlockSpec((tm, d), lambda i: (i, 0)), pl.BlockSpec(memory_space=pl.ANY)],
        out_specs=[pl.BlockSpec((tm, n), out_idx) for n in widths],
        out_shape=[jax.ShapeDtypeStruct((rows, n), F32) for n in widths],
        scratch_shapes=[pltpu.VMEM((n, d), BF16) for n in widths]
        + [pltpu.VMEM((2, stage_rows, d), F32), pltpu.SemaphoreType.DMA((2,))],
        compiler_params=_cparams(1),
        name="inproj",
    )(h, w_in)


def _outproj_kernel(x_ref, ya_ref, yb_ref, yc_ref, yd_ref, w_ref, mod_ref, gpost_ref, gnext_ref, o_ref, hn_ref):
    w = LRU_WIDTH
    n = EPI_ROWS
    for r in range(x_ref.shape[0] // n):
        rows = slice(r * n, (r + 1) * n)
        ycat = jnp.concatenate([ya_ref[rows, :], yb_ref[rows, :], yc_ref[rows, :], yd_ref[rows, :]], axis=1)
        y = jnp.dot(ycat, w_ref[...], preferred_element_type=F32)
        _residual_epilogue(rows, x_ref[rows, :], y, mod_ref[5:6, :], gpost_ref[...], mod_ref, gnext_ref, 6,
                           o_ref, hn_ref)


def _outproj(x, ys, w, mods, gpost, gpre, *, layer, n_rows, lat_rows, seq, ctx_len, n_batch):
    d = x.shape[1]
    tm = 256
    wy = ys[0].shape[1]
    blk = pl.BlockSpec((tm, d), lambda i: (i, 0))

    def y_idx(i):
        return (_x_to_b_block(i, tm, lat_rows, seq, ctx_len), 0)

    return pl.pallas_call(
        _outproj_kernel,
        grid=(n_rows // tm,),
        in_specs=[blk] + [pl.BlockSpec((tm, wy), y_idx)] * 4
        + [_layer_spec((4 * wy, d), layer),
           pl.BlockSpec((None, None, N_MOD, d), lambda i: _mod_row_idx(i, tm, layer, lat_rows, seq, n_batch)),
           pl.BlockSpec((None, 1, d), lambda i: (layer * 3 + 1, 0, 0)),
           pl.BlockSpec((None, 1, d), lambda i: (layer * 3 + 2, 0, 0))],
        out_specs=[blk, blk],
        out_shape=[jax.ShapeDtypeStruct((n_rows, d), F32), jax.ShapeDtypeStruct((n_rows, d), BF16)],
        compiler_params=_cparams(1),
        name="outproj",
    )(x, *ys, w, mods, gpost, gpre)


def _conv_chunk(u_ref, col0, ncols, base, first, last, t_tot, w_ref, b_ref):
    cols = slice(col0, col0 + ncols)
    prev = u_ref[pl.ds(pl.multiple_of(jnp.maximum(base - 8, 0), 8), 8), cols]
    nxt = u_ref[pl.ds(pl.multiple_of(jnp.minimum(base + CHUNK, t_tot - 8), 8), 8), cols]
    cur = u_ref[pl.ds(base, CHUNK), cols]
    prev = jnp.where(first, 0.0, prev)
    nxt = jnp.where(last, 0.0, nxt)
    win = jnp.concatenate([prev, cur, nxt], axis=0)
    n = CHUNK + 16
    xm1 = pltpu.roll(win, 1, axis=0)[8:8 + CHUNK]
    xp1 = pltpu.roll(win, n - 1, axis=0)[8:8 + CHUNK]
    xp2 = pltpu.roll(win, n - 2, axis=0)[8:8 + CHUNK]
    return (w_ref[0:1, :] * xm1 + w_ref[1:2, :] * cur + w_ref[2:3, :] * xp1 + w_ref[3:4, :] * xp2
            + b_ref[...])


def _chunk_order(i, nc_c, nc, reverse):
    if not reverse:
        return i
    return jnp.where(i < nc_c, nc_c - 1 - i, nc + nc_c - 1 - i)


def _stream_edges(c, nc_c, nc):
    first = jnp.logical_or(c == 0, c == nc_c)
    last = jnp.logical_or(c == nc_c - 1, c == nc - 1)
    return first, last


def _ssd_kernel(u_ref, cw_ref, cb_ref, dtb_ref, alog_ref, dskip_ref, nw_ref, o_ref,
                act_scr, dl_scr, y_scr, s_scr, *, nc_c, nc):
    t_tot = nc * CHUNK
    xbc0 = SSD_WIDTH
    dt0 = SSD_WIDTH + SSD_XBC

    def prep(c, _):
        base = pl.multiple_of(c * CHUNK, CHUNK)
        first, last = _stream_edges(c, nc_c, nc)
        conv = _conv_chunk(u_ref, xbc0, SSD_XBC, base, first, last, t_tot, cw_ref, cb_ref)
        act_scr[pl.ds(base, CHUNK), :] = _silu(conv)
        dt = u_ref[pl.ds(base, CHUNK), dt0:dt0 + LANES]
        dl_scr[pl.ds(base, CHUNK), :] = _softplus(dt + dtb_ref[...])
        return 0

    lax.fori_loop(0, nc, prep, 0)

    a_neg = -jnp.exp(alog_ref[...])
    row = _iota((CHUNK, CHUNK), 0)
    col = _iota((CHUNK, CHUNK), 1)
    lane_lo = _iota((CHUNK, LANES), 1) < SSD_HEADDIM
    grp_mask = [(_iota((CHUNK, LANES), 1) // SSD_STATE) == g for g in range(SSD_GROUPS)]

    s_scr[...] = jnp.zeros_like(s_scr)
    bodies = []
    for d in range(2):
        tri = (row >= col) if d == 0 else (row <= col)
        tri_b = tri.astype(BF16)
        edge = CHUNK - 1 if d == 0 else 0

        def body(i, d=d, tri=tri, tri_b=tri_b, edge=edge):
            c = _chunk_order(i, nc_c, nc, d == 1)
            base = pl.multiple_of(c * CHUNK, CHUNK)
            act = act_scr[pl.ds(base, CHUNK), :]
            bm = act[:, SSD_WIDTH:SSD_WIDTH + LANES]
            cm = act[:, SSD_WIDTH + LANES:SSD_WIDTH + 2 * LANES]
            delta = dl_scr[pl.ds(base, CHUNK), :]
            la = delta * a_neg
            cum_col = _dot_sel(tri_b, _split3(la))
            cum_row = cum_col.T
            delta_row = delta.T
            bt = bm.T
            gmat = [_bdot(jnp.where(grp_mask[g], cm, 0.0), bt) for g in range(SSD_GROUPS)]
            for k in range(SSD_HEADS // 2):
                g = (2 * k) // (SSD_HEADS // SSD_GROUPS)
                x_tile = act[:, k * LANES:(k + 1) * LANES]
                cg = jnp.where(grp_mask[g], cm, 0.0)
                m_list, w_list, e_list, tot_list = [], [], [], []
                for h in (2 * k, 2 * k + 1):
                    j = d * SSD_HEADS + h
                    ccol = cum_col[:, j:j + 1]
                    crow = cum_row[j:j + 1, :]
                    drow = delta_row[j:j + 1, :]
                    dec = jnp.where(tri, jnp.exp(jnp.where(tri, ccol - crow, 0.0)), 0.0)
                    m_list.append(gmat[g] * dec * drow)
                    tot = cum_row[j:j + 1, edge:edge + 1]
                    w_list.append(bt * (drow * jnp.exp(tot - crow)))
                    e_list.append(jnp.exp(ccol))
                    tot_list.append(jnp.exp(tot))
                y2 = _bdot(jnp.concatenate(m_list, axis=0), x_tile)
                u2 = _bdot(jnp.concatenate(w_list, axis=0), x_tile)
                s_old = s_scr[d, :, k * LANES:(k + 1) * LANES]
                y_inter = _bdot(cg, s_old) * jnp.where(lane_lo, e_list[0], e_list[1])
                y_tile = jnp.where(lane_lo, y2[:CHUNK], y2[CHUNK:]) + y_inter
                s_scr[d, :, k * LANES:(k + 1) * LANES] = (
                    s_old * jnp.where(lane_lo, tot_list[0], tot_list[1])
                    + jnp.where(lane_lo, u2[:CHUNK], u2[CHUNK:]))
                if d == 0:
                    y_tile = y_tile + x_tile * dskip_ref[:, k * LANES:(k + 1) * LANES]
                y_scr[d, pl.ds(base, CHUNK), k * LANES:(k + 1) * LANES] = y_tile

        bodies.append(body)

    def both(i, _):
        for step in range(2):
            bodies[0](2 * i + step)
            bodies[1](2 * i + step)
        return 0

    assert nc % 2 == 0
    lax.fori_loop(0, nc // 2, both, 0)

    def fin(c, _):
        base = pl.multiple_of(c * CHUNK, CHUNK)
        z = u_ref[pl.ds(base, CHUNK), 0:SSD_WIDTH]
        y = (y_scr[0, pl.ds(base, CHUNK), :] + y_scr[1, pl.ds(base, CHUNK), :]) * _silu(z)
        yn = y * lax.rsqrt(jnp.mean(y * y, axis=-1, keepdims=True) + EPS) * nw_ref[...]
        o_ref[pl.ds(base, CHUNK), :] = yn.astype(o_ref.dtype)
        return 0

    lax.fori_loop(0, nc, fin, 0)


def _ssd(u, conv_w, conv_b, dt_bias, a_log, d_skip, norm_w, *, layer, n_batch, t_ctx, t_lat):
    t_tot = t_ctx + t_lat
    nc_c, nc = t_ctx // CHUNK, t_tot // CHUNK
    return pl.pallas_call(
        functools.partial(_ssd_kernel, nc_c=nc_c, nc=nc),
        grid=(n_batch,),
        in_specs=[pl.BlockSpec((t_tot, SSD_PAD), lambda b: (b, 0)),
                  _layer_spec((CONV_W, SSD_XBC), layer), _layer_spec((1, SSD_XBC), layer),
                  _layer_spec((1, LANES), layer), _layer_spec((1, LANES), layer),
                  _layer_spec((1, SSD_WIDTH), layer), _layer_spec((1, SSD_WIDTH), layer)],
        out_specs=pl.BlockSpec((t_tot, SSD_WIDTH), lambda b: (b, 0)),
        out_shape=jax.ShapeDtypeStruct((n_batch * t_tot, SSD_WIDTH), BF16),
        scratch_shapes=[pltpu.VMEM((t_tot, SSD_XBC), F32), pltpu.VMEM((t_tot, LANES), F32),
                        pltpu.VMEM((2, t_tot, SSD_WIDTH), F32), pltpu.VMEM((2, CHUNK, SSD_WIDTH), F32)],
        compiler_params=_cparams(1),
        name="ssd",
    )(u, conv_w, conv_b, dt_bias, a_log, d_skip, norm_w)


def _lru_kernel(u_ref, cw_ref, cb_ref, wg_ref, bg_ref, lam_ref, o_ref, xc_scr, h_scr, *, nc_c, nc):
    t_tot = nc * CHUNK
    w = LRU_WIDTH

    def prep(c, _):
        base = pl.multiple_of(c * CHUNK, CHUNK)
        first, last = _stream_edges(c, nc_c, nc)
        xc_scr[pl.ds(base, CHUNK), :] = _conv_chunk(u_ref, 0, w, base, first, last, t_tot, cw_ref, cb_ref)
        return 0

    lax.fori_loop(0, nc, prep, 0)

    n_tiles = CHUNK // 8
    sub = _iota((n_tiles, 8, w), 1)

    bodies = []
    for d in range(2):
        coef = (-LRU_C * 1.4426950408889634) * _softplus(-lam_ref[d:d + 1, :])

        def body(i, h_prev, d=d, coef=coef):
            c = _chunk_order(i, nc_c, nc, d == 1)
            base = pl.multiple_of(c * CHUNK, CHUNK)
            xc = xc_scr[pl.ds(base, CHUNK), :]
            gates = _bdot(xc, wg_ref[d]) + bg_ref[d:d + 1, :]
            r = _sigmoid(gates[:, :w])
            ig = _sigmoid(gates[:, w:])
            a = jnp.exp2(r * coef)
            z = jnp.maximum(1.0 - a * a, 1e-12)
            b = (z * lax.rsqrt(z)) * (ig * xc)
            a = a.reshape(n_tiles, 8, w)
            b = b.reshape(n_tiles, 8, w)
            for k in (1, 2, 4):
                if d == 0:
                    keep = sub >= k
                    shift = k
                else:
                    keep = sub < 8 - k
                    shift = 8 - k
                a_s = jnp.where(keep, pltpu.roll(a, shift, axis=1), 1.0)
                b_s = jnp.where(keep, pltpu.roll(b, shift, axis=1), 0.0)
                b = a * b_s + b
                a = a * a_s
            tiles = range(n_tiles) if d == 0 else range(n_tiles - 1, -1, -1)
            outs = [None] * n_tiles
            for t in tiles:
                h_t = a[t] * h_prev + b[t]
                outs[t] = h_t
                h_prev = h_t[7:8] if d == 0 else h_t[0:1]
            h_scr[d, pl.ds(base, CHUNK), :] = jnp.concatenate(outs, axis=0)
            return h_prev

        bodies.append(body)

    def both(i, carry):
        hf, hb = carry
        for step in range(2):
            hf = bodies[0](2 * i + step, hf)
            hb = bodies[1](2 * i + step, hb)
        return hf, hb

    assert nc % 2 == 0
    zero = jnp.zeros((1, w), F32)
    lax.fori_loop(0, nc // 2, both, (zero, zero))

    def fin(c, _):
        base = pl.multiple_of(c * CHUNK, CHUNK)
        gate = u_ref[pl.ds(base, CHUNK), w:2 * w]
        out = (h_scr[0, pl.ds(base, CHUNK), :] + h_scr[1, pl.ds(base, CHUNK), :]) * jax.nn.gelu(gate)
        o_ref[pl.ds(base, CHUNK), :] = out.astype(o_ref.dtype)
        return 0

    lax.fori_loop(0, nc, fin, 0)


def _block_diag(wb):
    nb, bi, bj = wb.shape[-3:]
    eye = jnp.eye(nb, dtype=wb.dtype)
    return jnp.einsum('...hij,hg->...higj', wb, eye).reshape(wb.shape[:-3] + (nb * bi, nb * bj))


def _lru(u, conv_w, conv_b, wg, bg, lam, *, layer, n_batch, t_ctx, t_lat):
    t_tot = t_ctx + t_lat
    nc_c, nc = t_ctx // CHUNK, t_tot // CHUNK
    w = LRU_WIDTH
    return pl.pallas_call(
        functools.partial(_lru_kernel, nc_c=nc_c, nc=nc),
        grid=(n_batch,),
        in_specs=[pl.BlockSpec((t_tot, LRU_COLS), lambda b: (b, 0)),
                  _layer_spec((CONV_W, w), layer), _layer_spec((1, w), layer),
                  _layer_spec((2, w, 2 * w), layer), _layer_spec((2, 2 * w), layer), _layer_spec((2, w), layer)],
        out_specs=pl.BlockSpec((t_tot, w), lambda b: (b, 0)),
        out_shape=jax.ShapeDtypeStruct((n_batch * t_tot, w), BF16),
        scratch_shapes=[pltpu.VMEM((t_tot, w), F32), pltpu.VMEM((2, t_tot, w), F32)],
        compiler_params=_cparams(1),
        name="rglru",
    )(u, conv_w, conv_b, wg, bg, lam)


def _hgrn_kernel(q_ref, ff_ref, fb_ref, i_ref, g_ref, lbl_ref, nw_ref, esel_ref, o_ref, of_scr, ob_scr, st_scr, *,
                 layer, nc_c, nc, n_heads):
    n_sub = CHUNK // SUB
    row = _iota((CHUNK, CHUNK), 0)
    col = _iota((CHUNK, CHUNK), 1)
    scale = HGRN_HEADDIM ** -0.5
    log2e = 1.4426950408889634

    lbs = []
    for d in range(2):
        logits = lbl_ref[d]
        ex = jnp.exp(logits - jnp.max(logits, axis=0, keepdims=True))
        sm = ex / jnp.sum(ex, axis=0, keepdims=True)
        lbs.append(jnp.sum(sm[0:layer + 1], axis=0, keepdims=True) - sm[0:1])
    causal = [row >= col, row <= col]
    tri_full = [m.astype(BF16) for m in causal]
    valids = [jnp.logical_and(row // SUB == col // SUB, m) for m in causal]
    levels = tuple(SUB << n for n in range(1, (CHUNK // SUB).bit_length()))
    tpb = SUB // 8
    q_side = [[((row % g) >= g // 2) if d == 0 else ((row % g) < g // 2) for g in levels] for d in range(2)]
    pair_mask = [[jnp.logical_and(row // g == col // g,
                                  jnp.logical_and(q_side[d][n], ((col % g) < g // 2) if d == 0
                                                  else ((col % g) >= g // 2)))
                  for n, g in enumerate(levels)] for d in range(2)]
    st_scr[...] = jnp.zeros_like(st_scr)

    def chunk(d, hh, base):
        cols = slice(hh * LANES, (hh + 1) * LANES)
        lb = lbs[d][:, cols]
        f_ref = ff_ref if d == 0 else fb_ref
        qh = _silu(q_ref[pl.ds(base, CHUNK), cols]) * scale
        f = lb + (1.0 - lb) * _sigmoid(f_ref[pl.ds(base, CHUNK), cols])
        kk = 1.0 - f
        gl = jnp.log(f) * log2e
        v = i_ref[pl.ds(base, CHUNK), cols]
        cum = _dot_sel(tri_full[d], _split3(gl))
        halves = []
        nb = n_sub // 2
        for r0 in range(0, CHUNK, CHUNK // 2):
            rows = slice(r0, r0 + CHUNK // 2)
            q4 = qh[rows].reshape(nb, tpb, 8, LANES)
            c4 = cum[rows].reshape(nb, tpb, 8, LANES)
            k4 = kk[rows].reshape(nb, tpb, 8, LANES)
            zero = jnp.zeros((nb, 8, LANES), F32)
            acc = None
            for s0 in range(0, SUB, 8):
                ps = []
                for s in range(s0, s0 + 8):
                    ts, rs = s // 8, s % 8
                    cs = c4[:, ts, rs:rs + 1, :]
                    ks = k4[:, ts, rs:rs + 1, :]
                    tiles = []
                    for tt in range(tpb):
                        if tt == ts:
                            e = jnp.exp2(jnp.minimum(c4[:, tt] - cs, 0.0))
                        elif (tt > ts) == (d == 0):
                            e = jnp.exp2(c4[:, tt] - cs)
                        else:
                            tiles.append(zero)
                            continue
                        tiles.append((q4[:, tt] * e) * ks)
                    ps.append(jnp.stack(tiles, axis=1).reshape(CHUNK // 2, LANES).astype(BF16))
                part = jnp.dot(jnp.concatenate(ps, axis=1), esel_ref[s0 * LANES:(s0 + 8) * LANES, :],
                               preferred_element_type=F32)
                acc = part if acc is None else acc + part
            halves.append(acc)
        scores = jnp.where(valids[d], jnp.concatenate(halves, axis=0), 0.0)
        for n, g in enumerate(levels):
            pieces = []
            for g0 in range(0, CHUNK, g):
                b = g0 + g // 2 - 1 if d == 0 else g0 + g // 2
                pieces.append(jnp.exp2(-jnp.abs(cum[g0:g0 + g] - cum[b:b + 1])))
            w = pieces[0] if len(pieces) == 1 else jnp.concatenate(pieces, axis=0)
            xg = (jnp.where(q_side[d][n], qh, kk) * w).astype(BF16)
            scores = scores + jnp.where(pair_mask[d][n], _bdot_nt(xg, xg), 0.0)
        y = _bdot(scores, v)
        edge = CHUNK - 1 if d == 0 else 0
        tot = cum[edge:edge + 1]
        st = st_scr[d, hh]
        y = y + _bdot_nt(qh * jnp.exp2(cum), st)
        st_scr[d, hh] = st * jnp.exp2(tot) + _bdot(v.T, kk * jnp.exp2(tot - cum))
        return y

    per_trip = 2

    def body(i, _):
        for step in range(per_trip):
            for d in range(2):
                c = _chunk_order(per_trip * i + step, nc_c, nc, d == 1)
                base = pl.multiple_of(c * CHUNK, CHUNK)
                o_scr = of_scr if d == 0 else ob_scr
                for hh in range(n_heads):
                    o_scr[pl.ds(base, CHUNK), hh * LANES:(hh + 1) * LANES] = chunk(d, hh, base)
        return 0

    assert nc % per_trip == 0
    lax.fori_loop(0, nc // per_trip, body, 0)

    def fin(c, _):
        base = pl.multiple_of(c * CHUNK, CHUNK)
        for hh in range(n_heads):
            cols = slice(hh * LANES, (hh + 1) * LANES)
            o = of_scr[pl.ds(base, CHUNK), cols] + ob_scr[pl.ds(base, CHUNK), cols]
            on = o * lax.rsqrt(jnp.mean(o * o, axis=-1, keepdims=True) + EPS) * nw_ref[...]
            out = on * _silu(g_ref[pl.ds(base, CHUNK), cols])
            o_ref[pl.ds(base, CHUNK), cols] = out.astype(o_ref.dtype)
        return 0

    lax.fori_loop(0, nc, fin, 0)


def _hgrn_selector():
    s_of_row = np.arange(SUB * LANES) // LANES
    return jnp.asarray(s_of_row[:, None] == (np.arange(LANES)[None, :] % SUB), dtype=BF16)


def _hgrn(u, lb_logits, norm_w, *, layer, n_batch, t_ctx, t_lat):
    t_tot = t_ctx + t_lat
    nc_c, nc = t_ctx // CHUNK, t_tot // CHUNK
    n_heads = 2
    ng = HGRN_HEADS // n_heads
    wblk = n_heads * LANES
    depth = lb_logits.shape[1]
    col = lambda part: pl.BlockSpec((t_tot, wblk), lambda b, h, part=part: (b, part * ng + h))
    return pl.pallas_call(
        functools.partial(_hgrn_kernel, layer=layer, nc_c=nc_c, nc=nc, n_heads=n_heads),
        grid=(n_batch, ng),
        in_specs=[col(0), col(1), col(2), col(3), col(4),
                  pl.BlockSpec((2, depth, wblk), lambda b, h: (0, 0, h)),
                  _layer_spec((1, LANES), layer),
                  _const_spec((SUB * LANES, LANES))],
        out_specs=pl.BlockSpec((t_tot, wblk), lambda b, h: (b, h)),
        out_shape=jax.ShapeDtypeStruct((n_batch * t_tot, HGRN_WIDTH), BF16),
        scratch_shapes=[pltpu.VMEM((t_tot, wblk), F32), pltpu.VMEM((t_tot, wblk), F32),
                        pltpu.VMEM((2, n_heads, HGRN_HEADDIM, HGRN_HEADDIM), F32)],
        compiler_params=_cparams(2),
        name="hgrn2",
    )(u, u, u, u, u, lb_logits, norm_w, _hgrn_selector())


def _ret_consts(t_ctx, t_lat):
    gam = 1.0 - np.exp2(-5.0 - np.arange(RET_HEADS, dtype=np.float64))
    t = np.arange(CHUNK, dtype=np.float64)
    diff = np.abs(t[:, None] - t[None, :])
    dsym = gam[:, None, None] ** diff[None]
    dsym[:, np.arange(CHUNK), np.arange(CHUNK)] = 2.0
    head_of_lane = np.repeat(np.arange(RET_HEADS), RET_QK_DIM)
    qf = gam[head_of_lane][None, :] ** (t[:, None] + 1.0)
    qb = gam[head_of_lane][None, :] ** (CHUNK - t[:, None])
    kf = gam[:, None] ** (CHUNK - 1.0 - t[None, :])
    kb = gam[:, None] ** t[None, :]
    gtot = gam ** CHUNK
    pos = np.arange(t_lat)
    rows_p, cols_p = pos // GRID_W, pos % GRID_W
    quarter = RET_QK_DIM // 4
    inv = ROPE_BASE ** (-np.arange(quarter, dtype=np.float64) / quarter)
    ang_r = rows_p[:, None] * inv[None, :]
    ang_c = cols_p[:, None] * inv[None, :]
    cos_h = np.concatenate([np.cos(ang_r), np.cos(ang_r), np.cos(ang_c), np.cos(ang_c)], axis=1)
    sin_h = np.concatenate([-np.sin(ang_r), np.sin(ang_r), -np.sin(ang_c), np.sin(ang_c)], axis=1)
    cos_t = np.concatenate([np.ones((t_ctx, RET_QK_DIM)), cos_h], axis=0)
    sin_t = np.concatenate([np.zeros((t_ctx, RET_QK_DIM)), sin_h], axis=0)
    cos_t = np.tile(cos_t, (1, RET_HEADS))
    sin_t = np.tile(sin_t, (1, RET_HEADS))
    f32 = lambda a: jnp.asarray(a, dtype=F32)
    return (f32(dsym), f32(qf), f32(qb), f32(np.concatenate([kf, kb], axis=0)), [float(g) for g in gtot],
            f32(cos_t), f32(sin_t))


def _ret_kernel(u_ref, dsym_ref, qf_ref, qb_ref, kfac_ref, cos_ref, sin_ref, o_ref, o_scr, s_scr, q_scr, kt_scr, *,
                gtot, nc_c, nc):
    nqk = RET_HEADS * RET_QK_DIM
    lane = _iota((CHUNK, nqk), 1)
    low_half = (lane % (RET_QK_DIM // 2)) < (RET_QK_DIM // 4)
    tile_lane = _iota((CHUNK, LANES), 1)
    head_mask = [(tile_lane // RET_QK_DIM) == p for p in range(2)]
    kscale = RET_QK_DIM ** -0.5

    def rope(x, cos, sin):
        swapped = jnp.where(low_half, pltpu.roll(x, nqk - RET_QK_DIM // 4, axis=1),
                            pltpu.roll(x, RET_QK_DIM // 4, axis=1))
        return x * cos + swapped * sin

    def v_of(base, h):
        return u_ref[pl.ds(base, CHUNK), 2 * nqk + h * RET_V_DIM:2 * nqk + (h + 1) * RET_V_DIM]

    def prep(c):
        base = pl.multiple_of(c * CHUNK, CHUNK)
        cos = cos_ref[pl.ds(base, CHUNK), :]
        sin = sin_ref[pl.ds(base, CHUNK), :]
        q = rope(u_ref[pl.ds(base, CHUNK), 0:nqk], cos, sin)
        k = rope(u_ref[pl.ds(base, CHUNK), nqk:2 * nqk], cos, sin) * kscale
        kt = k.T
        q_scr[pl.ds(base, CHUNK), :] = q
        kt_scr[c] = kt
        for h in range(RET_HEADS):
            p = h // 2
            raw = _bdot(jnp.where(head_mask[h % 2], q[:, p * LANES:(p + 1) * LANES], 0.0),
                        kt[p * LANES:(p + 1) * LANES])
            o_scr[2, pl.ds(base, CHUNK), h * RET_V_DIM:(h + 1) * RET_V_DIM] = _bdot(raw * dsym_ref[h], v_of(base, h))

    def prep_pair(i, _):
        prep(2 * i)
        prep(2 * i + 1)
        return 0

    assert nc % 2 == 0
    lax.fori_loop(0, nc // 2, prep_pair, 0)

    s_scr[...] = jnp.zeros_like(s_scr)
    bodies = []
    for d in range(2):
        qfac = qf_ref if d == 0 else qb_ref

        def body(i, d=d, qfac=qfac):
            c = _chunk_order(i, nc_c, nc, d == 1)
            base = pl.multiple_of(c * CHUNK, CHUNK)
            qd = q_scr[pl.ds(base, CHUNK), :] * qfac[...]
            kt = kt_scr[c]
            for h in range(RET_HEADS):
                p = h // 2
                kt_p = kt[p * LANES:(p + 1) * LANES]
                s_old = s_scr[d, h]
                y = _bdot(jnp.where(head_mask[h % 2], qd[:, p * LANES:(p + 1) * LANES], 0.0), s_old)
                kfac = kfac_ref[d * RET_HEADS + h:d * RET_HEADS + h + 1, :]
                s_scr[d, h] = gtot[h] * s_old + _bdot(kt_p * kfac, v_of(base, h))
                o_scr[d, pl.ds(base, CHUNK), h * RET_V_DIM:(h + 1) * RET_V_DIM] = y

        bodies.append(body)

    def both(i, _):
        for step in range(2):
            bodies[0](2 * i + step)
            bodies[1](2 * i + step)
        return 0

    assert nc % 2 == 0
    lax.fori_loop(0, nc // 2, both, 0)

    def fin(c, _):
        base = pl.multiple_of(c * CHUNK, CHUNK)
        for h in range(RET_HEADS):
            cols = slice(h * RET_V_DIM, (h + 1) * RET_V_DIM)
            o = (o_scr[0, pl.ds(base, CHUNK), cols] + o_scr[1, pl.ds(base, CHUNK), cols]
                 + o_scr[2, pl.ds(base, CHUNK), cols])
            on = o * lax.rsqrt(jnp.mean(o * o, axis=-1, keepdims=True) + EPS)
            gate = u_ref[pl.ds(base, CHUNK), 2 * nqk + RET_WIDTH + h * RET_V_DIM:
                         2 * nqk + RET_WIDTH + (h + 1) * RET_V_DIM]
            o_ref[pl.ds(base, CHUNK), cols] = (_silu(gate) * on).astype(o_ref.dtype)
        return 0

    lax.fori_loop(0, nc, fin, 0)


def _ret(u, *, n_batch, t_ctx, t_lat):
    t_tot = t_ctx + t_lat
    nc_c, nc = t_ctx // CHUNK, t_tot // CHUNK
    dsym, qf, qb, kfac, gtot, cos_t, sin_t = _ret_consts(t_ctx, t_lat)
    nqk = RET_HEADS * RET_QK_DIM
    return pl.pallas_call(
        functools.partial(_ret_kernel, gtot=gtot, nc_c=nc_c, nc=nc),
        grid=(n_batch,),
        in_specs=[pl.BlockSpec((t_tot, RET_COLS), lambda b: (b, 0)),
                  _const_spec((RET_HEADS, CHUNK, CHUNK)), _const_spec((CHUNK, nqk)), _const_spec((CHUNK, nqk)),
                  _const_spec((2 * RET_HEADS, CHUNK)), _const_spec((t_tot, nqk)), _const_spec((t_tot, nqk))],
        out_specs=pl.BlockSpec((t_tot, RET_WIDTH), lambda b: (b, 0)),
        out_shape=jax.ShapeDtypeStruct((n_batch * t_tot, RET_WIDTH), BF16),
        scratch_shapes=[pltpu.VMEM((3, t_tot, RET_WIDTH), F32),
                        pltpu.VMEM((2, RET_HEADS, LANES, RET_V_DIM), F32),
                        pltpu.VMEM((t_tot, nqk), F32), pltpu.VMEM((nc, nqk, CHUNK), F32)],
        compiler_params=_cparams(1),
        name="retention",
    )(u, dsym, qf, qb, kfac, cos_t, sin_t)


def kernel(x, c, ctx, c_ctx, w_mod, b_mod, norm_pre, norm_post, ffn_w1, ffn_w3, ffn_w2, w_in, w_out,
           ssd_conv_w, ssd_conv_b, ssd_dt_bias, ssd_a_log, ssd_d, ssd_norm_w,
           lru_conv_w, lru_conv_b, lru_wa, lru_ba, lru_wx, lru_bx, lru_lambda,
           hgrn_lb_logits, hgrn_norm_w):
    n_batch, seq, d = x.shape
    ctx_len = ctx.shape[1]
    depth = w_mod.shape[0]
    lat_rows = n_batch * seq
    all_rows = lat_rows + n_batch * ctx_len
    geom = dict(lat_rows=lat_rows, seq=seq, ctx_len=ctx_len, n_batch=n_batch)
    mix = dict(n_batch=n_batch, t_ctx=ctx_len, t_lat=seq)

    c_all = jnp.concatenate([c, c_ctx[None, :], jnp.zeros((8 - n_batch - 1, d), F32)], axis=0)
    mods = _modulation(c_all, w_mod, b_mod).reshape(depth, 8, N_MOD, d)
    w2 = ffn_w2
    w_in_parts = jnp.swapaxes(w_in, 1, 2)
    w_out_b = w_out.astype(BF16)
    gpre = norm_pre.reshape(depth * 3, 1, d)
    gpost = norm_post.reshape(depth * 3, 1, d)
    rowvec = lambda v: v.reshape(depth, 1, -1)
    lane_pad = lambda v: jnp.pad(v.reshape(depth, 1, -1), ((0, 0), (0, 0), (0, LANES - v[0].size)))
    ssd_params = (ssd_conv_w, rowvec(ssd_conv_b), lane_pad(ssd_dt_bias), lane_pad(ssd_a_log),
                  rowvec(jnp.repeat(ssd_d, SSD_HEADDIM, axis=-1)), rowvec(ssd_norm_w))
    lru_wg = jnp.concatenate([_block_diag(lru_wa), _block_diag(lru_wx)], axis=-1).astype(BF16)
    lru_bg = jnp.concatenate([lru_ba.reshape(depth, 2, -1), lru_bx.reshape(depth, 2, -1)], axis=-1)
    lru_params = (lru_conv_w, rowvec(lru_conv_b), lru_wg, lru_bg, lru_lambda)
    hgrn_nw = rowvec(hgrn_norm_w)

    rowgeom = dict(lat_rows=lat_rows, seq=seq, n_batch=n_batch)
    xs = (x.reshape(lat_rows, d), ctx.reshape(n_batch * ctx_len, d))
    h = _prenorm(*xs, mods, gpre, **rowgeom)
    for l in range(depth):
        last = l == depth - 1
        g = _ffn_up(h, ffn_w1, ffn_w3, layer=l, k=0, n_rows=all_rows)
        xs, h = _ffn_down(g, xs, mods, gpost, gpre, w2, layer=l, k=0, j=0, nxt=3, nxt_layer=l,
                          n_rows=all_rows, **rowgeom)
        ua, ub, uc, ud = _inproj(h, w_in_parts, layer=l, **geom)
        ya = _ssd(ua, *ssd_params, layer=l, **mix)
        yb = _lru(ub, *lru_params, layer=l, **mix)
        yc = _hgrn(uc, hgrn_lb_logits, hgrn_nw, layer=l, **mix)
        yd = _ret(ud, **mix)
        n_rows = lat_rows if last else all_rows
        xs, h = _outproj(xs, (ya, yb, yc, yd), w_out_b, mods, gpost, gpre, layer=l, n_rows=n_rows, **geom)
        g = _ffn_up(h, ffn_w1, ffn_w3, layer=l, k=1, n_rows=n_rows)
        if last:
            (xs,) = _ffn_down(g, xs, mods, gpost, gpre, w2, layer=l, k=1, j=6, nxt=None, nxt_layer=l,
                              n_rows=n_rows, **rowgeom)
        else:
            xs, h = _ffn_down(g, xs, mods, gpost, gpre, w2, layer=l, k=1, j=6, nxt=0, nxt_layer=l + 1,
                              n_rows=n_rows, **rowgeom)
    return xs.reshape(n_batch, seq, d)
```

```python
import functools

import numpy as np
import jax
import jax.numpy as jnp
from jax import lax
from jax.experimental import pallas as pl
from jax.experimental.pallas import tpu as pltpu

F32 = jnp.float32
BF16 = jnp.bfloat16

N_MOD = 9
CONV_W = 4
EPS = 1e-6
GRID_W = 64

SSD_WIDTH = 512
SSD_HEADDIM = 64
SSD_HEADS = 8
SSD_GROUPS = 2
SSD_STATE = 64
SSD_XBC = SSD_WIDTH + 2 * SSD_GROUPS * SSD_STATE
SSD_COLS = SSD_WIDTH + SSD_XBC + 2 * SSD_HEADS
SSD_PAD = 1408
LRU_WIDTH = 512
LRU_C = 8.0
LRU_COLS = 2 * LRU_WIDTH
HGRN_WIDTH = 512
HGRN_HEADDIM = 128
HGRN_HEADS = 4
HGRN_COLS = 5 * HGRN_WIDTH
RET_WIDTH = 512
RET_V_DIM = 128
RET_HEADS = 4
RET_QK_DIM = 64
RET_COLS = 2 * RET_HEADS * RET_QK_DIM + 2 * RET_WIDTH
ROPE_BASE = 10000.0

LANES = 128
CHUNK = 128
SUB = 8
VMEM_LIMIT = 56 * 1024 * 1024


def _cparams(n_axes):
    return pltpu.CompilerParams(dimension_semantics=("arbitrary",) * n_axes,
                                vmem_limit_bytes=VMEM_LIMIT)


def _bdot(a, b):
    return jnp.dot(a.astype(BF16), b.astype(BF16), preferred_element_type=F32)


def _bdot_nt(a, b):
    return lax.dot_general(a.astype(BF16), b.astype(BF16), (((1,), (1,)), ((), ())),
                           preferred_element_type=F32)


def _split3(x):
    hi = x.astype(BF16)
    r1 = x - hi.astype(F32)
    mid = r1.astype(BF16)
    lo = (r1 - mid.astype(F32)).astype(BF16)
    return hi, mid, lo


def _dot_sel(sel, parts):
    return (jnp.dot(sel, parts[0], preferred_element_type=F32)
            + jnp.dot(sel, parts[1], preferred_element_type=F32)
            + jnp.dot(sel, parts[2], preferred_element_type=F32))


def _sigmoid(x):
    return 1.0 / (1.0 + jnp.exp(-x))


def _silu(x):
    return x * _sigmoid(x)


def _softplus(x):
    return jnp.maximum(x, 0.0) + jnp.log(1.0 + jnp.exp(-jnp.abs(x)))


def _iota(shape, dim):
    return lax.broadcasted_iota(jnp.int32, shape, dim)


def _const_spec(shape):
    return pl.BlockSpec(shape, lambda *_: (0,) * len(shape))


def _layer_spec(shape, layer):
    return pl.BlockSpec((None,) + shape, lambda *_: (layer,) + (0,) * len(shape))


def _mod_kernel(c_ref, w_ref, b_ref, o_ref):
    o_ref[...] = _bdot(_silu(c_ref[...]), w_ref[...]) + b_ref[...]


def _modulation(c_all, w_mod, b_mod):
    depth, d, n = w_mod.shape
    tn = 2048
    return pl.pallas_call(
        _mod_kernel,
        grid=(depth, n // tn),
        in_specs=[pl.BlockSpec((8, d), lambda l, j: (0, 0)),
                  pl.BlockSpec((None, d, tn), lambda l, j: (l, 0, j)),
                  pl.BlockSpec((None, 1, tn), lambda l, j: (l, 0, j))],
        out_specs=pl.BlockSpec((None, 8, tn), lambda l, j: (l, 0, j)),
        out_shape=jax.ShapeDtypeStruct((depth, 8, n), F32),
        compiler_params=_cparams(2),
        name="modulation",
    )(c_all, w_mod, b_mod.reshape(depth, 1, n))


def _mod_row_idx(i, tm, layer, lat_rows, seq, n_batch):
    return (layer, jnp.where(i * tm < lat_rows, (i * tm) // seq, n_batch), 0, 0)


def _mod_norm(x, g, shift, scale):
    return (x * lax.rsqrt(jnp.mean(x * x, axis=-1, keepdims=True) + EPS) * (g * (1.0 + scale))
            + shift).astype(BF16)


def _prenorm_kernel(xl_ref, xc_ref, mod_ref, g_ref, h_ref, *, n_split):
    x = jnp.where(pl.program_id(0) < n_split, xl_ref[...], xc_ref[...])
    h_ref[...] = _mod_norm(x, g_ref[...], mod_ref[0:1, :], mod_ref[1:2, :])


def _prenorm(xl, xc, mods, gpre, *, lat_rows, seq, n_batch):
    d = xl.shape[1]
    rows = xl.shape[0] + xc.shape[0]
    tm = 512
    n_split = xl.shape[0] // tm
    return pl.pallas_call(
        functools.partial(_prenorm_kernel, n_split=n_split),
        grid=(rows // tm,),
        in_specs=[pl.BlockSpec((tm, d), lambda i: (jnp.minimum(i, n_split - 1), 0)),
                  pl.BlockSpec((tm, d), lambda i: (jnp.maximum(i - n_split, 0), 0)),
                  pl.BlockSpec((None, None, N_MOD, d), lambda i: _mod_row_idx(i, tm, 0, lat_rows, seq, n_batch)),
                  pl.BlockSpec((None, 1, d), lambda i: (0, 0, 0))],
        out_specs=pl.BlockSpec((tm, d), lambda i: (i, 0)),
        out_shape=jax.ShapeDtypeStruct((rows, d), BF16),
        compiler_params=_cparams(1),
        name="prenorm",
    )(xl, xc, mods, gpre)


def _ffn_up_kernel(h_ref, w1_ref, w3_ref, g_ref, w1_scr, w3_scr, *, n_sub):
    @pl.when(pl.program_id(1) == 0)
    def _():
        w1_scr[...] = w1_ref[...].astype(BF16)
        w3_scr[...] = w3_ref[...].astype(BF16)

    rows = h_ref.shape[0] // n_sub
    for r in range(n_sub):
        h = h_ref[r * rows:(r + 1) * rows, :]
        a = jnp.dot(h, w1_scr[...], preferred_element_type=F32)
        b = jnp.dot(h, w3_scr[...], preferred_element_type=F32)
        g_ref[r * rows:(r + 1) * rows, :] = (_silu(a) * b).astype(BF16)


def _ffn_up(h, w1, w3, *, layer, k, n_rows):
    d = h.shape[1]
    dff = w1.shape[-1]
    n_sub = 4
    tm, tf = n_rows // 4, 512
    w_spec = pl.BlockSpec((None, None, d, tf), lambda f, i: (layer, k, 0, f))
    return pl.pallas_call(
        functools.partial(_ffn_up_kernel, n_sub=n_sub),
        grid=(dff // tf, n_rows // tm),
        in_specs=[pl.BlockSpec((tm, d), lambda f, i: (i, 0)), w_spec, w_spec],
        out_specs=pl.BlockSpec((tm, tf), lambda f, i: (i, f)),
        out_shape=jax.ShapeDtypeStruct((n_rows, dff), BF16),
        scratch_shapes=[pltpu.VMEM((d, tf), BF16), pltpu.VMEM((d, tf), BF16)],
        compiler_params=_cparams(2),
        name="ffn_up",
    )(h, w1, w3)


EPI_ROWS = 128


def _residual_epilogue(rows, x, y, gate, gpost, mod_ref, gnext_ref, nxt, o_ref, hn_ref):
    xn = x + y * lax.rsqrt(jnp.mean(y * y, axis=-1, keepdims=True) + EPS) * (gate * gpost)
    o_ref[rows, :] = xn
    if nxt is not None:
        hn_ref[rows, :] = _mod_norm(xn, gnext_ref[...], mod_ref[nxt:nxt + 1, :], mod_ref[nxt + 1:nxt + 2, :])


def _ffn_down_kernel(*refs, layer, k, j, nxt, nxt_mod_own, n_split):
    g_ref = refs[0]
    n_x = 1 if n_split is None else 2
    x_refs = refs[1:1 + n_x]
    mod_ref, gpost_ref = refs[1 + n_x:3 + n_x]
    pos = 3 + n_x
    nmod_ref = mod_ref
    gnext_ref = hn_ref = None
    if nxt is not None:
        if not nxt_mod_own:
            nmod_ref = refs[pos]
            pos += 1
        gnext_ref = refs[pos]
        pos += 1
    w2_hbm, o_ref = refs[pos], refs[pos + 1]
    pos += 2
    if nxt is not None:
        hn_ref = refs[pos]
        pos += 1
    w2_scr, stage_scr, sem = refs[pos], refs[pos + 1], refs[pos + 2]

    @pl.when(pl.program_id(0) == 0)
    def _():
        def consume(r0, chunk):
            w2_scr[pl.ds(r0, chunk.shape[0]), :] = chunk.astype(BF16)

        _stage_rows(w2_hbm.at[layer, k], 0, w2_hbm.shape[2], stage_scr.shape[1], stage_scr, sem, consume)

    n = EPI_ROWS
    for r in range(g_ref.shape[0] // n):
        rows = slice(r * n, (r + 1) * n)
        y = jnp.dot(g_ref[rows, :], w2_scr[...], preferred_element_type=F32)
        if n_split is None:
            x = x_refs[0][rows, :]
        else:
            x = jnp.where(pl.program_id(0) < n_split, x_refs[0][rows, :], x_refs[1][rows, :])
        _residual_epilogue(rows, x, y, 0.5 * mod_ref[j + 2:j + 3, :], gpost_ref[...], nmod_ref, gnext_ref,
                           nxt, o_ref, hn_ref)


def _ffn_down(g, x, mods, gpost, gpre, w2, *, layer, k, j, nxt, nxt_layer, n_rows, lat_rows, seq, n_batch):
    xs = x if isinstance(x, tuple) else (x,)
    d = xs[0].shape[1]
    dff = g.shape[1]
    tm = 256
    norm_row = layer * 3 + (0 if j == 0 else 2)
    blk = pl.BlockSpec((tm, d), lambda i: (i, 0))
    mod_spec = lambda ly: pl.BlockSpec((None, None, N_MOD, d),
                                       lambda i: _mod_row_idx(i, tm, ly, lat_rows, seq, n_batch))
    if len(xs) == 1:
        n_split = None
        x_specs = [blk]
    else:
        n_split = xs[0].shape[0] // tm
        x_specs = [pl.BlockSpec((tm, d), lambda i: (jnp.minimum(i, n_split - 1), 0)),
                   pl.BlockSpec((tm, d), lambda i: (jnp.maximum(i - n_split, 0), 0))]
    in_specs = [pl.BlockSpec((tm, dff), lambda i: (i, 0))] + x_specs + [
        mod_spec(layer), pl.BlockSpec((None, 1, d), lambda i: (norm_row, 0, 0))]
    args = [g, *xs, mods, gpost]
    out_specs, out_shape = [blk], [jax.ShapeDtypeStruct((n_rows, d), F32)]
    nxt_mod_own = nxt_layer == layer
    if nxt is not None:
        if not nxt_mod_own:
            in_specs.append(mod_spec(nxt_layer))
            args.append(mods)
        in_specs.append(pl.BlockSpec((None, 1, d), lambda i: (nxt_layer * 3 + nxt // 3, 0, 0)))
        args.append(gpre)
        out_specs.append(blk)
        out_shape.append(jax.ShapeDtypeStruct((n_rows, d), BF16))
    in_specs.append(pl.BlockSpec(memory_space=pl.ANY))
    args.append(w2)
    return pl.pallas_call(
        functools.partial(_ffn_down_kernel, layer=layer, k=k, j=j, nxt=nxt, nxt_mod_own=nxt_mod_own,
                          n_split=n_split),
        grid=(n_rows // tm,),
        in_specs=in_specs,
        out_specs=out_specs,
        out_shape=out_shape,
        scratch_shapes=[pltpu.VMEM((dff, d), BF16), pltpu.VMEM((2, 512, d), F32), pltpu.SemaphoreType.DMA((2,))],
        compiler_params=_cparams(1),
        name="ffn_down",
    )(*args)


def _stage_rows(src_hbm, row0, n_rows, rpc, stage_scr, sem, consume):
    n_chunks = n_rows // rpc
    assert n_chunks * rpc == n_rows and rpc <= stage_scr.shape[1]

    def copy(c, slot):
        return pltpu.make_async_copy(src_hbm.at[pl.ds(row0 + c * rpc, rpc)],
                                     stage_scr.at[slot, pl.ds(0, rpc)], sem.at[slot])

    copy(0, 0).start()

    def step(c, _):
        slot = c % 2

        @pl.when(c + 1 < n_chunks)
        def _():
            copy(c + 1, 1 - slot).start()

        copy(c, slot).wait()
        consume(pl.multiple_of(c * rpc, 16), stage_scr[slot, pl.ds(0, rpc), :])
        return 0

    lax.fori_loop(0, n_chunks, step, 0)


def _inproj_kernel(h_ref, w_hbm, oa_ref, ob_ref, oc_ref, od_ref, wa_scr, wb_scr, wc_scr, wd_scr, stage_scr, sem, *,
                   layer, splits):
    w_scr = (wa_scr, wb_scr, wc_scr, wd_scr)

    @pl.when(pl.program_id(0) == 0)
    def _():
        for m, scr in enumerate(w_scr):
            n_rows = splits[m + 1] - splits[m]
            rpc = max(r for r in range(16, stage_scr.shape[1] + 1, 16) if n_rows % r == 0)

            def consume(r, chunk, scr=scr):
                scr[pl.ds(r, chunk.shape[0]), :] = chunk.astype(BF16)

            _stage_rows(w_hbm.at[layer], splits[m], n_rows, rpc, stage_scr, sem, consume)
            if scr.shape[0] > n_rows:
                scr[n_rows:, :] = jnp.zeros((scr.shape[0] - n_rows, scr.shape[1]), BF16)

    h = h_ref[...]
    for w, o_ref in zip(w_scr, (oa_ref, ob_ref, oc_ref, od_ref)):
        o_ref[...] = lax.dot_general(h, w[...], (((1,), (1,)), ((), ())), preferred_element_type=F32)


def _x_to_b_block(i, tm, lat_rows, seq, ctx_len):
    per_b = (seq + ctx_len) // tm
    lat_b = seq // tm
    ctx_b = ctx_len // tm
    n_lat = lat_rows // tm
    lat_idx = (i // lat_b) * per_b + ctx_b + i % lat_b
    ic = i - n_lat
    ctx_idx = (ic // ctx_b) * per_b + ic % ctx_b
    return jnp.where(i < n_lat, lat_idx, ctx_idx)


def _inproj(h, w_in, *, layer, lat_rows, seq, ctx_len, n_batch):
    rows, d = h.shape
    tm = 256
    stage_rows = 512
    splits = (0, SSD_COLS, SSD_COLS + LRU_COLS, SSD_COLS + LRU_COLS + HGRN_COLS, w_in.shape[1])
    widths = [SSD_PAD, LRU_COLS, HGRN_COLS, RET_COLS]

    def out_idx(i):
        return (_x_to_b_block(i, tm, lat_rows, seq, ctx_len), 0)

    return pl.pallas_call(
        functools.partial(_inproj_kernel, layer=layer, splits=splits),
        grid=(rows // tm,),
        in_specs=[pl.BlockSpec((tm, d), lambda i: (i, 0)), pl.BlockSpec(memory_space=pl.ANY)],
        out_specs=[pl.BlockSpec((tm, n), out_idx) for n in widths],
        out_shape=[jax.ShapeDtypeStruct((rows, n), F32) for n in widths],
        scratch_shapes=[pltpu.VMEM((n, d), BF16) for n in widths]
        + [pltpu.VMEM((2, stage_rows, d), F32), pltpu.SemaphoreType.DMA((2,))],
        compiler_params=_cparams(1),
        name="inproj",
    )(h, w_in)


def _outproj_kernel(x_ref, ya_ref, yb_ref, yc_ref, yd_ref, w_ref, mod_ref, gpost_ref, gnext_ref, o_ref, hn_ref):
    w = LRU_WIDTH
    n = EPI_ROWS
    for r in range(x_ref.shape[0] // n):
        rows = slice(r * n, (r + 1) * n)
        ycat = jnp.concatenate([ya_ref[rows, :], yb_ref[rows, :], yc_ref[rows, :], yd_ref[rows, :]], axis=1)
        y = jnp.dot(ycat, w_ref[...], preferred_element_type=F32)
        _residual_epilogue(rows, x_ref[rows, :], y, mod_ref[5:6, :], gpost_ref[...], mod_ref, gnext_ref, 6,
                           o_ref, hn_ref)


def _outproj(x, ys, w, mods, gpost, gpre, *, layer, n_rows, lat_rows, seq, ctx_len, n_batch):
    d = x.shape[1]
    tm = 256
    wy = ys[0].shape[1]
    blk = pl.BlockSpec((tm, d), lambda i: (i, 0))

    def y_idx(i):
        return (_x_to_b_block(i, tm, lat_rows, seq, ctx_len), 0)

    return pl.pallas_call(
        _outproj_kernel,
        grid=(n_rows // tm,),
        in_specs=[blk] + [pl.BlockSpec((tm, wy), y_idx)] * 4
        + [_layer_spec((4 * wy, d), layer),
           pl.BlockSpec((None, None, N_MOD, d), lambda i: _mod_row_idx(i, tm, layer, lat_rows, seq, n_batch)),
           pl.BlockSpec((None, 1, d), lambda i: (layer * 3 + 1, 0, 0)),
           pl.BlockSpec((None, 1, d), lambda i: (layer * 3 + 2, 0, 0))],
        out_specs=[blk, blk],
        out_shape=[jax.ShapeDtypeStruct((n_rows, d), F32), jax.ShapeDtypeStruct((n_rows, d), BF16)],
        compiler_params=_cparams(1),
        name="outproj",
    )(x, *ys, w, mods, gpost, gpre)


def _conv_chunk(u_ref, col0, ncols, base, first, last, t_tot, w_ref, b_ref):
    cols = slice(col0, col0 + ncols)
    prev = u_ref[pl.ds(pl.multiple_of(jnp.maximum(base - 8, 0), 8), 8), cols]
    nxt = u_ref[pl.ds(pl.multiple_of(jnp.minimum(base + CHUNK, t_tot - 8), 8), 8), cols]
    cur = u_ref[pl.ds(base, CHUNK), cols]
    prev = jnp.where(first, 0.0, prev)
    nxt = jnp.where(last, 0.0, nxt)
    win = jnp.concatenate([prev, cur, nxt], axis=0)
    n = CHUNK + 16
    xm1 = pltpu.roll(win, 1, axis=0)[8:8 + CHUNK]
    xp1 = pltpu.roll(win, n - 1, axis=0)[8:8 + CHUNK]
    xp2 = pltpu.roll(win, n - 2, axis=0)[8:8 + CHUNK]
    return (w_ref[0:1, :] * xm1 + w_ref[1:2, :] * cur + w_ref[2:3, :] * xp1 + w_ref[3:4, :] * xp2
            + b_ref[...])


def _chunk_order(i, nc_c, nc, reverse):
    if not reverse:
        return i
    return jnp.where(i < nc_c, nc_c - 1 - i, nc + nc_c - 1 - i)


def _loop_chunks(nc, fn, per_trip=3):
    while nc % per_trip:
        per_trip -= 1

    def trip(i, _):
        for step in range(per_trip):
            fn(per_trip * i + step, 0)
        return 0

    lax.fori_loop(0, nc // per_trip, trip, 0)


def _stream_edges(c, nc_c, nc):
    first = jnp.logical_or(c == 0, c == nc_c)
    last = jnp.logical_or(c == nc_c - 1, c == nc - 1)
    return first, last


def _ssd_kernel(u_ref, cw_ref, cb_ref, dtb_ref, alog_ref, dskip_ref, nw_ref, o_ref,
                act_scr, dl_scr, y_scr, s_scr, *, nc_c, nc):
    t_tot = nc * CHUNK
    xbc0 = SSD_WIDTH
    dt0 = SSD_WIDTH + SSD_XBC

    def prep(c, _):
        base = pl.multiple_of(c * CHUNK, CHUNK)
        first, last = _stream_edges(c, nc_c, nc)
        conv = _conv_chunk(u_ref, xbc0, SSD_XBC, base, first, last, t_tot, cw_ref, cb_ref)
        act_scr[pl.ds(base, CHUNK), :] = _silu(conv)
        dt = u_ref[pl.ds(base, CHUNK), dt0:dt0 + LANES]
        dl_scr[pl.ds(base, CHUNK), :] = _softplus(dt + dtb_ref[...])
        return 0

    lax.fori_loop(0, nc, prep, 0)

    a_neg = -jnp.exp(alog_ref[...])
    row = _iota((CHUNK, CHUNK), 0)
    col = _iota((CHUNK, CHUNK), 1)
    lane_lo = _iota((CHUNK, LANES), 1) < SSD_HEADDIM
    grp_mask = [(_iota((CHUNK, LANES), 1) // SSD_STATE) == g for g in range(SSD_GROUPS)]

    s_scr[...] = jnp.zeros_like(s_scr)
    bodies = []
    for d in range(2):
        tri = (row >= col) if d == 0 else (row <= col)
        tri_b = tri.astype(BF16)
        edge = CHUNK - 1 if d == 0 else 0

        def body(i, d=d, tri=tri, tri_b=tri_b, edge=edge):
            c = _chunk_order(i, nc_c, nc, d == 1)
            base = pl.multiple_of(c * CHUNK, CHUNK)
            act = act_scr[pl.ds(base, CHUNK), :]
            bm = act[:, SSD_WIDTH:SSD_WIDTH + LANES]
            cm = act[:, SSD_WIDTH + LANES:SSD_WIDTH + 2 * LANES]
            delta = dl_scr[pl.ds(base, CHUNK), :]
            la = delta * a_neg
            cum_col = _dot_sel(tri_b, _split3(la))
            cum_row = cum_col.T
            delta_row = delta.T
            bt = bm.T
            gmat = [_bdot(jnp.where(grp_mask[g], cm, 0.0), bt) for g in range(SSD_GROUPS)]
            for k in range(SSD_HEADS // 2):
                g = (2 * k) // (SSD_HEADS // SSD_GROUPS)
                x_tile = act[:, k * LANES:(k + 1) * LANES]
                cg = jnp.where(grp_mask[g], cm, 0.0)
                m_list, w_list, e_list, tot_list = [], [], [], []
                for h in (2 * k, 2 * k + 1):
                    j = d * SSD_HEADS + h
                    ccol = cum_col[:, j:j + 1]
                    crow = cum_row[j:j + 1, :]
                    drow = delta_row[j:j + 1, :]
                    dec = jnp.where(tri, jnp.exp(jnp.where(tri, ccol - crow, 0.0)), 0.0)
                    m_list.append(gmat[g] * dec * drow)
                    tot = cum_row[j:j + 1, edge:edge + 1]
                    w_list.append(bt * (drow * jnp.exp(tot - crow)))
                    e_list.append(jnp.exp(ccol))
                    tot_list.append(jnp.exp(tot))
                y2 = _bdot(jnp.concatenate(m_list, axis=0), x_tile)
                u2 = _bdot(jnp.concatenate(w_list, axis=0), x_tile)
                s_old = s_scr[d, :, k * LANES:(k + 1) * LANES]
                y_inter = _bdot(cg, s_old) * jnp.where(lane_lo, e_list[0], e_list[1])
                y_tile = jnp.where(lane_lo, y2[:CHUNK], y2[CHUNK:]) + y_inter
                s_scr[d, :, k * LANES:(k + 1) * LANES] = (
                    s_old * jnp.where(lane_lo, tot_list[0], tot_list[1])
                    + jnp.where(lane_lo, u2[:CHUNK], u2[CHUNK:]))
                if d == 0:
                    y_tile = y_tile + x_tile * dskip_ref[:, k * LANES:(k + 1) * LANES]
                y_scr[d, pl.ds(base, CHUNK), k * LANES:(k + 1) * LANES] = y_tile

        bodies.append(body)

    def both(i, _):
        for step in range(2):
            bodies[0](2 * i + step)
            bodies[1](2 * i + step)
        return 0

    assert nc % 2 == 0
    lax.fori_loop(0, nc // 2, both, 0)

    def fin(c, _):
        base = pl.multiple_of(c * CHUNK, CHUNK)
        z = u_ref[pl.ds(base, CHUNK), 0:SSD_WIDTH]
        y = (y_scr[0, pl.ds(base, CHUNK), :] + y_scr[1, pl.ds(base, CHUNK), :]) * _silu(z)
        yn = y * lax.rsqrt(jnp.mean(y * y, axis=-1, keepdims=True) + EPS) * nw_ref[...]
        o_ref[pl.ds(base, CHUNK), :] = yn.astype(o_ref.dtype)
        return 0

    _loop_chunks(nc, fin)


def _ssd(u, conv_w, conv_b, dt_bias, a_log, d_skip, norm_w, *, layer, n_batch, t_ctx, t_lat):
    t_tot = t_ctx + t_lat
    nc_c, nc = t_ctx // CHUNK, t_tot // CHUNK
    return pl.pallas_call(
        functools.partial(_ssd_kernel, nc_c=nc_c, nc=nc),
        grid=(n_batch,),
        in_specs=[pl.BlockSpec((t_tot, SSD_PAD), lambda b: (b, 0)),
                  _layer_spec((CONV_W, SSD_XBC), layer), _layer_spec((1, SSD_XBC), layer),
                  _layer_spec((1, LANES), layer), _layer_spec((1, LANES), layer),
                  _layer_spec((1, SSD_WIDTH), layer), _layer_spec((1, SSD_WIDTH), layer)],
        out_specs=pl.BlockSpec((t_tot, SSD_WIDTH), lambda b: (b, 0)),
        out_shape=jax.ShapeDtypeStruct((n_batch * t_tot, SSD_WIDTH), BF16),
        scratch_shapes=[pltpu.VMEM((t_tot, SSD_XBC), F32), pltpu.VMEM((t_tot, LANES), F32),
                        pltpu.VMEM((2, t_tot, SSD_WIDTH), F32), pltpu.VMEM((2, CHUNK, SSD_WIDTH), F32)],
        compiler_params=_cparams(1),
        name="ssd",
    )(u, conv_w, conv_b, dt_bias, a_log, d_skip, norm_w)


def _lru_kernel(u_ref, cw_ref, cb_ref, wg_ref, bg_ref, lam_ref, o_ref, xc_scr, h_scr, *, nc_c, nc):
    t_tot = nc * CHUNK
    w = LRU_WIDTH

    def prep(c, _):
        base = pl.multiple_of(c * CHUNK, CHUNK)
        first, last = _stream_edges(c, nc_c, nc)
        xc_scr[pl.ds(base, CHUNK), :] = _conv_chunk(u_ref, 0, w, base, first, last, t_tot, cw_ref, cb_ref)
        return 0

    lax.fori_loop(0, nc, prep, 0)

    n_tiles = CHUNK // 8
    sub = _iota((n_tiles, 8, w), 1)

    bodies = []
    for d in range(2):
        coef = (-LRU_C * 1.4426950408889634) * _softplus(-lam_ref[d:d + 1, :])

        def body(i, h_prev, d=d, coef=coef):
            c = _chunk_order(i, nc_c, nc, d == 1)
            base = pl.multiple_of(c * CHUNK, CHUNK)
            xc = xc_scr[pl.ds(base, CHUNK), :]
            gates = _bdot(xc, wg_ref[d]) + bg_ref[d:d + 1, :]
            r = _sigmoid(gates[:, :w])
            ig = _sigmoid(gates[:, w:])
            a = jnp.exp2(r * coef)
            z = jnp.maximum(1.0 - a * a, 1e-12)
            b = (z * lax.rsqrt(z)) * (ig * xc)
            a = a.reshape(n_tiles, 8, w)
            b = b.reshape(n_tiles, 8, w)
            for k in (1, 2, 4):
                if d == 0:
                    keep = sub >= k
                    shift = k
                else:
                    keep = sub < 8 - k
                    shift = 8 - k
                a_s = jnp.where(keep, pltpu.roll(a, shift, axis=1), 1.0)
                b_s = jnp.where(keep, pltpu.roll(b, shift, axis=1), 0.0)
                b = a * b_s + b
                a = a * a_s
            tiles = range(n_tiles) if d == 0 else range(n_tiles - 1, -1, -1)
            outs = [None] * n_tiles
            for t in tiles:
                h_t = a[t] * h_prev + b[t]
                outs[t] = h_t
                h_prev = h_t[7:8] if d == 0 else h_t[0:1]
            h_scr[d, pl.ds(base, CHUNK), :] = jnp.concatenate(outs, axis=0)
            return h_prev

        bodies.append(body)

    def both(i, carry):
        hf, hb = carry
        for step in range(2):
            hf = bodies[0](2 * i + step, hf)
            hb = bodies[1](2 * i + step, hb)
        return hf, hb

    assert nc % 2 == 0
    zero = jnp.zeros((1, w), F32)
    lax.fori_loop(0, nc // 2, both, (zero, zero))

    def fin(c, _):
        base = pl.multiple_of(c * CHUNK, CHUNK)
        gate = u_ref[pl.ds(base, CHUNK), w:2 * w]
        out = (h_scr[0, pl.ds(base, CHUNK), :] + h_scr[1, pl.ds(base, CHUNK), :]) * jax.nn.gelu(gate)
        o_ref[pl.ds(base, CHUNK), :] = out.astype(o_ref.dtype)
        return 0

    _loop_chunks(nc, fin)


def _block_diag(wb):
    nb, bi, bj = wb.shape[-3:]
    eye = jnp.eye(nb, dtype=wb.dtype)
    return jnp.einsum('...hij,hg->...higj', wb, eye).reshape(wb.shape[:-3] + (nb * bi, nb * bj))


def _lru(u, conv_w, conv_b, wg, bg, lam, *, layer, n_batch, t_ctx, t_lat):
    t_tot = t_ctx + t_lat
    nc_c, nc = t_ctx // CHUNK, t_tot // CHUNK
    w = LRU_WIDTH
    return pl.pallas_call(
        functools.partial(_lru_kernel, nc_c=nc_c, nc=nc),
        grid=(n_batch,),
        in_specs=[pl.BlockSpec((t_tot, LRU_COLS), lambda b: (b, 0)),
                  _layer_spec((CONV_W, w), layer), _layer_spec((1, w), layer),
                  _layer_spec((2, w, 2 * w), layer), _layer_spec((2, 2 * w), layer), _layer_spec((2, w), layer)],
        out_specs=pl.BlockSpec((t_tot, w), lambda b: (b, 0)),
        out_shape=jax.ShapeDtypeStruct((n_batch * t_tot, w), BF16),
        scratch_shapes=[pltpu.VMEM((t_tot, w), F32), pltpu.VMEM((2, t_tot, w), F32)],
        compiler_params=_cparams(1),
        name="rglru",
    )(u, conv_w, conv_b, wg, bg, lam)


def _hgrn_kernel(q_ref, ff_ref, fb_ref, i_ref, g_ref, lbl_ref, nw_ref, esel_ref, o_ref, of_scr, ob_scr, st_scr, *,
                 layer, nc_c, nc, n_heads):
    n_sub = CHUNK // SUB
    row = _iota((CHUNK, CHUNK), 0)
    col = _iota((CHUNK, CHUNK), 1)
    scale = HGRN_HEADDIM ** -0.5
    log2e = 1.4426950408889634

    lbs = []
    for d in range(2):
        logits = lbl_ref[d]
        ex = jnp.exp(logits - jnp.max(logits, axis=0, keepdims=True))
        sm = ex / jnp.sum(ex, axis=0, keepdims=True)
        lbs.append(jnp.sum(sm[0:layer + 1], axis=0, keepdims=True) - sm[0:1])
    causal = [row >= col, row <= col]
    tri_full = [m.astype(BF16) for m in causal]
    valids = [jnp.logical_and(row // SUB == col // SUB, m) for m in causal]
    levels = tuple(SUB << n for n in range(1, (CHUNK // SUB).bit_length()))
    tpb = SUB // 8
    q_side = [[((row % g) >= g // 2) if d == 0 else ((row % g) < g // 2) for g in levels] for d in range(2)]
    pair_mask = [[jnp.logical_and(row // g == col // g,
                                  jnp.logical_and(q_side[d][n], ((col % g) < g // 2) if d == 0
                                                  else ((col % g) >= g // 2)))
                  for n, g in enumerate(levels)] for d in range(2)]
    st_scr[...] = jnp.zeros_like(st_scr)

    def chunk(d, hh, base):
        cols = slice(hh * LANES, (hh + 1) * LANES)
        lb = lbs[d][:, cols]
        f_ref = ff_ref if d == 0 else fb_ref
        qh = _silu(q_ref[pl.ds(base, CHUNK), cols]) * scale
        f = lb + (1.0 - lb) * _sigmoid(f_ref[pl.ds(base, CHUNK), cols])
        kk = 1.0 - f
        gl = jnp.log(f) * log2e
        v = i_ref[pl.ds(base, CHUNK), cols]
        cum = _dot_sel(tri_full[d], _split3(gl))
        halves = []
        nb = n_sub // 2
        for r0 in range(0, CHUNK, CHUNK // 2):
            rows = slice(r0, r0 + CHUNK // 2)
            q4 = qh[rows].reshape(nb, tpb, 8, LANES)
            c4 = cum[rows].reshape(nb, tpb, 8, LANES)
            k4 = kk[rows].reshape(nb, tpb, 8, LANES)
            zero = jnp.zeros((nb, 8, LANES), F32)
            acc = None
            for s0 in range(0, SUB, 8):
                ps = []
                for s in range(s0, s0 + 8):
                    ts, rs = s // 8, s % 8
                    cs = c4[:, ts, rs:rs + 1, :]
                    ks = k4[:, ts, rs:rs + 1, :]
                    tiles = []
                    for tt in range(tpb):
                        if tt == ts:
                            e = jnp.exp2(jnp.minimum(c4[:, tt] - cs, 0.0))
                        elif (tt > ts) == (d == 0):
                            e = jnp.exp2(c4[:, tt] - cs)
                        else:
                            tiles.append(zero)
                            continue
                        tiles.append((q4[:, tt] * e) * ks)
                    ps.append(jnp.stack(tiles, axis=1).reshape(CHUNK // 2, LANES).astype(BF16))
                part = jnp.dot(jnp.concatenate(ps, axis=1), esel_ref[s0 * LANES:(s0 + 8) * LANES, :],
                               preferred_element_type=F32)
                acc = part if acc is None else acc + part
            halves.append(acc)
        scores = jnp.where(valids[d], jnp.concatenate(halves, axis=0), 0.0)
        for n, g in enumerate(levels):
            pieces = []
            for g0 in range(0, CHUNK, g):
                b = g0 + g // 2 - 1 if d == 0 else g0 + g // 2
                pieces.append(jnp.exp2(-jnp.abs(cum[g0:g0 + g] - cum[b:b + 1])))
            w = pieces[0] if len(pieces) == 1 else jnp.concatenate(pieces, axis=0)
            xg = (jnp.where(q_side[d][n], qh, kk) * w).astype(BF16)
            scores = scores + jnp.where(pair_mask[d][n], _bdot_nt(xg, xg), 0.0)
        y = _bdot(scores, v)
        edge = CHUNK - 1 if d == 0 else 0
        tot = cum[edge:edge + 1]
        st = st_scr[d, hh]
        y = y + _bdot_nt(qh * jnp.exp2(cum), st)
        st_scr[d, hh] = st * jnp.exp2(tot) + _bdot(v.T, kk * jnp.exp2(tot - cum))
        return y

    per_trip = 2

    def body(i, _):
        for step in range(per_trip):
            for d in range(2):
                c = _chunk_order(per_trip * i + step, nc_c, nc, d == 1)
                base = pl.multiple_of(c * CHUNK, CHUNK)
                o_scr = of_scr if d == 0 else ob_scr
                for hh in range(n_heads):
                    o_scr[pl.ds(base, CHUNK), hh * LANES:(hh + 1) * LANES] = chunk(d, hh, base)
        return 0

    assert nc % per_trip == 0
    lax.fori_loop(0, nc // per_trip, body, 0)

    def fin(c, _):
        base = pl.multiple_of(c * CHUNK, CHUNK)
        for hh in range(n_heads):
            cols = slice(hh * LANES, (hh + 1) * LANES)
            o = of_scr[pl.ds(base, CHUNK), cols] + ob_scr[pl.ds(base, CHUNK), cols]
            on = o * lax.rsqrt(jnp.mean(o * o, axis=-1, keepdims=True) + EPS) * nw_ref[...]
            out = on * _silu(g_ref[pl.ds(base, CHUNK), cols])
            o_ref[pl.ds(base, CHUNK), cols] = out.astype(o_ref.dtype)
        return 0

    _loop_chunks(nc, fin)


def _hgrn_selector():
    s_of_row = np.arange(SUB * LANES) // LANES
    return jnp.asarray(s_of_row[:, None] == (np.arange(LANES)[None, :] % SUB), dtype=BF16)


def _hgrn(u, lb_logits, norm_w, *, layer, n_batch, t_ctx, t_lat):
    t_tot = t_ctx + t_lat
    nc_c, nc = t_ctx // CHUNK, t_tot // CHUNK
    n_heads = 2
    ng = HGRN_HEADS // n_heads
    wblk = n_heads * LANES
    depth = lb_logits.shape[1]
    col = lambda part: pl.BlockSpec((t_tot, wblk), lambda b, h, part=part: (b, part * ng + h))
    return pl.pallas_call(
        functools.partial(_hgrn_kernel, layer=layer, nc_c=nc_c, nc=nc, n_heads=n_heads),
        grid=(n_batch, ng),
        in_specs=[col(0), col(1), col(2), col(3), col(4),
                  pl.BlockSpec((2, depth, wblk), lambda b, h: (0, 0, h)),
                  _layer_spec((1, LANES), layer),
                  _const_spec((SUB * LANES, LANES))],
        out_specs=pl.BlockSpec((t_tot, wblk), lambda b, h: (b, h)),
        out_shape=jax.ShapeDtypeStruct((n_batch * t_tot, HGRN_WIDTH), BF16),
        scratch_shapes=[pltpu.VMEM((t_tot, wblk), F32), pltpu.VMEM((t_tot, wblk), F32),
                        pltpu.VMEM((2, n_heads, HGRN_HEADDIM, HGRN_HEADDIM), F32)],
        compiler_params=_cparams(2),
        name="hgrn2",
    )(u, u, u, u, u, lb_logits, norm_w, _hgrn_selector())


def _ret_consts(t_ctx, t_lat):
    gam = 1.0 - np.exp2(-5.0 - np.arange(RET_HEADS, dtype=np.float64))
    t = np.arange(CHUNK, dtype=np.float64)
    diff = np.abs(t[:, None] - t[None, :])
    dsym = gam[:, None, None] ** diff[None]
    dsym[:, np.arange(CHUNK), np.arange(CHUNK)] = 2.0
    head_of_lane = np.repeat(np.arange(RET_HEADS), RET_QK_DIM)
    qf = gam[head_of_lane][None, :] ** (t[:, None] + 1.0)
    qb = gam[head_of_lane][None, :] ** (CHUNK - t[:, None])
    kf = gam[:, None] ** (CHUNK - 1.0 - t[None, :])
    kb = gam[:, None] ** t[None, :]
    gtot = gam ** CHUNK
    pos = np.arange(t_lat)
    rows_p, cols_p = pos // GRID_W, pos % GRID_W
    quarter = RET_QK_DIM // 4
    inv = ROPE_BASE ** (-np.arange(quarter, dtype=np.float64) / quarter)
    ang_r = rows_p[:, None] * inv[None, :]
    ang_c = cols_p[:, None] * inv[None, :]
    cos_h = np.concatenate([np.cos(ang_r), np.cos(ang_r), np.cos(ang_c), np.cos(ang_c)], axis=1)
    sin_h = np.concatenate([-np.sin(ang_r), np.sin(ang_r), -np.sin(ang_c), np.sin(ang_c)], axis=1)
    cos_t = np.concatenate([np.ones((t_ctx, RET_QK_DIM)), cos_h], axis=0)
    sin_t = np.concatenate([np.zeros((t_ctx, RET_QK_DIM)), sin_h], axis=0)
    cos_t = np.tile(cos_t, (1, RET_HEADS))
    sin_t = np.tile(sin_t, (1, RET_HEADS))
    f32 = lambda a: jnp.asarray(a, dtype=F32)
    return (f32(dsym), f32(qf), f32(qb), f32(np.concatenate([kf, kb], axis=0)), [float(g) for g in gtot],
            f32(cos_t), f32(sin_t))


def _ret_kernel(u_ref, dsym_ref, qf_ref, qb_ref, kfac_ref, cos_ref, sin_ref, o_ref, o_scr, s_scr, q_scr, kt_scr, *,
                gtot, nc_c, nc):
    nqk = RET_HEADS * RET_QK_DIM
    lane = _iota((CHUNK, nqk), 1)
    low_half = (lane % (RET_QK_DIM // 2)) < (RET_QK_DIM // 4)
    tile_lane = _iota((CHUNK, LANES), 1)
    head_mask = [(tile_lane // RET_QK_DIM) == p for p in range(2)]
    kscale = RET_QK_DIM ** -0.5

    def rope(x, cos, sin):
        swapped = jnp.where(low_half, pltpu.roll(x, nqk - RET_QK_DIM // 4, axis=1),
                            pltpu.roll(x, RET_QK_DIM // 4, axis=1))
        return x * cos + swapped * sin

    def v_of(base, h):
        return u_ref[pl.ds(base, CHUNK), 2 * nqk + h * RET_V_DIM:2 * nqk + (h + 1) * RET_V_DIM]

    def prep(c):
        base = pl.multiple_of(c * CHUNK, CHUNK)
        cos = cos_ref[pl.ds(base, CHUNK), :]
        sin = sin_ref[pl.ds(base, CHUNK), :]
        q = rope(u_ref[pl.ds(base, CHUNK), 0:nqk], cos, sin)
        k = rope(u_ref[pl.ds(base, CHUNK), nqk:2 * nqk], cos, sin) * kscale
        kt = k.T
        q_scr[pl.ds(base, CHUNK), :] = q
        kt_scr[c] = kt
        for h in range(RET_HEADS):
            p = h // 2
            raw = _bdot(jnp.where(head_mask[h % 2], q[:, p * LANES:(p + 1) * LANES], 0.0),
                        kt[p * LANES:(p + 1) * LANES])
            o_scr[2, pl.ds(base, CHUNK), h * RET_V_DIM:(h + 1) * RET_V_DIM] = _bdot(raw * dsym_ref[h], v_of(base, h))

    def prep_pair(i, _):
        prep(2 * i)
        prep(2 * i + 1)
        return 0

    assert nc % 2 == 0
    lax.fori_loop(0, nc // 2, prep_pair, 0)

    s_scr[...] = jnp.zeros_like(s_scr)
    bodies = []
    for d in range(2):
        qfac = qf_ref if d == 0 else qb_ref

        def body(i, d=d, qfac=qfac):
            c = _chunk_order(i, nc_c, nc, d == 1)
            base = pl.multiple_of(c * CHUNK, CHUNK)
            qd = q_scr[pl.ds(base, CHUNK), :] * qfac[...]
            kt = kt_scr[c]
            for h in range(RET_HEADS):
                p = h // 2
                kt_p = kt[p * LANES:(p + 1) * LANES]
                s_old = s_scr[d, h]
                y = _bdot(jnp.where(head_mask[h % 2], qd[:, p * LANES:(p + 1) * LANES], 0.0), s_old)
                kfac = kfac_ref[d * RET_HEADS + h:d * RET_HEADS + h + 1, :]
                s_scr[d, h] = gtot[h] * s_old + _bdot(kt_p * kfac, v_of(base, h))
                o_scr[d, pl.ds(base, CHUNK), h * RET_V_DIM:(h + 1) * RET_V_DIM] = y

        bodies.append(body)

    def both(i, _):
        for step in range(2):
            bodies[0](2 * i + step)
            bodies[1](2 * i + step)
        return 0

    assert nc % 2 == 0
    lax.fori_loop(0, nc // 2, both, 0)

    def fin(c, _):
        base = pl.multiple_of(c * CHUNK, CHUNK)
        for h in range(RET_HEADS):
            cols = slice(h * RET_V_DIM, (h + 1) * RET_V_DIM)
            o = (o_scr[0, pl.ds(base, CHUNK), cols] + o_scr[1, pl.ds(base, CHUNK), cols]
                 + o_scr[2, pl.ds(base, CHUNK), cols])
            on = o * lax.rsqrt(jnp.mean(o * o, axis=-1, keepdims=True) + EPS)
            gate = u_ref[pl.ds(base, CHUNK), 2 * nqk + RET_WIDTH + h * RET_V_DIM:
                         2 * nqk + RET_WIDTH + (h + 1) * RET_V_DIM]
            o_ref[pl.ds(base, CHUNK), cols] = (_silu(gate) * on).astype(o_ref.dtype)
        return 0

    _loop_chunks(nc, fin)


def _ret(u, *, n_batch, t_ctx, t_lat):
    t_tot = t_ctx + t_lat
    nc_c, nc = t_ctx // CHUNK, t_tot // CHUNK
    dsym, qf, qb, kfac, gtot, cos_t, sin_t = _ret_consts(t_ctx, t_lat)
    nqk = RET_HEADS * RET_QK_DIM
    return pl.pallas_call(
        functools.partial(_ret_kernel, gtot=gtot, nc_c=nc_c, nc=nc),
        grid=(n_batch,),
        in_specs=[pl.BlockSpec((t_tot, RET_COLS), lambda b: (b, 0)),
                  _const_spec((RET_HEADS, CHUNK, CHUNK)), _const_spec((CHUNK, nqk)), _const_spec((CHUNK, nqk)),
                  _const_spec((2 * RET_HEADS, CHUNK)), _const_spec((t_tot, nqk)), _const_spec((t_tot, nqk))],
        out_specs=pl.BlockSpec((t_tot, RET_WIDTH), lambda b: (b, 0)),
        out_shape=jax.ShapeDtypeStruct((n_batch * t_tot, RET_WIDTH), BF16),
        scratch_shapes=[pltpu.VMEM((3, t_tot, RET_WIDTH), F32),
                        pltpu.VMEM((2, RET_HEADS, LANES, RET_V_DIM), F32),
                        pltpu.VMEM((t_tot, nqk), F32), pltpu.VMEM((nc, nqk, CHUNK), F32)],
        compiler_params=_cparams(1),
        name="retention",
    )(u, dsym, qf, qb, kfac, cos_t, sin_t)


def kernel(x, c, ctx, c_ctx, w_mod, b_mod, norm_pre, norm_post, ffn_w1, ffn_w3, ffn_w2, w_in, w_out,
           ssd_conv_w, ssd_conv_b, ssd_dt_bias, ssd_a_log, ssd_d, ssd_norm_w,
           lru_conv_w, lru_conv_b, lru_wa, lru_ba, lru_wx, lru_bx, lru_lambda,
           hgrn_lb_logits, hgrn_norm_w):
    n_batch, seq, d = x.shape
    ctx_len = ctx.shape[1]
    depth = w_mod.shape[0]
    lat_rows = n_batch * seq
    all_rows = lat_rows + n_batch * ctx_len
    geom = dict(lat_rows=lat_rows, seq=seq, ctx_len=ctx_len, n_batch=n_batch)
    mix = dict(n_batch=n_batch, t_ctx=ctx_len, t_lat=seq)

    c_all = jnp.concatenate([c, c_ctx[None, :], jnp.zeros((8 - n_batch - 1, d), F32)], axis=0)
    mods = _modulation(c_all, w_mod, b_mod).reshape(depth, 8, N_MOD, d)
    w2 = ffn_w2
    w_in_parts = jnp.swapaxes(w_in, 1, 2)
    w_out_b = w_out.astype(BF16)
    gpre = norm_pre.reshape(depth * 3, 1, d)
    gpost = norm_post.reshape(depth * 3, 1, d)
    rowvec = lambda v: v.reshape(depth, 1, -1)
    lane_pad = lambda v: jnp.pad(v.reshape(depth, 1, -1), ((0, 0), (0, 0), (0, LANES - v[0].size)))
    ssd_params = (ssd_conv_w, rowvec(ssd_conv_b), lane_pad(ssd_dt_bias), lane_pad(ssd_a_log),
                  rowvec(jnp.repeat(ssd_d, SSD_HEADDIM, axis=-1)), rowvec(ssd_norm_w))
    lru_wg = jnp.concatenate([_block_diag(lru_wa), _block_diag(lru_wx)], axis=-1).astype(BF16)
    lru_bg = jnp.concatenate([lru_ba.reshape(depth, 2, -1), lru_bx.reshape(depth, 2, -1)], axis=-1)
    lru_params = (lru_conv_w, rowvec(lru_conv_b), lru_wg, lru_bg, lru_lambda)
    hgrn_nw = rowvec(hgrn_norm_w)

    rowgeom = dict(lat_rows=lat_rows, seq=seq, n_batch=n_batch)
    xs = (x.reshape(lat_rows, d), ctx.reshape(n_batch * ctx_len, d))
    h = _prenorm(*xs, mods, gpre, **rowgeom)
    for l in range(depth):
        last = l == depth - 1
        g = _ffn_up(h, ffn_w1, ffn_w3, layer=l, k=0, n_rows=all_rows)
        xs, h = _ffn_down(g, xs, mods, gpost, gpre, w2, layer=l, k=0, j=0, nxt=3, nxt_layer=l,
                          n_rows=all_rows, **rowgeom)
        ua, ub, uc, ud = _inproj(h, w_in_parts, layer=l, **geom)
        ya = _ssd(ua, *ssd_params, layer=l, **mix)
        yb = _lru(ub, *lru_params, layer=l, **mix)
        yc = _hgrn(uc, hgrn_lb_logits, hgrn_nw, layer=l, **mix)
        yd = _ret(ud, **mix)
        n_rows = lat_rows if last else all_rows
        xs, h = _outproj(xs, (ya, yb, yc, yd), w_out_b, mods, gpost, gpre, layer=l, n_rows=n_rows, **geom)
        g = _ffn_up(h, ffn_w1, ffn_w3, layer=l, k=1, n_rows=n_rows)
        if last:
            (xs,) = _ffn_down(g, xs, mods, gpost, gpre, w2, layer=l, k=1, j=6, nxt=None, nxt_layer=l,
                              n_rows=n_rows, **rowgeom)
        else:
            xs, h = _ffn_down(g, xs, mods, gpost, gpre, w2, layer=l, k=1, j=6, nxt=0, nxt_layer=l + 1,
                              n_rows=n_rows, **rowgeom)
    return xs.reshape(n_batch, seq, d)
```

```python
import functools

import numpy as np
import jax
import jax.numpy as jnp
from jax import lax
from jax.experimental import pallas as pl
from jax.experimental.pallas import tpu as pltpu

F32 = jnp.float32
BF16 = jnp.bfloat16

N_MOD = 9
CONV_W = 4
EPS = 1e-6
GRID_W = 64

SSD_WIDTH = 512
SSD_HEADDIM = 64
SSD_HEADS = 8
SSD_GROUPS = 2
SSD_STATE = 64
SSD_XBC = SSD_WIDTH + 2 * SSD_GROUPS * SSD_STATE
SSD_COLS = SSD_WIDTH + SSD_XBC + 2 * SSD_HEADS
SSD_PAD = 1408
LRU_WIDTH = 512
LRU_C = 8.0
LRU_COLS = 2 * LRU_WIDTH
HGRN_WIDTH = 512
HGRN_HEADDIM = 128
HGRN_HEADS = 4
HGRN_COLS = 5 * HGRN_WIDTH
RET_WIDTH = 512
RET_V_DIM = 128
RET_HEADS = 4
RET_QK_DIM = 64
RET_COLS = 2 * RET_HEADS * RET_QK_DIM + 2 * RET_WIDTH
ROPE_BASE = 10000.0

LANES = 128
CHUNK = 128
SUB = 8
VMEM_LIMIT = 56 * 1024 * 1024


def _cparams(n_axes):
    return pltpu.CompilerParams(dimension_semantics=("arbitrary",) * n_axes,
                                vmem_limit_bytes=VMEM_LIMIT)


def _bdot(a, b):
    return jnp.dot(a.astype(BF16), b.astype(BF16), preferred_element_type=F32)


def _bdot_nt(a, b):
    return lax.dot_general(a.astype(BF16), b.astype(BF16), (((1,), (1,)), ((), ())),
                           preferred_element_type=F32)


def _split3(x):
    hi = x.astype(BF16)
    r1 = x - hi.astype(F32)
    mid = r1.astype(BF16)
    lo = (r1 - mid.astype(F32)).astype(BF16)
    return hi, mid, lo


def _dot_sel(sel, parts):
    return (jnp.dot(sel, parts[0], preferred_element_type=F32)
            + jnp.dot(sel, parts[1], preferred_element_type=F32)
            + jnp.dot(sel, parts[2], preferred_element_type=F32))


def _sigmoid(x):
    return 1.0 / (1.0 + jnp.exp(-x))


def _silu(x):
    return x * _sigmoid(x)


def _softplus(x):
    return jnp.maximum(x, 0.0) + jnp.log(1.0 + jnp.exp(-jnp.abs(x)))


def _iota(shape, dim):
    return lax.broadcasted_iota(jnp.int32, shape, dim)


def _const_spec(shape):
    return pl.BlockSpec(shape, lambda *_: (0,) * len(shape))


def _layer_spec(shape, layer):
    return pl.BlockSpec((None,) + shape, lambda *_: (layer,) + (0,) * len(shape))


def _mod_kernel(c_ref, w_ref, b_ref, o_ref):
    o_ref[...] = _bdot(_silu(c_ref[...]), w_ref[...]) + b_ref[...]


def _modulation(c_all, w_mod, b_mod):
    depth, d, n = w_mod.shape
    tn = 2048
    return pl.pallas_call(
        _mod_kernel,
        grid=(depth, n // tn),
        in_specs=[pl.BlockSpec((8, d), lambda l, j: (0, 0)),
                  pl.BlockSpec((None, d, tn), lambda l, j: (l, 0, j)),
                  pl.BlockSpec((None, 1, tn), lambda l, j: (l, 0, j))],
        out_specs=pl.BlockSpec((None, 8, tn), lambda l, j: (l, 0, j)),
        out_shape=jax.ShapeDtypeStruct((depth, 8, n), F32),
        compiler_params=_cparams(2),
        name="modulation",
    )(c_all, w_mod, b_mod.reshape(depth, 1, n))


def _mod_row_idx(i, tm, layer, lat_rows, seq, n_batch):
    return (layer, jnp.where(i * tm < lat_rows, (i * tm) // seq, n_batch), 0, 0)


def _mod_norm(x, g, shift, scale):
    return (x * lax.rsqrt(jnp.mean(x * x, axis=-1, keepdims=True) + EPS) * (g * (1.0 + scale))
            + shift).astype(BF16)


def _prenorm_kernel(xl_ref, xc_ref, mod_ref, g_ref, h_ref, *, n_split):
    x = jnp.where(pl.program_id(0) < n_split, xl_ref[...], xc_ref[...])
    h_ref[...] = _mod_norm(x, g_ref[...], mod_ref[0:1, :], mod_ref[1:2, :])


def _prenorm(xl, xc, mods, gpre, *, lat_rows, seq, n_batch):
    d = xl.shape[1]
    rows = xl.shape[0] + xc.shape[0]
    tm = 512
    n_split = xl.shape[0] // tm
    return pl.pallas_call(
        functools.partial(_prenorm_kernel, n_split=n_split),
        grid=(rows // tm,),
        in_specs=[pl.BlockSpec((tm, d), lambda i: (jnp.minimum(i, n_split - 1), 0)),
                  pl.BlockSpec((tm, d), lambda i: (jnp.maximum(i - n_split, 0), 0)),
                  pl.BlockSpec((None, None, N_MOD, d), lambda i: _mod_row_idx(i, tm, 0, lat_rows, seq, n_batch)),
                  pl.BlockSpec((None, 1, d), lambda i: (0, 0, 0))],
        out_specs=pl.BlockSpec((tm, d), lambda i: (i, 0)),
        out_shape=jax.ShapeDtypeStruct((rows, d), BF16),
        compiler_params=_cparams(1),
        name="prenorm",
    )(xl, xc, mods, gpre)


def _ffn_up_kernel(h_ref, w1_ref, w3_ref, g_ref, w1_scr, w3_scr, *, n_sub):
    @pl.when(pl.program_id(1) == 0)
    def _():
        w1_scr[...] = w1_ref[...].astype(BF16)
        w3_scr[...] = w3_ref[...].astype(BF16)

    rows = h_ref.shape[0] // n_sub
    for r in range(n_sub):
        h = h_ref[r * rows:(r + 1) * rows, :]
        a = jnp.dot(h, w1_scr[...], preferred_element_type=F32)
        b = jnp.dot(h, w3_scr[...], preferred_element_type=F32)
        g_ref[r * rows:(r + 1) * rows, :] = (_silu(a) * b).astype(BF16)


def _ffn_up(h, w1, w3, *, layer, k, n_rows):
    d = h.shape[1]
    dff = w1.shape[-1]
    n_sub = 4
    tm, tf = n_rows // 4, 512
    w_spec = pl.BlockSpec((None, None, d, tf), lambda f, i: (layer, k, 0, f))
    return pl.pallas_call(
        functools.partial(_ffn_up_kernel, n_sub=n_sub),
        grid=(dff // tf, n_rows // tm),
        in_specs=[pl.BlockSpec((tm, d), lambda f, i: (i, 0)), w_spec, w_spec],
        out_specs=pl.BlockSpec((tm, tf), lambda f, i: (i, f)),
        out_shape=jax.ShapeDtypeStruct((n_rows, dff), BF16),
        scratch_shapes=[pltpu.VMEM((d, tf), BF16), pltpu.VMEM((d, tf), BF16)],
        compiler_params=_cparams(2),
        name="ffn_up",
    )(h, w1, w3)


EPI_ROWS = 128


def _residual_epilogue(rows, x, y, gate, gpost, mod_ref, gnext_ref, nxt, o_ref, hn_ref):
    xn = x + y * lax.rsqrt(jnp.mean(y * y, axis=-1, keepdims=True) + EPS) * (gate * gpost)
    o_ref[rows, :] = xn
    if nxt is not None:
        hn_ref[rows, :] = _mod_norm(xn, gnext_ref[...], mod_ref[nxt:nxt + 1, :], mod_ref[nxt + 1:nxt + 2, :])


def _ffn_down_kernel(*refs, layer, k, j, nxt, nxt_mod_own, n_split):
    g_ref = refs[0]
    n_x = 1 if n_split is None else 2
    x_refs = refs[1:1 + n_x]
    mod_ref, gpost_ref = refs[1 + n_x:3 + n_x]
    pos = 3 + n_x
    nmod_ref = mod_ref
    gnext_ref = hn_ref = None
    if nxt is not None:
        if not nxt_mod_own:
            nmod_ref = refs[pos]
            pos += 1
        gnext_ref = refs[pos]
        pos += 1
    w2_hbm, o_ref = refs[pos], refs[pos + 1]
    pos += 2
    if nxt is not None:
        hn_ref = refs[pos]
        pos += 1
    w2_scr, stage_scr, sem = refs[pos], refs[pos + 1], refs[pos + 2]

    n = EPI_ROWS
    subs = [slice(r * n, (r + 1) * n) for r in range(g_ref.shape[0] // n)]

    def finish(rows, y):
        if n_split is None:
            x = x_refs[0][rows, :]
        else:
            x = jnp.where(pl.program_id(0) < n_split, x_refs[0][rows, :], x_refs[1][rows, :])
        _residual_epilogue(rows, x, y, 0.5 * mod_ref[j + 2:j + 3, :], gpost_ref[...], nmod_ref, gnext_ref,
                           nxt, o_ref, hn_ref)

    @pl.when(pl.program_id(0) == 0)
    def _():
        rpc = stage_scr.shape[1]
        n_chunks = w2_hbm.shape[2] // rpc

        def copy(c):
            return pltpu.make_async_copy(w2_hbm.at[layer, k, pl.ds(c * rpc, rpc)], stage_scr.at[c % 2],
                                         sem.at[c % 2])

        copy(0).start()
        acc = [None] * len(subs)
        for c in range(n_chunks):
            if c + 1 < n_chunks:
                copy(c + 1).start()
            copy(c).wait()
            wb = stage_scr[c % 2].astype(BF16)
            w2_scr[c * rpc:(c + 1) * rpc, :] = wb
            for r, rows in enumerate(subs):
                part = jnp.dot(g_ref[rows, c * rpc:(c + 1) * rpc], wb, preferred_element_type=F32)
                acc[r] = part if acc[r] is None else acc[r] + part
        for r, rows in enumerate(subs):
            finish(rows, acc[r])

    @pl.when(pl.program_id(0) > 0)
    def _():
        for rows in subs:
            finish(rows, jnp.dot(g_ref[rows, :], w2_scr[...], preferred_element_type=F32))


def _ffn_down(g, x, mods, gpost, gpre, w2, *, layer, k, j, nxt, nxt_layer, n_rows, lat_rows, seq, n_batch):
    xs = x if isinstance(x, tuple) else (x,)
    d = xs[0].shape[1]
    dff = g.shape[1]
    tm = 256
    norm_row = layer * 3 + (0 if j == 0 else 2)
    blk = pl.BlockSpec((tm, d), lambda i: (i, 0))
    mod_spec = lambda ly: pl.BlockSpec((None, None, N_MOD, d),
                                       lambda i: _mod_row_idx(i, tm, ly, lat_rows, seq, n_batch))
    if len(xs) == 1:
        n_split = None
        x_specs = [blk]
    else:
        n_split = xs[0].shape[0] // tm
        x_specs = [pl.BlockSpec((tm, d), lambda i: (jnp.minimum(i, n_split - 1), 0)),
                   pl.BlockSpec((tm, d), lambda i: (jnp.maximum(i - n_split, 0), 0))]
    in_specs = [pl.BlockSpec((tm, dff), lambda i: (i, 0))] + x_specs + [
        mod_spec(layer), pl.BlockSpec((None, 1, d), lambda i: (norm_row, 0, 0))]
    args = [g, *xs, mods, gpost]
    out_specs, out_shape = [blk], [jax.ShapeDtypeStruct((n_rows, d), F32)]
    nxt_mod_own = nxt_layer == layer
    if nxt is not None:
        if not nxt_mod_own:
            in_specs.append(mod_spec(nxt_layer))
            args.append(mods)
        in_specs.append(pl.BlockSpec((None, 1, d), lambda i: (nxt_layer * 3 + nxt // 3, 0, 0)))
        args.append(gpre)
        out_specs.append(blk)
        out_shape.append(jax.ShapeDtypeStruct((n_rows, d), BF16))
    in_specs.append(pl.BlockSpec(memory_space=pl.ANY))
    args.append(w2)
    return pl.pallas_call(
        functools.partial(_ffn_down_kernel, layer=layer, k=k, j=j, nxt=nxt, nxt_mod_own=nxt_mod_own,
                          n_split=n_split),
        grid=(n_rows // tm,),
        in_specs=in_specs,
        out_specs=out_specs,
        out_shape=out_shape,
        scratch_shapes=[pltpu.VMEM((dff, d), BF16), pltpu.VMEM((2, 512, d), F32), pltpu.SemaphoreType.DMA((2,))],
        compiler_params=_cparams(1),
        name="ffn_down",
    )(*args)


def _stage_rows(src_hbm, row0, n_rows, rpc, stage_scr, sem, consume):
    n_chunks = n_rows // rpc
    assert n_chunks * rpc == n_rows and rpc <= stage_scr.shape[1]

    def copy(c, slot):
        return pltpu.make_async_copy(src_hbm.at[pl.ds(row0 + c * rpc, rpc)],
                                     stage_scr.at[slot, pl.ds(0, rpc)], sem.at[slot])

    copy(0, 0).start()

    def step(c, _):
        slot = c % 2

        @pl.when(c + 1 < n_chunks)
        def _():
            copy(c + 1, 1 - slot).start()

        copy(c, slot).wait()
        consume(pl.multiple_of(c * rpc, 16), stage_scr[slot, pl.ds(0, rpc), :])
        return 0

    lax.fori_loop(0, n_chunks, step, 0)


def _inproj_kernel(h_ref, w_hbm, oa_ref, ob_ref, oc_ref, od_ref, wa_scr, wb_scr, wc_scr, wd_scr, stage_scr, sem, *,
                   layer, splits):
    w_scr = (wa_scr, wb_scr, wc_scr, wd_scr)

    @pl.when(pl.program_id(0) == 0)
    def _():
        for m, scr in enumerate(w_scr):
            n_rows = splits[m + 1] - splits[m]
            rpc = max(r for r in range(16, stage_scr.shape[1] + 1, 16) if n_rows % r == 0)

            def consume(r, chunk, scr=scr):
                scr[pl.ds(r, chunk.shape[0]), :] = chunk.astype(BF16)

            _stage_rows(w_hbm.at[layer], splits[m], n_rows, rpc, stage_scr, sem, consume)
            if scr.shape[0] > n_rows:
                scr[n_rows:, :] = jnp.zeros((scr.shape[0] - n_rows, scr.shape[1]), BF16)

    h = h_ref[...]
    for w, o_ref in zip(w_scr, (oa_ref, ob_ref, oc_ref, od_ref)):
        o_ref[...] = lax.dot_general(h, w[...], (((1,), (1,)), ((), ())), preferred_element_type=F32)


def _x_to_b_block(i, tm, lat_rows, seq, ctx_len):
    per_b = (seq + ctx_len) // tm
    lat_b = seq // tm
    ctx_b = ctx_len // tm
    n_lat = lat_rows // tm
    lat_idx = (i // lat_b) * per_b + ctx_b + i % lat_b
    ic = i - n_lat
    ctx_idx = (ic // ctx_b) * per_b + ic % ctx_b
    return jnp.where(i < n_lat, lat_idx, ctx_idx)


def _inproj(h, w_in, *, layer, lat_rows, seq, ctx_len, n_batch):
    rows, d = h.shape
    tm = 256
    stage_rows = 512
    splits = (0, SSD_COLS, SSD_COLS + LRU_COLS, SSD_COLS + LRU_COLS + HGRN_COLS, w_in.shape[1])
    widths = [SSD_PAD, LRU_COLS, HGRN_COLS, RET_COLS]

    def out_idx(i):
        return (_x_to_b_block(i, tm, lat_rows, seq, ctx_len), 0)

    return pl.pallas_call(
        functools.partial(_inproj_kernel, layer=layer, splits=splits),
        grid=(rows // tm,),
        in_specs=[pl.BlockSpec((tm, d), lambda i: (i, 0)), pl.BlockSpec(memory_space=pl.ANY)],
        out_specs=[pl.BlockSpec((tm, n), out_idx) for n in widths],
        out_shape=[jax.ShapeDtypeStruct((rows, n), F32) for n in widths],
        scratch_shapes=[pltpu.VMEM((n, d), BF16) for n in widths]
        + [pltpu.VMEM((2, stage_rows, d), F32), pltpu.SemaphoreType.DMA((2,))],
        compiler_params=_cparams(1),
        name="inproj",
    )(h, w_in)


def _outproj_kernel(x_ref, *refs):
    w_ref, mod_ref, gpost_ref, gnext_ref, o_ref, hn_ref = refs[-6:]
    y_refs = refs[:-6]
    n = EPI_ROWS
    per_blk = x_ref.shape[0] // (len(y_refs) // 4) // n
    for r in range(x_ref.shape[0] // n):
        rows = slice(r * n, (r + 1) * n)
        ys = y_refs[4 * (r // per_blk):4 * (r // per_blk + 1)]
        src = slice((r % per_blk) * n, (r % per_blk + 1) * n)
        ycat = jnp.concatenate([y[src, :] for y in ys], axis=1)
        y = jnp.dot(ycat, w_ref[...], preferred_element_type=F32)
        _residual_epilogue(rows, x_ref[rows, :], y, mod_ref[5:6, :], gpost_ref[...], mod_ref, gnext_ref, 6,
                           o_ref, hn_ref)


def _outproj(x, ys, w, mods, gpost, gpre, *, layer, n_rows, lat_rows, seq, ctx_len, n_batch):
    d = x.shape[1]
    tb = 256
    n_blk = 2
    tm = n_blk * tb
    wy = ys[0].shape[1]
    blk = pl.BlockSpec((tm, d), lambda i: (i, 0))

    def y_spec(k):
        return pl.BlockSpec((tb, wy), lambda i: (_x_to_b_block(n_blk * i + k, tb, lat_rows, seq, ctx_len), 0))

    return pl.pallas_call(
        _outproj_kernel,
        grid=(n_rows // tm,),
        in_specs=[blk] + [y_spec(k) for k in range(n_blk) for _ in range(4)]
        + [_layer_spec((4 * wy, d), layer),
           pl.BlockSpec((None, None, N_MOD, d), lambda i: _mod_row_idx(i, tm, layer, lat_rows, seq, n_batch)),
           pl.BlockSpec((None, 1, d), lambda i: (layer * 3 + 1, 0, 0)),
           pl.BlockSpec((None, 1, d), lambda i: (layer * 3 + 2, 0, 0))],
        out_specs=[blk, blk],
        out_shape=[jax.ShapeDtypeStruct((n_rows, d), F32), jax.ShapeDtypeStruct((n_rows, d), BF16)],
        compiler_params=_cparams(1),
        name="outproj",
    )(x, *(list(ys) * n_blk), w, mods, gpost, gpre)


def _conv_chunk(u_ref, col0, ncols, base, first, last, t_tot, w_ref, b_ref):
    cols = slice(col0, col0 + ncols)
    prev = u_ref[pl.ds(pl.multiple_of(jnp.maximum(base - 8, 0), 8), 8), cols]
    nxt = u_ref[pl.ds(pl.multiple_of(jnp.minimum(base + CHUNK, t_tot - 8), 8), 8), cols]
    cur = u_ref[pl.ds(base, CHUNK), cols]
    prev = jnp.where(first, 0.0, prev)
    nxt = jnp.where(last, 0.0, nxt)
    win = jnp.concatenate([prev, cur, nxt], axis=0)
    n = CHUNK + 16
    xm1 = pltpu.roll(win, 1, axis=0)[8:8 + CHUNK]
    xp1 = pltpu.roll(win, n - 1, axis=0)[8:8 + CHUNK]
    xp2 = pltpu.roll(win, n - 2, axis=0)[8:8 + CHUNK]
    return (w_ref[0:1, :] * xm1 + w_ref[1:2, :] * cur + w_ref[2:3, :] * xp1 + w_ref[3:4, :] * xp2
            + b_ref[...])


def _chunk_order(i, nc_c, nc, reverse):
    if not reverse:
        return i
    return jnp.where(i < nc_c, nc_c - 1 - i, nc + nc_c - 1 - i)


def _loop_chunks(nc, fn, per_trip=3):
    while nc % per_trip:
        per_trip -= 1

    def trip(i, _):
        for step in range(per_trip):
            fn(per_trip * i + step, 0)
        return 0

    lax.fori_loop(0, nc // per_trip, trip, 0)


def _stream_edges(c, nc_c, nc):
    first = jnp.logical_or(c == 0, c == nc_c)
    last = jnp.logical_or(c == nc_c - 1, c == nc - 1)
    return first, last


def _ssd_kernel(u_ref, cw_ref, cb_ref, dtb_ref, alog_ref, dskip_ref, nw_ref, o_ref,
                act_scr, dl_scr, y_scr, s_scr, *, nc_c, nc):
    t_tot = nc * CHUNK
    xbc0 = SSD_WIDTH
    dt0 = SSD_WIDTH + SSD_XBC

    def prep(c, _):
        base = pl.multiple_of(c * CHUNK, CHUNK)
        first, last = _stream_edges(c, nc_c, nc)
        conv = _conv_chunk(u_ref, xbc0, SSD_XBC, base, first, last, t_tot, cw_ref, cb_ref)
        act_scr[pl.ds(base, CHUNK), :] = _silu(conv)
        dt = u_ref[pl.ds(base, CHUNK), dt0:dt0 + LANES]
        dl_scr[pl.ds(base, CHUNK), :] = _softplus(dt + dtb_ref[...])
        return 0

    lax.fori_loop(0, nc, prep, 0)

    a_neg = -jnp.exp(alog_ref[...])
    row = _iota((CHUNK, CHUNK), 0)
    col = _iota((CHUNK, CHUNK), 1)
    lane_lo = _iota((CHUNK, LANES), 1) < SSD_HEADDIM
    grp_mask = [(_iota((CHUNK, LANES), 1) // SSD_STATE) == g for g in range(SSD_GROUPS)]

    s_scr[...] = jnp.zeros_like(s_scr)
    bodies = []
    for d in range(2):
        tri = (row >= col) if d == 0 else (row <= col)
        tri_b = tri.astype(BF16)
        edge = CHUNK - 1 if d == 0 else 0

        def body(i, d=d, tri=tri, tri_b=tri_b, edge=edge):
            c = _chunk_order(i, nc_c, nc, d == 1)
            base = pl.multiple_of(c * CHUNK, CHUNK)
            act = act_scr[pl.ds(base, CHUNK), :]
            bm = act[:, SSD_WIDTH:SSD_WIDTH + LANES]
            cm = act[:, SSD_WIDTH + LANES:SSD_WIDTH + 2 * LANES]
            delta = dl_scr[pl.ds(base, CHUNK), :]
            la = delta * a_neg
            cum_col = _dot_sel(tri_b, _split3(la))
            cum_row = cum_col.T
            delta_row = delta.T
            bt = bm.T
            gmat = [_bdot(jnp.where(grp_mask[g], cm, 0.0), bt) for g in range(SSD_GROUPS)]
            for k in range(SSD_HEADS // 2):
                g = (2 * k) // (SSD_HEADS // SSD_GROUPS)
                x_tile = act[:, k * LANES:(k + 1) * LANES]
                cg = jnp.where(grp_mask[g], cm, 0.0)
                m_list, w_list, e_list, tot_list = [], [], [], []
                for h in (2 * k, 2 * k + 1):
                    j = d * SSD_HEADS + h
                    ccol = cum_col[:, j:j + 1]
                    crow = cum_row[j:j + 1, :]
                    drow = delta_row[j:j + 1, :]
                    dec = jnp.where(tri, jnp.exp(jnp.where(tri, ccol - crow, 0.0)), 0.0)
                    m_list.append(gmat[g] * dec * drow)
                    tot = cum_row[j:j + 1, edge:edge + 1]
                    w_list.append(bt * (drow * jnp.exp(tot - crow)))
                    e_list.append(jnp.exp(ccol))
                    tot_list.append(jnp.exp(tot))
                y2 = _bdot(jnp.concatenate(m_list, axis=0), x_tile)
                u2 = _bdot(jnp.concatenate(w_list, axis=0), x_tile)
                s_old = s_scr[d, :, k * LANES:(k + 1) * LANES]
                y_inter = _bdot(cg, s_old) * jnp.where(lane_lo, e_list[0], e_list[1])
                y_tile = jnp.where(lane_lo, y2[:CHUNK], y2[CHUNK:]) + y_inter
                s_scr[d, :, k * LANES:(k + 1) * LANES] = (
                    s_old * jnp.where(lane_lo, tot_list[0], tot_list[1])
                    + jnp.where(lane_lo, u2[:CHUNK], u2[CHUNK:]))
                if d == 0:
                    y_tile = y_tile + x_tile * dskip_ref[:, k * LANES:(k + 1) * LANES]
                y_scr[d, pl.ds(base, CHUNK), k * LANES:(k + 1) * LANES] = y_tile

        bodies.append(body)

    def both(i, _):
        for step in range(2):
            bodies[0](2 * i + step)
            bodies[1](2 * i + step)
        return 0

    assert nc % 2 == 0
    lax.fori_loop(0, nc // 2, both, 0)

    def fin(c, _):
        base = pl.multiple_of(c * CHUNK, CHUNK)
        z = u_ref[pl.ds(base, CHUNK), 0:SSD_WIDTH]
        y = (y_scr[0, pl.ds(base, CHUNK), :] + y_scr[1, pl.ds(base, CHUNK), :]) * _silu(z)
        yn = y * lax.rsqrt(jnp.mean(y * y, axis=-1, keepdims=True) + EPS) * nw_ref[...]
        o_ref[pl.ds(base, CHUNK), :] = yn.astype(o_ref.dtype)
        return 0

    _loop_chunks(nc, fin)


def _ssd(u, conv_w, conv_b, dt_bias, a_log, d_skip, norm_w, *, layer, n_batch, t_ctx, t_lat):
    t_tot = t_ctx + t_lat
    nc_c, nc = t_ctx // CHUNK, t_tot // CHUNK
    return pl.pallas_call(
        functools.partial(_ssd_kernel, nc_c=nc_c, nc=nc),
        grid=(n_batch,),
        in_specs=[pl.BlockSpec((t_tot, SSD_PAD), lambda b: (b, 0)),
                  _layer_spec((CONV_W, SSD_XBC), layer), _layer_spec((1, SSD_XBC), layer),
                  _layer_spec((1, LANES), layer), _layer_spec((1, LANES), layer),
                  _layer_spec((1, SSD_WIDTH), layer), _layer_spec((1, SSD_WIDTH), layer)],
        out_specs=pl.BlockSpec((t_tot, SSD_WIDTH), lambda b: (b, 0)),
        out_shape=jax.ShapeDtypeStruct((n_batch * t_tot, SSD_WIDTH), BF16),
        scratch_shapes=[pltpu.VMEM((t_tot, SSD_XBC), F32), pltpu.VMEM((t_tot, LANES), F32),
                        pltpu.VMEM((2, t_tot, SSD_WIDTH), F32), pltpu.VMEM((2, CHUNK, SSD_WIDTH), F32)],
        compiler_params=_cparams(1),
        name="ssd",
    )(u, conv_w, conv_b, dt_bias, a_log, d_skip, norm_w)


def _lru_kernel(u_ref, cw_ref, cb_ref, wg_ref, bg_ref, lam_ref, o_ref, xc_scr, h_scr, *, nc_c, nc):
    t_tot = nc * CHUNK
    w = LRU_WIDTH

    def prep(c, _):
        base = pl.multiple_of(c * CHUNK, CHUNK)
        first, last = _stream_edges(c, nc_c, nc)
        xc_scr[pl.ds(base, CHUNK), :] = _conv_chunk(u_ref, 0, w, base, first, last, t_tot, cw_ref, cb_ref)
        return 0

    lax.fori_loop(0, nc, prep, 0)

    n_tiles = CHUNK // 8
    sub = _iota((n_tiles, 8, w), 1)

    bodies = []
    for d in range(2):
        coef = (-LRU_C * 1.4426950408889634) * _softplus(-lam_ref[d:d + 1, :])

        def body(i, h_prev, d=d, coef=coef):
            c = _chunk_order(i, nc_c, nc, d == 1)
            base = pl.multiple_of(c * CHUNK, CHUNK)
            xc = xc_scr[pl.ds(base, CHUNK), :]
            gates = _bdot(xc, wg_ref[d]) + bg_ref[d:d + 1, :]
            r = _sigmoid(gates[:, :w])
            ig = _sigmoid(gates[:, w:])
            a = jnp.exp2(r * coef)
            z = jnp.maximum(1.0 - a * a, 1e-12)
            b = (z * lax.rsqrt(z)) * (ig * xc)
            a = a.reshape(n_tiles, 8, w)
            b = b.reshape(n_tiles, 8, w)
            for k in (1, 2, 4):
                if d == 0:
                    keep = sub >= k
                    shift = k
                else:
                    keep = sub < 8 - k
                    shift = 8 - k
                a_s = jnp.where(keep, pltpu.roll(a, shift, axis=1), 1.0)
                b_s = jnp.where(keep, pltpu.roll(b, shift, axis=1), 0.0)
                b = a * b_s + b
                a = a * a_s
            tiles = range(n_tiles) if d == 0 else range(n_tiles - 1, -1, -1)
            outs = [None] * n_tiles
            for t in tiles:
                h_t = a[t] * h_prev + b[t]
                outs[t] = h_t
                h_prev = h_t[7:8] if d == 0 else h_t[0:1]
            h_scr[d, pl.ds(base, CHUNK), :] = jnp.concatenate(outs, axis=0)
            return h_prev

        bodies.append(body)

    def both(i, carry):
        hf, hb = carry
        for step in range(2):
            hf = bodies[0](2 * i + step, hf)
            hb = bodies[1](2 * i + step, hb)
        return hf, hb

    assert nc % 2 == 0
    zero = jnp.zeros((1, w), F32)
    lax.fori_loop(0, nc // 2, both, (zero, zero))

    def fin(c, _):
        base = pl.multiple_of(c * CHUNK, CHUNK)
        gate = u_ref[pl.ds(base, CHUNK), w:2 * w]
        out = (h_scr[0, pl.ds(base, CHUNK), :] + h_scr[1, pl.ds(base, CHUNK), :]) * jax.nn.gelu(gate)
        o_ref[pl.ds(base, CHUNK), :] = out.astype(o_ref.dtype)
        return 0

    _loop_chunks(nc, fin)


def _block_diag(wb):
    nb, bi, bj = wb.shape[-3:]
    eye = jnp.eye(nb, dtype=wb.dtype)
    return jnp.einsum('...hij,hg->...higj', wb, eye).reshape(wb.shape[:-3] + (nb * bi, nb * bj))


def _lru(u, conv_w, conv_b, wg, bg, lam, *, layer, n_batch, t_ctx, t_lat):
    t_tot = t_ctx + t_lat
    nc_c, nc = t_ctx // CHUNK, t_tot // CHUNK
    w = LRU_WIDTH
    return pl.pallas_call(
        functools.partial(_lru_kernel, nc_c=nc_c, nc=nc),
        grid=(n_batch,),
        in_specs=[pl.BlockSpec((t_tot, LRU_COLS), lambda b: (b, 0)),
                  _layer_spec((CONV_W, w), layer), _layer_spec((1, w), layer),
                  _layer_spec((2, w, 2 * w), layer), _layer_spec((2, 2 * w), layer), _layer_spec((2, w), layer)],
        out_specs=pl.BlockSpec((t_tot, w), lambda b: (b, 0)),
        out_shape=jax.ShapeDtypeStruct((n_batch * t_tot, w), BF16),
        scratch_shapes=[pltpu.VMEM((t_tot, w), F32), pltpu.VMEM((2, t_tot, w), F32)],
        compiler_params=_cparams(1),
        name="rglru",
    )(u, conv_w, conv_b, wg, bg, lam)


def _hgrn_kernel(q_ref, ff_ref, fb_ref, i_ref, g_ref, lbl_ref, nw_ref, esel_ref, o_ref, of_scr, ob_scr, st_scr, *,
                 layer, nc_c, nc, n_heads):
    n_sub = CHUNK // SUB
    row = _iota((CHUNK, CHUNK), 0)
    col = _iota((CHUNK, CHUNK), 1)
    scale = HGRN_HEADDIM ** -0.5
    log2e = 1.4426950408889634

    lbs = []
    for d in range(2):
        logits = lbl_ref[d]
        ex = jnp.exp(logits - jnp.max(logits, axis=0, keepdims=True))
        sm = ex / jnp.sum(ex, axis=0, keepdims=True)
        lbs.append(jnp.sum(sm[0:layer + 1], axis=0, keepdims=True) - sm[0:1])
    causal = [row >= col, row <= col]
    tri_full = [m.astype(BF16) for m in causal]
    valids = [jnp.logical_and(row // SUB == col // SUB, m) for m in causal]
    levels = tuple(SUB << n for n in range(1, (CHUNK // SUB).bit_length()))
    tpb = SUB // 8
    q_side = [[((row % g) >= g // 2) if d == 0 else ((row % g) < g // 2) for g in levels] for d in range(2)]
    pair_mask = [[jnp.logical_and(row // g == col // g,
                                  jnp.logical_and(q_side[d][n], ((col % g) < g // 2) if d == 0
                                                  else ((col % g) >= g // 2)))
                  for n, g in enumerate(levels)] for d in range(2)]
    st_scr[...] = jnp.zeros_like(st_scr)

    def chunk(d, hh, base):
        cols = slice(hh * LANES, (hh + 1) * LANES)
        lb = lbs[d][:, cols]
        f_ref = ff_ref if d == 0 else fb_ref
        qh = _silu(q_ref[pl.ds(base, CHUNK), cols]) * scale
        f = lb + (1.0 - lb) * _sigmoid(f_ref[pl.ds(base, CHUNK), cols])
        kk = 1.0 - f
        gl = jnp.log(f) * log2e
        v = i_ref[pl.ds(base, CHUNK), cols]
        cum = _dot_sel(tri_full[d], _split3(gl))
        halves = []
        nb = n_sub // 2
        for r0 in range(0, CHUNK, CHUNK // 2):
            rows = slice(r0, r0 + CHUNK // 2)
            q4 = qh[rows].reshape(nb, tpb, 8, LANES)
            c4 = cum[rows].reshape(nb, tpb, 8, LANES)
            k4 = kk[rows].reshape(nb, tpb, 8, LANES)
            zero = jnp.zeros((nb, 8, LANES), F32)
            acc = None
            for s0 in range(0, SUB, 8):
                ps = []
                for s in range(s0, s0 + 8):
                    ts, rs = s // 8, s % 8
                    cs = c4[:, ts, rs:rs + 1, :]
                    ks = k4[:, ts, rs:rs + 1, :]
                    tiles = []
                    for tt in range(tpb):
                        if tt == ts:
                            e = jnp.exp2(jnp.minimum(c4[:, tt] - cs, 0.0))
                        elif (tt > ts) == (d == 0):
                            e = jnp.exp2(c4[:, tt] - cs)
                        else:
                            tiles.append(zero)
                            continue
                        tiles.append((q4[:, tt] * e) * ks)
                    ps.append(jnp.stack(tiles, axis=1).reshape(CHUNK // 2, LANES).astype(BF16))
                part = jnp.dot(jnp.concatenate(ps, axis=1), esel_ref[s0 * LANES:(s0 + 8) * LANES, :],
                               preferred_element_type=F32)
                acc = part if acc is None else acc + part
            halves.append(acc)
        scores = jnp.where(valids[d], jnp.concatenate(halves, axis=0), 0.0)
        for n, g in enumerate(levels):
            pieces = []
            for g0 in range(0, CHUNK, g):
                b = g0 + g // 2 - 1 if d == 0 else g0 + g // 2
                pieces.append(jnp.exp2(-jnp.abs(cum[g0:g0 + g] - cum[b:b + 1])))
            w = pieces[0] if len(pieces) == 1 else jnp.concatenate(pieces, axis=0)
            xg = (jnp.where(q_side[d][n], qh, kk) * w).astype(BF16)
            scores = scores + jnp.where(pair_mask[d][n], _bdot_nt(xg, xg), 0.0)
        y = _bdot(scores, v)
        edge = CHUNK - 1 if d == 0 else 0
        tot = cum[edge:edge + 1]
        st = st_scr[d, hh]
        y = y + _bdot_nt(qh * jnp.exp2(cum), st)
        st_scr[d, hh] = st * jnp.exp2(tot) + _bdot(v.T, kk * jnp.exp2(tot - cum))
        return y

    per_trip = 2

    def body(i, _):
        for step in range(per_trip):
            for d in range(2):
                c = _chunk_order(per_trip * i + step, nc_c, nc, d == 1)
                base = pl.multiple_of(c * CHUNK, CHUNK)
                o_scr = of_scr if d == 0 else ob_scr
                for hh in range(n_heads):
                    o_scr[pl.ds(base, CHUNK), hh * LANES:(hh + 1) * LANES] = chunk(d, hh, base)
        return 0

    assert nc % per_trip == 0
    lax.fori_loop(0, nc // per_trip, body, 0)

    def fin(c, _):
        base = pl.multiple_of(c * CHUNK, CHUNK)
        for hh in range(n_heads):
            cols = slice(hh * LANES, (hh + 1) * LANES)
            o = of_scr[pl.ds(base, CHUNK), cols] + ob_scr[pl.ds(base, CHUNK), cols]
            on = o * lax.rsqrt(jnp.mean(o * o, axis=-1, keepdims=True) + EPS) * nw_ref[...]
            out = on * _silu(g_ref[pl.ds(base, CHUNK), cols])
            o_ref[pl.ds(base, CHUNK), cols] = out.astype(o_ref.dtype)
        return 0

    _loop_chunks(nc, fin)


def _hgrn_selector():
    s_of_row = np.arange(SUB * LANES) // LANES
    return jnp.asarray(s_of_row[:, None] == (np.arange(LANES)[None, :] % SUB), dtype=BF16)


def _hgrn(u, lb_logits, norm_w, *, layer, n_batch, t_ctx, t_lat):
    t_tot = t_ctx + t_lat
    nc_c, nc = t_ctx // CHUNK, t_tot // CHUNK
    n_heads = 2
    ng = HGRN_HEADS // n_heads
    wblk = n_heads * LANES
    depth = lb_logits.shape[1]
    col = lambda part: pl.BlockSpec((t_tot, wblk), lambda b, h, part=part: (b, part * ng + h))
    return pl.pallas_call(
        functools.partial(_hgrn_kernel, layer=layer, nc_c=nc_c, nc=nc, n_heads=n_heads),
        grid=(n_batch, ng),
        in_specs=[col(0), col(1), col(2), col(3), col(4),
                  pl.BlockSpec((2, depth, wblk), lambda b, h: (0, 0, h)),
                  _layer_spec((1, LANES), layer),
                  _const_spec((SUB * LANES, LANES))],
        out_specs=pl.BlockSpec((t_tot, wblk), lambda b, h: (b, h)),
        out_shape=jax.ShapeDtypeStruct((n_batch * t_tot, HGRN_WIDTH), BF16),
        scratch_shapes=[pltpu.VMEM((t_tot, wblk), F32), pltpu.VMEM((t_tot, wblk), F32),
                        pltpu.VMEM((2, n_heads, HGRN_HEADDIM, HGRN_HEADDIM), F32)],
        compiler_params=_cparams(2),
        name="hgrn2",
    )(u, u, u, u, u, lb_logits, norm_w, _hgrn_selector())


def _ret_consts(t_ctx, t_lat):
    gam = 1.0 - np.exp2(-5.0 - np.arange(RET_HEADS, dtype=np.float64))
    t = np.arange(CHUNK, dtype=np.float64)
    diff = np.abs(t[:, None] - t[None, :])
    dsym = gam[:, None, None] ** diff[None]
    dsym[:, np.arange(CHUNK), np.arange(CHUNK)] = 2.0
    head_of_lane = np.repeat(np.arange(RET_HEADS), RET_QK_DIM)
    qf = gam[head_of_lane][None, :] ** (t[:, None] + 1.0)
    qb = gam[head_of_lane][None, :] ** (CHUNK - t[:, None])
    kf = gam[:, None] ** (CHUNK - 1.0 - t[None, :])
    kb = gam[:, None] ** t[None, :]
    gtot = gam ** CHUNK
    pos = np.arange(t_lat)
    rows_p, cols_p = pos // GRID_W, pos % GRID_W
    quarter = RET_QK_DIM // 4
    inv = ROPE_BASE ** (-np.arange(quarter, dtype=np.float64) / quarter)
    ang_r = rows_p[:, None] * inv[None, :]
    ang_c = cols_p[:, None] * inv[None, :]
    cos_h = np.concatenate([np.cos(ang_r), np.cos(ang_r), np.cos(ang_c), np.cos(ang_c)], axis=1)
    sin_h = np.concatenate([-np.sin(ang_r), np.sin(ang_r), -np.sin(ang_c), np.sin(ang_c)], axis=1)
    cos_t = np.concatenate([np.ones((t_ctx, RET_QK_DIM)), cos_h], axis=0)
    sin_t = np.concatenate([np.zeros((t_ctx, RET_QK_DIM)), sin_h], axis=0)
    cos_t = np.tile(cos_t, (1, RET_HEADS))
    sin_t = np.tile(sin_t, (1, RET_HEADS))
    f32 = lambda a: jnp.asarray(a, dtype=F32)
    return (f32(dsym), f32(qf), f32(qb), f32(np.concatenate([kf, kb], axis=0)), [float(g) for g in gtot],
            f32(cos_t), f32(sin_t))


def _ret_kernel(u_ref, dsym_ref, qf_ref, qb_ref, kfac_ref, cos_ref, sin_ref, o_ref, o_scr, s_scr, q_scr, kt_scr, *,
                gtot, nc_c, nc):
    nqk = RET_HEADS * RET_QK_DIM
    lane = _iota((CHUNK, nqk), 1)
    low_half = (lane % (RET_QK_DIM // 2)) < (RET_QK_DIM // 4)
    tile_lane = _iota((CHUNK, LANES), 1)
    head_mask = [(tile_lane // RET_QK_DIM) == p for p in range(2)]
    kscale = RET_QK_DIM ** -0.5

    def rope(x, cos, sin):
        swapped = jnp.where(low_half, pltpu.roll(x, nqk - RET_QK_DIM // 4, axis=1),
                            pltpu.roll(x, RET_QK_DIM // 4, axis=1))
        return x * cos + swapped * sin

    def v_of(base, h):
        return u_ref[pl.ds(base, CHUNK), 2 * nqk + h * RET_V_DIM:2 * nqk + (h + 1) * RET_V_DIM]

    def prep(c):
        base = pl.multiple_of(c * CHUNK, CHUNK)
        cos = cos_ref[pl.ds(base, CHUNK), :]
        sin = sin_ref[pl.ds(base, CHUNK), :]
        q = rope(u_ref[pl.ds(base, CHUNK), 0:nqk], cos, sin)
        k = rope(u_ref[pl.ds(base, CHUNK), nqk:2 * nqk], cos, sin) * kscale
        kt = k.T
        q_scr[pl.ds(base, CHUNK), :] = q
        kt_scr[c] = kt
        for h in range(RET_HEADS):
            p = h // 2
            raw = _bdot(jnp.where(head_mask[h % 2], q[:, p * LANES:(p + 1) * LANES], 0.0),
                        kt[p * LANES:(p + 1) * LANES])
            o_scr[2, pl.ds(base, CHUNK), h * RET_V_DIM:(h + 1) * RET_V_DIM] = _bdot(raw * dsym_ref[h], v_of(base, h))

    def prep_pair(i, _):
        prep(2 * i)
        prep(2 * i + 1)
        return 0

    assert nc % 2 == 0
    lax.fori_loop(0, nc // 2, prep_pair, 0)

    s_scr[...] = jnp.zeros_like(s_scr)
    bodies = []
    for d in range(2):
        qfac = qf_ref if d == 0 else qb_ref

        def body(i, d=d, qfac=qfac):
            c = _chunk_order(i, nc_c, nc, d == 1)
            base = pl.multiple_of(c * CHUNK, CHUNK)
            qd = q_scr[pl.ds(base, CHUNK), :] * qfac[...]
            kt = kt_scr[c]
            for h in range(RET_HEADS):
                p = h // 2
                kt_p = kt[p * LANES:(p + 1) * LANES]
                s_old = s_scr[d, h]
                y = _bdot(jnp.where(head_mask[h % 2], qd[:, p * LANES:(p + 1) * LANES], 0.0), s_old)
                kfac = kfac_ref[d * RET_HEADS + h:d * RET_HEADS + h + 1, :]
                s_scr[d, h] = gtot[h] * s_old + _bdot(kt_p * kfac, v_of(base, h))
                o_scr[d, pl.ds(base, CHUNK), h * RET_V_DIM:(h + 1) * RET_V_DIM] = y

        bodies.append(body)

    def both(i, _):
        for step in range(2):
            bodies[0](2 * i + step)
            bodies[1](2 * i + step)
        return 0

    assert nc % 2 == 0
    lax.fori_loop(0, nc // 2, both, 0)

    def fin(c, _):
        base = pl.multiple_of(c * CHUNK, CHUNK)
        for h in range(RET_HEADS):
            cols = slice(h * RET_V_DIM, (h + 1) * RET_V_DIM)
            o = (o_scr[0, pl.ds(base, CHUNK), cols] + o_scr[1, pl.ds(base, CHUNK), cols]
                 + o_scr[2, pl.ds(base, CHUNK), cols])
            on = o * lax.rsqrt(jnp.mean(o * o, axis=-1, keepdims=True) + EPS)
            gate = u_ref[pl.ds(base, CHUNK), 2 * nqk + RET_WIDTH + h * RET_V_DIM:
                         2 * nqk + RET_WIDTH + (h + 1) * RET_V_DIM]
            o_ref[pl.ds(base, CHUNK), cols] = (_silu(gate) * on).astype(o_ref.dtype)
        return 0

    _loop_chunks(nc, fin)


def _ret(u, *, n_batch, t_ctx, t_lat):
    t_tot = t_ctx + t_lat
    nc_c, nc = t_ctx // CHUNK, t_tot // CHUNK
    dsym, qf, qb, kfac, gtot, cos_t, sin_t = _ret_consts(t_ctx, t_lat)
    nqk = RET_HEADS * RET_QK_DIM
    return pl.pallas_call(
        functools.partial(_ret_kernel, gtot=gtot, nc_c=nc_c, nc=nc),
        grid=(n_batch,),
        in_specs=[pl.BlockSpec((t_tot, RET_COLS), lambda b: (b, 0)),
                  _const_spec((RET_HEADS, CHUNK, CHUNK)), _const_spec((CHUNK, nqk)), _const_spec((CHUNK, nqk)),
                  _const_spec((2 * RET_HEADS, CHUNK)), _const_spec((t_tot, nqk)), _const_spec((t_tot, nqk))],
        out_specs=pl.BlockSpec((t_tot, RET_WIDTH), lambda b: (b, 0)),
        out_shape=jax.ShapeDtypeStruct((n_batch * t_tot, RET_WIDTH), BF16),
        scratch_shapes=[pltpu.VMEM((3, t_tot, RET_WIDTH), F32),
                        pltpu.VMEM((2, RET_HEADS, LANES, RET_V_DIM), F32),
                        pltpu.VMEM((t_tot, nqk), F32), pltpu.VMEM((nc, nqk, CHUNK), F32)],
        compiler_params=_cparams(1),
        name="retention",
    )(u, dsym, qf, qb, kfac, cos_t, sin_t)


def kernel(x, c, ctx, c_ctx, w_mod, b_mod, norm_pre, norm_post, ffn_w1, ffn_w3, ffn_w2, w_in, w_out,
           ssd_conv_w, ssd_conv_b, ssd_dt_bias, ssd_a_log, ssd_d, ssd_norm_w,
           lru_conv_w, lru_conv_b, lru_wa, lru_ba, lru_wx, lru_bx, lru_lambda,
           hgrn_lb_logits, hgrn_norm_w):
    n_batch, seq, d = x.shape
    ctx_len = ctx.shape[1]
    depth = w_mod.shape[0]
    lat_rows = n_batch * seq
    all_rows = lat_rows + n_batch * ctx_len
    geom = dict(lat_rows=lat_rows, seq=seq, ctx_len=ctx_len, n_batch=n_batch)
    mix = dict(n_batch=n_batch, t_ctx=ctx_len, t_lat=seq)

    c_all = jnp.concatenate([c, c_ctx[None, :], jnp.zeros((8 - n_batch - 1, d), F32)], axis=0)
    mods = _modulation(c_all, w_mod, b_mod).reshape(depth, 8, N_MOD, d)
    w2 = ffn_w2
    w_in_parts = jnp.swapaxes(w_in, 1, 2)
    w_out_b = w_out.astype(BF16)
    gpre = norm_pre.reshape(depth * 3, 1, d)
    gpost = norm_post.reshape(depth * 3, 1, d)
    rowvec = lambda v: v.reshape(depth, 1, -1)
    lane_pad = lambda v: jnp.pad(v.reshape(depth, 1, -1), ((0, 0), (0, 0), (0, LANES - v[0].size)))
    ssd_params = (ssd_conv_w, rowvec(ssd_conv_b), lane_pad(ssd_dt_bias), lane_pad(ssd_a_log),
                  rowvec(jnp.repeat(ssd_d, SSD_HEADDIM, axis=-1)), rowvec(ssd_norm_w))
    lru_wg = jnp.concatenate([_block_diag(lru_wa), _block_diag(lru_wx)], axis=-1).astype(BF16)
    lru_bg = jnp.concatenate([lru_ba.reshape(depth, 2, -1), lru_bx.reshape(depth, 2, -1)], axis=-1)
    lru_params = (lru_conv_w, rowvec(lru_conv_b), lru_wg, lru_bg, lru_lambda)
    hgrn_nw = rowvec(hgrn_norm_w)

    rowgeom = dict(lat_rows=lat_rows, seq=seq, n_batch=n_batch)
    xs = (x.reshape(lat_rows, d), ctx.reshape(n_batch * ctx_len, d))
    h = _prenorm(*xs, mods, gpre, **rowgeom)
    for l in range(depth):
        last = l == depth - 1
        g = _ffn_up(h, ffn_w1, ffn_w3, layer=l, k=0, n_rows=all_rows)
        xs, h = _ffn_down(g, xs, mods, gpost, gpre, w2, layer=l, k=0, j=0, nxt=3, nxt_layer=l,
                          n_rows=all_rows, **rowgeom)
        ua, ub, uc, ud = _inproj(h, w_in_parts, layer=l, **geom)
        ya = _ssd(ua, *ssd_params, layer=l, **mix)
        yb = _lru(ub, *lru_params, layer=l, **mix)
        yc = _hgrn(uc, hgrn_lb_logits, hgrn_nw, layer=l, **mix)
        yd = _ret(ud, **mix)
        n_rows = lat_rows if last else all_rows
        xs, h = _outproj(xs, (ya, yb, yc, yd), w_out_b, mods, gpost, gpre, layer=l, n_rows=n_rows, **geom)
        g = _ffn_up(h, ffn_w1, ffn_w3, layer=l, k=1, n_rows=n_rows)
        if last:
            (xs,) = _ffn_down(g, xs, mods, gpost, gpre, w2, layer=l, k=1, j=6, nxt=None, nxt_layer=l,
                              n_rows=n_rows, **rowgeom)
        else:
            xs, h = _ffn_down(g, xs, mods, gpost, gpre, w2, layer=l, k=1, j=6, nxt=0, nxt_layer=l + 1,
                              n_rows=n_rows, **rowgeom)
    return xs.reshape(n_batch, seq, d)
```

```python
import functools

import numpy as np
import jax
import jax.numpy as jnp
from jax import lax
from jax.experimental import pallas as pl
from jax.experimental.pallas import tpu as pltpu

F32 = jnp.float32
BF16 = jnp.bfloat16

N_MOD = 9
CONV_W = 4
EPS = 1e-6
GRID_W = 64

SSD_WIDTH = 512
SSD_HEADDIM = 64
SSD_HEADS = 8
SSD_GROUPS = 2
SSD_STATE = 64
SSD_XBC = SSD_WIDTH + 2 * SSD_GROUPS * SSD_STATE
SSD_COLS = SSD_WIDTH + SSD_XBC + 2 * SSD_HEADS
SSD_PAD = 1408
LRU_WIDTH = 512
LRU_C = 8.0
LRU_COLS = 2 * LRU_WIDTH
HGRN_WIDTH = 512
HGRN_HEADDIM = 128
HGRN_HEADS = 4
HGRN_COLS = 5 * HGRN_WIDTH
RET_WIDTH = 512
RET_V_DIM = 128
RET_HEADS = 4
RET_QK_DIM = 64
RET_COLS = 2 * RET_HEADS * RET_QK_DIM + 2 * RET_WIDTH
ROPE_BASE = 10000.0

LANES = 128
CHUNK = 128
SUB = 8
VMEM_LIMIT = 56 * 1024 * 1024


def _cparams(n_axes):
    return pltpu.CompilerParams(dimension_semantics=("arbitrary",) * n_axes,
                                vmem_limit_bytes=VMEM_LIMIT)


def _bdot(a, b):
    return jnp.dot(a.astype(BF16), b.astype(BF16), preferred_element_type=F32)


def _bdot_nt(a, b):
    return lax.dot_general(a.astype(BF16), b.astype(BF16), (((1,), (1,)), ((), ())),
                           preferred_element_type=F32)


def _split3(x):
    hi = x.astype(BF16)
    r1 = x - hi.astype(F32)
    mid = r1.astype(BF16)
    lo = (r1 - mid.astype(F32)).astype(BF16)
    return hi, mid, lo


def _dot_sel(sel, parts):
    return (jnp.dot(sel, parts[0], preferred_element_type=F32)
            + jnp.dot(sel, parts[1], preferred_element_type=F32)
            + jnp.dot(sel, parts[2], preferred_element_type=F32))


def _sigmoid(x):
    return 1.0 / (1.0 + jnp.exp(-x))


def _silu(x):
    return x * _sigmoid(x)


def _softplus(x):
    return jnp.maximum(x, 0.0) + jnp.log(1.0 + jnp.exp(-jnp.abs(x)))


def _iota(shape, dim):
    return lax.broadcasted_iota(jnp.int32, shape, dim)


def _const_spec(shape):
    return pl.BlockSpec(shape, lambda *_: (0,) * len(shape))


def _layer_spec(shape, layer):
    return pl.BlockSpec((None,) + shape, lambda *_: (layer,) + (0,) * len(shape))


def _mod_kernel(c_ref, w_ref, b_ref, o_ref):
    o_ref[...] = _bdot(_silu(c_ref[...]), w_ref[...]) + b_ref[...]


def _modulation(c_all, w_mod, b_mod):
    depth, d, n = w_mod.shape
    tn = 2048
    return pl.pallas_call(
        _mod_kernel,
        grid=(depth, n // tn),
        in_specs=[pl.BlockSpec((8, d), lambda l, j: (0, 0)),
                  pl.BlockSpec((None, d, tn), lambda l, j: (l, 0, j)),
                  pl.BlockSpec((None, 1, tn), lambda l, j: (l, 0, j))],
        out_specs=pl.BlockSpec((None, 8, tn), lambda l, j: (l, 0, j)),
        out_shape=jax.ShapeDtypeStruct((depth, 8, n), F32),
        compiler_params=_cparams(2),
        name="modulation",
    )(c_all, w_mod, b_mod.reshape(depth, 1, n))


def _mod_row_idx(i, tm, layer, lat_rows, seq, n_batch):
    return (layer, jnp.where(i * tm < lat_rows, (i * tm) // seq, n_batch), 0, 0)


def _mod_norm(x, g, shift, scale):
    return (x * lax.rsqrt(jnp.mean(x * x, axis=-1, keepdims=True) + EPS) * (g * (1.0 + scale))
            + shift).astype(BF16)


def _prenorm_kernel(xl_ref, xc_ref, mod_ref, g_ref, h_ref, *, n_split):
    x = jnp.where(pl.program_id(0) < n_split, xl_ref[...], xc_ref[...])
    h_ref[...] = _mod_norm(x, g_ref[...], mod_ref[0:1, :], mod_ref[1:2, :])


def _prenorm(xl, xc, mods, gpre, *, lat_rows, seq, n_batch):
    d = xl.shape[1]
    rows = xl.shape[0] + xc.shape[0]
    tm = 512
    n_split = xl.shape[0] // tm
    return pl.pallas_call(
        functools.partial(_prenorm_kernel, n_split=n_split),
        grid=(rows // tm,),
        in_specs=[pl.BlockSpec((tm, d), lambda i: (jnp.minimum(i, n_split - 1), 0)),
                  pl.BlockSpec((tm, d), lambda i: (jnp.maximum(i - n_split, 0), 0)),
                  pl.BlockSpec((None, None, N_MOD, d), lambda i: _mod_row_idx(i, tm, 0, lat_rows, seq, n_batch)),
                  pl.BlockSpec((None, 1, d), lambda i: (0, 0, 0))],
        out_specs=pl.BlockSpec((tm, d), lambda i: (i, 0)),
        out_shape=jax.ShapeDtypeStruct((rows, d), BF16),
        compiler_params=_cparams(1),
        name="prenorm",
    )(xl, xc, mods, gpre)


def _ffn_up_kernel(h_ref, w1_ref, w3_ref, g_ref, w1_scr, w3_scr, *, n_sub):
    @pl.when(pl.program_id(1) == 0)
    def _():
        w1_scr[...] = w1_ref[...].astype(BF16)
        w3_scr[...] = w3_ref[...].astype(BF16)

    rows = h_ref.shape[0] // n_sub
    for r in range(n_sub):
        h = h_ref[r * rows:(r + 1) * rows, :]
        a = jnp.dot(h, w1_scr[...], preferred_element_type=F32)
        b = jnp.dot(h, w3_scr[...], preferred_element_type=F32)
        g_ref[r * rows:(r + 1) * rows, :] = (_silu(a) * b).astype(BF16)


def _ffn_up(h, w1, w3, *, layer, k, n_rows):
    d = h.shape[1]
    dff = w1.shape[-1]
    n_sub = 4
    tm, tf = n_rows // 4, 512
    w_spec = pl.BlockSpec((None, None, d, tf), lambda f, i: (layer, k, 0, f))
    return pl.pallas_call(
        functools.partial(_ffn_up_kernel, n_sub=n_sub),
        grid=(dff // tf, n_rows // tm),
        in_specs=[pl.BlockSpec((tm, d), lambda f, i: (i, 0)), w_spec, w_spec],
        out_specs=pl.BlockSpec((tm, tf), lambda f, i: (i, f)),
        out_shape=jax.ShapeDtypeStruct((n_rows, dff), BF16),
        scratch_shapes=[pltpu.VMEM((d, tf), BF16), pltpu.VMEM((d, tf), BF16)],
        compiler_params=_cparams(2),
        name="ffn_up",
    )(h, w1, w3)


EPI_ROWS = 128


def _residual_epilogue(rows, x, y, gate, gpost, mod_ref, gnext_ref, nxt, o_ref, hn_ref):
    xn = x + y * lax.rsqrt(jnp.mean(y * y, axis=-1, keepdims=True) + EPS) * (gate * gpost)
    o_ref[rows, :] = xn
    if nxt is not None:
        hn_ref[rows, :] = _mod_norm(xn, gnext_ref[...], mod_ref[nxt:nxt + 1, :], mod_ref[nxt + 1:nxt + 2, :])


def _ffn_down_kernel(*refs, layer, k, j, nxt, nxt_mod_own, n_split):
    g_ref = refs[0]
    n_x = 1 if n_split is None else 2
    x_refs = refs[1:1 + n_x]
    mod_ref, gpost_ref = refs[1 + n_x:3 + n_x]
    pos = 3 + n_x
    nmod_ref = mod_ref
    gnext_ref = hn_ref = None
    if nxt is not None:
        if not nxt_mod_own:
            nmod_ref = refs[pos]
            pos += 1
        gnext_ref = refs[pos]
        pos += 1
    w2_hbm, o_ref = refs[pos], refs[pos + 1]
    pos += 2
    if nxt is not None:
        hn_ref = refs[pos]
        pos += 1
    w2_scr, stage_scr, sem = refs[pos], refs[pos + 1], refs[pos + 2]

    n = EPI_ROWS
    subs = [slice(r * n, (r + 1) * n) for r in range(g_ref.shape[0] // n)]

    def finish(rows, y):
        if n_split is None:
            x = x_refs[0][rows, :]
        else:
            x = jnp.where(pl.program_id(0) < n_split, x_refs[0][rows, :], x_refs[1][rows, :])
        _residual_epilogue(rows, x, y, 0.5 * mod_ref[j + 2:j + 3, :], gpost_ref[...], nmod_ref, gnext_ref,
                           nxt, o_ref, hn_ref)

    @pl.when(pl.program_id(0) == 0)
    def _():
        rpc = stage_scr.shape[1]
        n_chunks = w2_hbm.shape[2] // rpc

        def copy(c):
            return pltpu.make_async_copy(w2_hbm.at[layer, k, pl.ds(c * rpc, rpc)], stage_scr.at[c % 2],
                                         sem.at[c % 2])

        copy(0).start()
        acc = [None] * len(subs)
        for c in range(n_chunks):
            if c + 1 < n_chunks:
                copy(c + 1).start()
            copy(c).wait()
            wb = stage_scr[c % 2].astype(BF16)
            w2_scr[c * rpc:(c + 1) * rpc, :] = wb
            for r, rows in enumerate(subs):
                part = jnp.dot(g_ref[rows, c * rpc:(c + 1) * rpc], wb, preferred_element_type=F32)
                acc[r] = part if acc[r] is None else acc[r] + part
        for r, rows in enumerate(subs):
            finish(rows, acc[r])

    @pl.when(pl.program_id(0) > 0)
    def _():
        for rows in subs:
            finish(rows, jnp.dot(g_ref[rows, :], w2_scr[...], preferred_element_type=F32))


def _ffn_down(g, x, mods, gpost, gpre, w2, *, layer, k, j, nxt, nxt_layer, n_rows, lat_rows, seq, n_batch):
    xs = x if isinstance(x, tuple) else (x,)
    d = xs[0].shape[1]
    dff = g.shape[1]
    tm = 256
    norm_row = layer * 3 + (0 if j == 0 else 2)
    blk = pl.BlockSpec((tm, d), lambda i: (i, 0))
    mod_spec = lambda ly: pl.BlockSpec((None, None, N_MOD, d),
                                       lambda i: _mod_row_idx(i, tm, ly, lat_rows, seq, n_batch))
    if len(xs) == 1:
        n_split = None
        x_specs = [blk]
    else:
        n_split = xs[0].shape[0] // tm
        x_specs = [pl.BlockSpec((tm, d), lambda i: (jnp.minimum(i, n_split - 1), 0)),
                   pl.BlockSpec((tm, d), lambda i: (jnp.maximum(i - n_split, 0), 0))]
    in_specs = [pl.BlockSpec((tm, dff), lambda i: (i, 0))] + x_specs + [
        mod_spec(layer), pl.BlockSpec((None, 1, d), lambda i: (norm_row, 0, 0))]
    args = [g, *xs, mods, gpost]
    out_specs, out_shape = [blk], [jax.ShapeDtypeStruct((n_rows, d), F32)]
    nxt_mod_own = nxt_layer == layer
    if nxt is not None:
        if not nxt_mod_own:
            in_specs.append(mod_spec(nxt_layer))
            args.append(mods)
        in_specs.append(pl.BlockSpec((None, 1, d), lambda i: (nxt_layer * 3 + nxt // 3, 0, 0)))
        args.append(gpre)
        out_specs.append(blk)
        out_shape.append(jax.ShapeDtypeStruct((n_rows, d), BF16))
    in_specs.append(pl.BlockSpec(memory_space=pl.ANY))
    args.append(w2)
    return pl.pallas_call(
        functools.partial(_ffn_down_kernel, layer=layer, k=k, j=j, nxt=nxt, nxt_mod_own=nxt_mod_own,
                          n_split=n_split),
        grid=(n_rows // tm,),
        in_specs=in_specs,
        out_specs=out_specs,
        out_shape=out_shape,
        scratch_shapes=[pltpu.VMEM((dff, d), BF16), pltpu.VMEM((2, 512, d), F32), pltpu.SemaphoreType.DMA((2,))],
        compiler_params=_cparams(1),
        name="ffn_down",
    )(*args)


def _stage_rows(src_hbm, row0, n_rows, rpc, stage_scr, sem, consume):
    n_chunks = n_rows // rpc
    assert n_chunks * rpc == n_rows and rpc <= stage_scr.shape[1]

    def copy(c, slot):
        return pltpu.make_async_copy(src_hbm.at[pl.ds(row0 + c * rpc, rpc)],
                                     stage_scr.at[slot, pl.ds(0, rpc)], sem.at[slot])

    copy(0, 0).start()

    def step(c, _):
        slot = c % 2

        @pl.when(c + 1 < n_chunks)
        def _():
            copy(c + 1, 1 - slot).start()

        copy(c, slot).wait()
        consume(pl.multiple_of(c * rpc, 16), stage_scr[slot, pl.ds(0, rpc), :])
        return 0

    lax.fori_loop(0, n_chunks, step, 0)


def _inproj_kernel(h_ref, w_hbm, oa_ref, ob_ref, oc_ref, od_ref, wa_scr, wb_scr, wc_scr, wd_scr, stage_scr, sem, *,
                   layer, splits):
    w_scr = (wa_scr, wb_scr, wc_scr, wd_scr)

    @pl.when(pl.program_id(0) == 0)
    def _():
        for m, scr in enumerate(w_scr):
            n_rows = splits[m + 1] - splits[m]
            rpc = max(r for r in range(16, stage_scr.shape[1] + 1, 16) if n_rows % r == 0)

            def consume(r, chunk, scr=scr, m=m):
                if m == 2:
                    half = chunk.shape[0] // 2
                    assert chunk.shape[0] == HGRN_WIDTH
                    for pair in range(2):
                        dst = pl.multiple_of(pair * (scr.shape[0] // 2) + r // 2, 16)
                        scr[pl.ds(dst, half), :] = chunk[pair * half:(pair + 1) * half].astype(BF16)
                else:
                    scr[pl.ds(r, chunk.shape[0]), :] = chunk.astype(BF16)

            _stage_rows(w_hbm.at[layer], splits[m], n_rows, rpc, stage_scr, sem, consume)
            if scr.shape[0] > n_rows:
                scr[n_rows:, :] = jnp.zeros((scr.shape[0] - n_rows, scr.shape[1]), BF16)

    h = h_ref[...]
    for w, o_ref in zip(w_scr, (oa_ref, ob_ref, oc_ref, od_ref)):
        o_ref[...] = lax.dot_general(h, w[...], (((1,), (1,)), ((), ())), preferred_element_type=F32)


def _x_to_b_block(i, tm, lat_rows, seq, ctx_len):
    per_b = (seq + ctx_len) // tm
    lat_b = seq // tm
    ctx_b = ctx_len // tm
    n_lat = lat_rows // tm
    lat_idx = (i // lat_b) * per_b + ctx_b + i % lat_b
    ic = i - n_lat
    ctx_idx = (ic // ctx_b) * per_b + ic % ctx_b
    return jnp.where(i < n_lat, lat_idx, ctx_idx)


def _inproj(h, w_in, *, layer, lat_rows, seq, ctx_len, n_batch):
    rows, d = h.shape
    tm = 256
    stage_rows = 512
    splits = (0, SSD_COLS, SSD_COLS + LRU_COLS, SSD_COLS + LRU_COLS + HGRN_COLS, w_in.shape[1])
    widths = [SSD_PAD, LRU_COLS, HGRN_COLS, RET_COLS]

    def out_idx(i):
        return (_x_to_b_block(i, tm, lat_rows, seq, ctx_len), 0)

    return pl.pallas_call(
        functools.partial(_inproj_kernel, layer=layer, splits=splits),
        grid=(rows // tm,),
        in_specs=[pl.BlockSpec((tm, d), lambda i: (i, 0)), pl.BlockSpec(memory_space=pl.ANY)],
        out_specs=[pl.BlockSpec((tm, n), out_idx) for n in widths],
        out_shape=[jax.ShapeDtypeStruct((rows, n), F32) for n in widths],
        scratch_shapes=[pltpu.VMEM((n, d), BF16) for n in widths]
        + [pltpu.VMEM((2, stage_rows, d), F32), pltpu.SemaphoreType.DMA((2,))],
        compiler_params=_cparams(1),
        name="inproj",
    )(h, w_in)


def _outproj_kernel(x_ref, *refs):
    w_ref, mod_ref, gpost_ref, gnext_ref, o_ref, hn_ref = refs[-6:]
    y_refs = refs[:-6]
    n = EPI_ROWS
    per_blk = x_ref.shape[0] // (len(y_refs) // 4) // n
    for r in range(x_ref.shape[0] // n):
        rows = slice(r * n, (r + 1) * n)
        ys = y_refs[4 * (r // per_blk):4 * (r // per_blk + 1)]
        src = slice((r % per_blk) * n, (r % per_blk + 1) * n)
        ycat = jnp.concatenate([y[src, :] for y in ys], axis=1)
        y = jnp.dot(ycat, w_ref[...], preferred_element_type=F32)
        _residual_epilogue(rows, x_ref[rows, :], y, mod_ref[5:6, :], gpost_ref[...], mod_ref, gnext_ref, 6,
                           o_ref, hn_ref)


def _outproj(x, ys, w, mods, gpost, gpre, *, layer, n_rows, lat_rows, seq, ctx_len, n_batch):
    d = x.shape[1]
    tb = 256
    n_blk = 2
    tm = n_blk * tb
    wy = ys[0].shape[1]
    blk = pl.BlockSpec((tm, d), lambda i: (i, 0))

    def y_spec(k):
        return pl.BlockSpec((tb, wy), lambda i: (_x_to_b_block(n_blk * i + k, tb, lat_rows, seq, ctx_len), 0))

    return pl.pallas_call(
        _outproj_kernel,
        grid=(n_rows // tm,),
        in_specs=[blk] + [y_spec(k) for k in range(n_blk) for _ in range(4)]
        + [_layer_spec((4 * wy, d), layer),
           pl.BlockSpec((None, None, N_MOD, d), lambda i: _mod_row_idx(i, tm, layer, lat_rows, seq, n_batch)),
           pl.BlockSpec((None, 1, d), lambda i: (layer * 3 + 1, 0, 0)),
           pl.BlockSpec((None, 1, d), lambda i: (layer * 3 + 2, 0, 0))],
        out_specs=[blk, blk],
        out_shape=[jax.ShapeDtypeStruct((n_rows, d), F32), jax.ShapeDtypeStruct((n_rows, d), BF16)],
        compiler_params=_cparams(1),
        name="outproj",
    )(x, *(list(ys) * n_blk), w, mods, gpost, gpre)


def _conv_chunk(u_ref, col0, ncols, base, first, last, t_tot, w_ref, b_ref):
    cols = slice(col0, col0 + ncols)
    prev = u_ref[pl.ds(pl.multiple_of(jnp.maximum(base - 8, 0), 8), 8), cols]
    nxt = u_ref[pl.ds(pl.multiple_of(jnp.minimum(base + CHUNK, t_tot - 8), 8), 8), cols]
    cur = u_ref[pl.ds(base, CHUNK), cols]
    prev = jnp.where(first, 0.0, prev)
    nxt = jnp.where(last, 0.0, nxt)
    win = jnp.concatenate([prev, cur, nxt], axis=0)
    n = CHUNK + 16
    xm1 = pltpu.roll(win, 1, axis=0)[8:8 + CHUNK]
    xp1 = pltpu.roll(win, n - 1, axis=0)[8:8 + CHUNK]
    xp2 = pltpu.roll(win, n - 2, axis=0)[8:8 + CHUNK]
    return (w_ref[0:1, :] * xm1 + w_ref[1:2, :] * cur + w_ref[2:3, :] * xp1 + w_ref[3:4, :] * xp2
            + b_ref[...])


def _chunk_order(i, nc_c, nc, reverse):
    if not reverse:
        return i
    return jnp.where(i < nc_c, nc_c - 1 - i, nc + nc_c - 1 - i)


def _loop_chunks(nc, fn, per_trip=3):
    while nc % per_trip:
        per_trip -= 1

    def trip(i, _):
        for step in range(per_trip):
            fn(per_trip * i + step, 0)
        return 0

    lax.fori_loop(0, nc // per_trip, trip, 0)


def _stream_edges(c, nc_c, nc):
    first = jnp.logical_or(c == 0, c == nc_c)
    last = jnp.logical_or(c == nc_c - 1, c == nc - 1)
    return first, last


def _ssd_kernel(u_ref, cw_ref, cb_ref, dtb_ref, alog_ref, dskip_ref, nw_ref, o_ref,
                act_scr, dl_scr, y_scr, s_scr, *, nc_c, nc):
    t_tot = nc * CHUNK
    xbc0 = SSD_WIDTH
    dt0 = SSD_WIDTH + SSD_XBC

    def prep(c, _):
        base = pl.multiple_of(c * CHUNK, CHUNK)
        first, last = _stream_edges(c, nc_c, nc)
        conv = _conv_chunk(u_ref, xbc0, SSD_XBC, base, first, last, t_tot, cw_ref, cb_ref)
        act_scr[pl.ds(base, CHUNK), :] = _silu(conv)
        dt = u_ref[pl.ds(base, CHUNK), dt0:dt0 + LANES]
        dl_scr[pl.ds(base, CHUNK), :] = _softplus(dt + dtb_ref[...])
        return 0

    lax.fori_loop(0, nc, prep, 0)

    a_neg = -jnp.exp(alog_ref[...])
    row = _iota((CHUNK, CHUNK), 0)
    col = _iota((CHUNK, CHUNK), 1)
    lane_lo = _iota((CHUNK, LANES), 1) < SSD_HEADDIM
    grp_mask = [(_iota((CHUNK, LANES), 1) // SSD_STATE) == g for g in range(SSD_GROUPS)]

    s_scr[...] = jnp.zeros_like(s_scr)
    bodies = []
    for d in range(2):
        tri = (row >= col) if d == 0 else (row <= col)
        tri_b = tri.astype(BF16)
        edge = CHUNK - 1 if d == 0 else 0

        def body(i, d=d, tri=tri, tri_b=tri_b, edge=edge):
            c = _chunk_order(i, nc_c, nc, d == 1)
            base = pl.multiple_of(c * CHUNK, CHUNK)
            act = act_scr[pl.ds(base, CHUNK), :]
            bm = act[:, SSD_WIDTH:SSD_WIDTH + LANES]
            cm = act[:, SSD_WIDTH + LANES:SSD_WIDTH + 2 * LANES]
            delta = dl_scr[pl.ds(base, CHUNK), :]
            la = delta * a_neg
            cum_col = _dot_sel(tri_b, _split3(la))
            cum_row = cum_col.T
            delta_row = delta.T
            bt = bm.T
            gmat = [_bdot(jnp.where(grp_mask[g], cm, 0.0), bt) for g in range(SSD_GROUPS)]
            for k in range(SSD_HEADS // 2):
                g = (2 * k) // (SSD_HEADS // SSD_GROUPS)
                x_tile = act[:, k * LANES:(k + 1) * LANES]
                cg = jnp.where(grp_mask[g], cm, 0.0)
                m_list, w_list, e_list, tot_list = [], [], [], []
                for h in (2 * k, 2 * k + 1):
                    j = d * SSD_HEADS + h
                    ccol = cum_col[:, j:j + 1]
                    crow = cum_row[j:j + 1, :]
                    drow = delta_row[j:j + 1, :]
                    dec = jnp.where(tri, jnp.exp(jnp.where(tri, ccol - crow, 0.0)), 0.0)
                    m_list.append(gmat[g] * dec * drow)
                    tot = cum_row[j:j + 1, edge:edge + 1]
                    w_list.append(bt * (drow * jnp.exp(tot - crow)))
                    e_list.append(jnp.exp(ccol))
                    tot_list.append(jnp.exp(tot))
                y2 = _bdot(jnp.concatenate(m_list, axis=0), x_tile)
                u2 = _bdot(jnp.concatenate(w_list, axis=0), x_tile)
                s_old = s_scr[d, :, k * LANES:(k + 1) * LANES]
                y_inter = _bdot(cg, s_old) * jnp.where(lane_lo, e_list[0], e_list[1])
                y_tile = jnp.where(lane_lo, y2[:CHUNK], y2[CHUNK:]) + y_inter
                s_scr[d, :, k * LANES:(k + 1) * LANES] = (
                    s_old * jnp.where(lane_lo, tot_list[0], tot_list[1])
                    + jnp.where(lane_lo, u2[:CHUNK], u2[CHUNK:]))
                if d == 0:
                    y_tile = y_tile + x_tile * dskip_ref[:, k * LANES:(k + 1) * LANES]
                y_scr[d, pl.ds(base, CHUNK), k * LANES:(k + 1) * LANES] = y_tile

        bodies.append(body)

    def both(i, _):
        for step in range(2):
            bodies[0](2 * i + step)
            bodies[1](2 * i + step)
        return 0

    assert nc % 2 == 0
    lax.fori_loop(0, nc // 2, both, 0)

    def fin(c, _):
        base = pl.multiple_of(c * CHUNK, CHUNK)
        z = u_ref[pl.ds(base, CHUNK), 0:SSD_WIDTH]
        y = (y_scr[0, pl.ds(base, CHUNK), :] + y_scr[1, pl.ds(base, CHUNK), :]) * _silu(z)
        yn = y * lax.rsqrt(jnp.mean(y * y, axis=-1, keepdims=True) + EPS) * nw_ref[...]
        o_ref[pl.ds(base, CHUNK), :] = yn.astype(o_ref.dtype)
        return 0

    _loop_chunks(nc, fin)


def _ssd(u, conv_w, conv_b, dt_bias, a_log, d_skip, norm_w, *, layer, n_batch, t_ctx, t_lat):
    t_tot = t_ctx + t_lat
    nc_c, nc = t_ctx // CHUNK, t_tot // CHUNK
    return pl.pallas_call(
        functools.partial(_ssd_kernel, nc_c=nc_c, nc=nc),
        grid=(n_batch,),
        in_specs=[pl.BlockSpec((t_tot, SSD_PAD), lambda b: (b, 0)),
                  _layer_spec((CONV_W, SSD_XBC), layer), _layer_spec((1, SSD_XBC), layer),
                  _layer_spec((1, LANES), layer), _layer_spec((1, LANES), layer),
                  _layer_spec((1, SSD_WIDTH), layer), _layer_spec((1, SSD_WIDTH), layer)],
        out_specs=pl.BlockSpec((t_tot, SSD_WIDTH), lambda b: (b, 0)),
        out_shape=jax.ShapeDtypeStruct((n_batch * t_tot, SSD_WIDTH), BF16),
        scratch_shapes=[pltpu.VMEM((t_tot, SSD_XBC), F32), pltpu.VMEM((t_tot, LANES), F32),
                        pltpu.VMEM((2, t_tot, SSD_WIDTH), F32), pltpu.VMEM((2, CHUNK, SSD_WIDTH), F32)],
        compiler_params=_cparams(1),
        name="ssd",
    )(u, conv_w, conv_b, dt_bias, a_log, d_skip, norm_w)


def _lru_kernel(u_ref, cw_ref, cb_ref, wg_ref, bg_ref, lam_ref, o_ref, xc_scr, h_scr, *, nc_c, nc):
    t_tot = nc * CHUNK
    w = LRU_WIDTH

    def prep(c, _):
        base = pl.multiple_of(c * CHUNK, CHUNK)
        first, last = _stream_edges(c, nc_c, nc)
        xc_scr[pl.ds(base, CHUNK), :] = _conv_chunk(u_ref, 0, w, base, first, last, t_tot, cw_ref, cb_ref)
        return 0

    lax.fori_loop(0, nc, prep, 0)

    n_tiles = CHUNK // 8
    sub = _iota((n_tiles, 8, w), 1)

    bodies = []
    for d in range(2):
        coef = (-LRU_C * 1.4426950408889634) * _softplus(-lam_ref[d:d + 1, :])

        def body(i, h_prev, d=d, coef=coef):
            c = _chunk_order(i, nc_c, nc, d == 1)
            base = pl.multiple_of(c * CHUNK, CHUNK)
            xc = xc_scr[pl.ds(base, CHUNK), :]
            gates = _bdot(xc, wg_ref[d]) + bg_ref[d:d + 1, :]
            r = _sigmoid(gates[:, :w])
            ig = _sigmoid(gates[:, w:])
            a = jnp.exp2(r * coef)
            z = jnp.maximum(1.0 - a * a, 1e-12)
            b = (z * lax.rsqrt(z)) * (ig * xc)
            a = a.reshape(n_tiles, 8, w)
            b = b.reshape(n_tiles, 8, w)
            for k in (1, 2, 4):
                if d == 0:
                    keep = sub >= k
                    shift = k
                else:
                    keep = sub < 8 - k
                    shift = 8 - k
                a_s = jnp.where(keep, pltpu.roll(a, shift, axis=1), 1.0)
                b_s = jnp.where(keep, pltpu.roll(b, shift, axis=1), 0.0)
                b = a * b_s + b
                a = a * a_s
            tiles = range(n_tiles) if d == 0 else range(n_tiles - 1, -1, -1)
            outs = [None] * n_tiles
            for t in tiles:
                h_t = a[t] * h_prev + b[t]
                outs[t] = h_t
                h_prev = h_t[7:8] if d == 0 else h_t[0:1]
            h_scr[d, pl.ds(base, CHUNK), :] = jnp.concatenate(outs, axis=0)
            return h_prev

        bodies.append(body)

    def both(i, carry):
        hf, hb = carry
        for step in range(2):
            hf = bodies[0](2 * i + step, hf)
            hb = bodies[1](2 * i + step, hb)
        return hf, hb

    assert nc % 2 == 0
    zero = jnp.zeros((1, w), F32)
    lax.fori_loop(0, nc // 2, both, (zero, zero))

    def fin(c, _):
        base = pl.multiple_of(c * CHUNK, CHUNK)
        gate = u_ref[pl.ds(base, CHUNK), w:2 * w]
        out = (h_scr[0, pl.ds(base, CHUNK), :] + h_scr[1, pl.ds(base, CHUNK), :]) * jax.nn.gelu(gate)
        o_ref[pl.ds(base, CHUNK), :] = out.astype(o_ref.dtype)
        return 0

    _loop_chunks(nc, fin)


def _block_diag(wb):
    nb, bi, bj = wb.shape[-3:]
    eye = jnp.eye(nb, dtype=wb.dtype)
    return jnp.einsum('...hij,hg->...higj', wb, eye).reshape(wb.shape[:-3] + (nb * bi, nb * bj))


def _lru(u, conv_w, conv_b, wg, bg, lam, *, layer, n_batch, t_ctx, t_lat):
    t_tot = t_ctx + t_lat
    nc_c, nc = t_ctx // CHUNK, t_tot // CHUNK
    w = LRU_WIDTH
    return pl.pallas_call(
        functools.partial(_lru_kernel, nc_c=nc_c, nc=nc),
        grid=(n_batch,),
        in_specs=[pl.BlockSpec((t_tot, LRU_COLS), lambda b: (b, 0)),
                  _layer_spec((CONV_W, w), layer), _layer_spec((1, w), layer),
                  _layer_spec((2, w, 2 * w), layer), _layer_spec((2, 2 * w), layer), _layer_spec((2, w), layer)],
        out_specs=pl.BlockSpec((t_tot, w), lambda b: (b, 0)),
        out_shape=jax.ShapeDtypeStruct((n_batch * t_tot, w), BF16),
        scratch_shapes=[pltpu.VMEM((t_tot, w), F32), pltpu.VMEM((2, t_tot, w), F32)],
        compiler_params=_cparams(1),
        name="rglru",
    )(u, conv_w, conv_b, wg, bg, lam)


def _hgrn_kernel(u_ref, lbl_ref, nw_ref, esel_ref, o_ref, of_scr, ob_scr, st_scr, *,
                 layer, nc_c, nc, n_heads):
    wblk = n_heads * LANES
    q_ref, ff_ref, fb_ref, i_ref, g_ref = (u_ref.at[:, p * wblk:(p + 1) * wblk] for p in range(5))
    n_sub = CHUNK // SUB
    row = _iota((CHUNK, CHUNK), 0)
    col = _iota((CHUNK, CHUNK), 1)
    scale = HGRN_HEADDIM ** -0.5
    log2e = 1.4426950408889634

    lbs = []
    for d in range(2):
        logits = lbl_ref[d]
        ex = jnp.exp(logits - jnp.max(logits, axis=0, keepdims=True))
        sm = ex / jnp.sum(ex, axis=0, keepdims=True)
        lbs.append(jnp.sum(sm[0:layer + 1], axis=0, keepdims=True) - sm[0:1])
    causal = [row >= col, row <= col]
    tri_full = [m.astype(BF16) for m in causal]
    valids = [jnp.logical_and(row // SUB == col // SUB, m) for m in causal]
    levels = tuple(SUB << n for n in range(1, (CHUNK // SUB).bit_length()))
    tpb = SUB // 8
    q_side = [[((row % g) >= g // 2) if d == 0 else ((row % g) < g // 2) for g in levels] for d in range(2)]
    pair_mask = [[jnp.logical_and(row // g == col // g,
                                  jnp.logical_and(q_side[d][n], ((col % g) < g // 2) if d == 0
                                                  else ((col % g) >= g // 2)))
                  for n, g in enumerate(levels)] for d in range(2)]
    st_scr[...] = jnp.zeros_like(st_scr)

    def chunk(d, hh, base):
        cols = slice(hh * LANES, (hh + 1) * LANES)
        lb = lbs[d][:, cols]
        f_ref = ff_ref if d == 0 else fb_ref
        qh = _silu(q_ref[pl.ds(base, CHUNK), cols]) * scale
        f = lb + (1.0 - lb) * _sigmoid(f_ref[pl.ds(base, CHUNK), cols])
        kk = 1.0 - f
        gl = jnp.log(f) * log2e
        v = i_ref[pl.ds(base, CHUNK), cols]
        cum = _dot_sel(tri_full[d], _split3(gl))
        halves = []
        nb = n_sub // 2
        for r0 in range(0, CHUNK, CHUNK // 2):
            rows = slice(r0, r0 + CHUNK // 2)
            q4 = qh[rows].reshape(nb, tpb, 8, LANES)
            c4 = cum[rows].reshape(nb, tpb, 8, LANES)
            k4 = kk[rows].reshape(nb, tpb, 8, LANES)
            zero = jnp.zeros((nb, 8, LANES), F32)
            acc = None
            for s0 in range(0, SUB, 8):
                ps = []
                for s in range(s0, s0 + 8):
                    ts, rs = s // 8, s % 8
                    cs = c4[:, ts, rs:rs + 1, :]
                    ks = k4[:, ts, rs:rs + 1, :]
                    tiles = []
                    for tt in range(tpb):
                        if tt == ts:
                            e = jnp.exp2(jnp.minimum(c4[:, tt] - cs, 0.0))
                        elif (tt > ts) == (d == 0):
                            e = jnp.exp2(c4[:, tt] - cs)
                        else:
                            tiles.append(zero)
                            continue
                        tiles.append((q4[:, tt] * e) * ks)
                    ps.append(jnp.stack(tiles, axis=1).reshape(CHUNK // 2, LANES).astype(BF16))
                part = jnp.dot(jnp.concatenate(ps, axis=1), esel_ref[s0 * LANES:(s0 + 8) * LANES, :],
                               preferred_element_type=F32)
                acc = part if acc is None else acc + part
            halves.append(acc)
        scores = jnp.where(valids[d], jnp.concatenate(halves, axis=0), 0.0)
        for n, g in enumerate(levels):
            pieces = []
            for g0 in range(0, CHUNK, g):
                b = g0 + g // 2 - 1 if d == 0 else g0 + g // 2
                pieces.append(jnp.exp2(-jnp.abs(cum[g0:g0 + g] - cum[b:b + 1])))
            w = pieces[0] if len(pieces) == 1 else jnp.concatenate(pieces, axis=0)
            xg = (jnp.where(q_side[d][n], qh, kk) * w).astype(BF16)
            scores = scores + jnp.where(pair_mask[d][n], _bdot_nt(xg, xg), 0.0)
        y = _bdot(scores, v)
        edge = CHUNK - 1 if d == 0 else 0
        tot = cum[edge:edge + 1]
        st = st_scr[d, hh]
        y = y + _bdot_nt(qh * jnp.exp2(cum), st)
        st_scr[d, hh] = st * jnp.exp2(tot) + _bdot(v.T, kk * jnp.exp2(tot - cum))
        return y

    per_trip = 2

    def body(i, _):
        for step in range(per_trip):
            for d in range(2):
                c = _chunk_order(per_trip * i + step, nc_c, nc, d == 1)
                base = pl.multiple_of(c * CHUNK, CHUNK)
                o_scr = of_scr if d == 0 else ob_scr
                for hh in range(n_heads):
                    o_scr[pl.ds(base, CHUNK), hh * LANES:(hh + 1) * LANES] = chunk(d, hh, base)
        return 0

    assert nc % per_trip == 0
    lax.fori_loop(0, nc // per_trip, body, 0)

    def fin(c, _):
        base = pl.multiple_of(c * CHUNK, CHUNK)
        for hh in range(n_heads):
            cols = slice(hh * LANES, (hh + 1) * LANES)
            o = of_scr[pl.ds(base, CHUNK), cols] + ob_scr[pl.ds(base, CHUNK), cols]
            on = o * lax.rsqrt(jnp.mean(o * o, axis=-1, keepdims=True) + EPS) * nw_ref[...]
            out = on * _silu(g_ref[pl.ds(base, CHUNK), cols])
            o_ref[pl.ds(base, CHUNK), cols] = out.astype(o_ref.dtype)
        return 0

    _loop_chunks(nc, fin)


def _hgrn_selector():
    s_of_row = np.arange(SUB * LANES) // LANES
    return jnp.asarray(s_of_row[:, None] == (np.arange(LANES)[None, :] % SUB), dtype=BF16)


def _hgrn(u, lb_logits, norm_w, *, layer, n_batch, t_ctx, t_lat):
    t_tot = t_ctx + t_lat
    nc_c, nc = t_ctx // CHUNK, t_tot // CHUNK
    n_heads = 2
    ng = HGRN_HEADS // n_heads
    wblk = n_heads * LANES
    depth = lb_logits.shape[1]
    return pl.pallas_call(
        functools.partial(_hgrn_kernel, layer=layer, nc_c=nc_c, nc=nc, n_heads=n_heads),
        grid=(n_batch, ng),
        in_specs=[pl.BlockSpec((t_tot, 5 * wblk), lambda b, h: (b, h)),
                  pl.BlockSpec((2, depth, wblk), lambda b, h: (0, 0, h)),
                  _layer_spec((1, LANES), layer),
                  _const_spec((SUB * LANES, LANES))],
        out_specs=pl.BlockSpec((t_tot, wblk), lambda b, h: (b, h)),
        out_shape=jax.ShapeDtypeStruct((n_batch * t_tot, HGRN_WIDTH), BF16),
        scratch_shapes=[pltpu.VMEM((t_tot, wblk), F32), pltpu.VMEM((t_tot, wblk), F32),
                        pltpu.VMEM((2, n_heads, HGRN_HEADDIM, HGRN_HEADDIM), F32)],
        compiler_params=_cparams(2),
        name="hgrn2",
    )(u, lb_logits, norm_w, _hgrn_selector())


def _ret_consts(t_ctx, t_lat):
    gam = 1.0 - np.exp2(-5.0 - np.arange(RET_HEADS, dtype=np.float64))
    t = np.arange(CHUNK, dtype=np.float64)
    diff = np.abs(t[:, None] - t[None, :])
    dsym = gam[:, None, None] ** diff[None]
    dsym[:, np.arange(CHUNK), np.arange(CHUNK)] = 2.0
    head_of_lane = np.repeat(np.arange(RET_HEADS), RET_QK_DIM)
    qf = gam[head_of_lane][None, :] ** (t[:, None] + 1.0)
    qb = gam[head_of_lane][None, :] ** (CHUNK - t[:, None])
    kf = gam[:, None] ** (CHUNK - 1.0 - t[None, :])
    kb = gam[:, None] ** t[None, :]
    gtot = gam ** CHUNK
    pos = np.arange(t_lat)
    rows_p, cols_p = pos // GRID_W, pos % GRID_W
    quarter = RET_QK_DIM // 4
    inv = ROPE_BASE ** (-np.arange(quarter, dtype=np.float64) / quarter)
    ang_r = rows_p[:, None] * inv[None, :]
    ang_c = cols_p[:, None] * inv[None, :]
    cos_h = np.concatenate([np.cos(ang_r), np.cos(ang_r), np.cos(ang_c), np.cos(ang_c)], axis=1)
    sin_h = np.concatenate([-np.sin(ang_r), np.sin(ang_r), -np.sin(ang_c), np.sin(ang_c)], axis=1)
    cos_t = np.concatenate([np.ones((t_ctx, RET_QK_DIM)), cos_h], axis=0)
    sin_t = np.concatenate([np.zeros((t_ctx, RET_QK_DIM)), sin_h], axis=0)
    cos_t = np.tile(cos_t, (1, RET_HEADS))
    sin_t = np.tile(sin_t, (1, RET_HEADS))
    f32 = lambda a: jnp.asarray(a, dtype=F32)
    return (f32(dsym), f32(qf), f32(qb), f32(np.concatenate([kf, kb], axis=0)), [float(g) for g in gtot],
            f32(cos_t), f32(sin_t))


def _ret_kernel(u_ref, dsym_ref, qf_ref, qb_ref, kfac_ref, cos_ref, sin_ref, o_ref, o_scr, s_scr, q_scr, kt_scr, *,
                gtot, nc_c, nc):
    nqk = RET_HEADS * RET_QK_DIM
    lane = _iota((CHUNK, nqk), 1)
    low_half = (lane % (RET_QK_DIM // 2)) < (RET_QK_DIM // 4)
    tile_lane = _iota((CHUNK, LANES), 1)
    head_mask = [(tile_lane // RET_QK_DIM) == p for p in range(2)]
    kscale = RET_QK_DIM ** -0.5

    def rope(x, cos, sin):
        swapped = jnp.where(low_half, pltpu.roll(x, nqk - RET_QK_DIM // 4, axis=1),
                            pltpu.roll(x, RET_QK_DIM // 4, axis=1))
        return x * cos + swapped * sin

    def v_of(base, h):
        return u_ref[pl.ds(base, CHUNK), 2 * nqk + h * RET_V_DIM:2 * nqk + (h + 1) * RET_V_DIM]

    def prep(c):
        base = pl.multiple_of(c * CHUNK, CHUNK)
        cos = cos_ref[pl.ds(base, CHUNK), :]
        sin = sin_ref[pl.ds(base, CHUNK), :]
        q = rope(u_ref[pl.ds(base, CHUNK), 0:nqk], cos, sin)
        k = rope(u_ref[pl.ds(base, CHUNK), nqk:2 * nqk], cos, sin) * kscale
        kt = k.T
        q_scr[pl.ds(base, CHUNK), :] = q
        kt_scr[c] = kt
        for h in range(RET_HEADS):
            p = h // 2
            raw = _bdot(jnp.where(head_mask[h % 2], q[:, p * LANES:(p + 1) * LANES], 0.0),
                        kt[p * LANES:(p + 1) * LANES])
            o_scr[2, pl.ds(base, CHUNK), h * RET_V_DIM:(h + 1) * RET_V_DIM] = _bdot(raw * dsym_ref[h], v_of(base, h))

    def prep_pair(i, _):
        prep(2 * i)
        prep(2 * i + 1)
        return 0

    assert nc % 2 == 0
    lax.fori_loop(0, nc // 2, prep_pair, 0)

    s_scr[...] = jnp.zeros_like(s_scr)
    bodies = []
    for d in range(2):
        qfac = qf_ref if d == 0 else qb_ref

        def body(i, d=d, qfac=qfac):
            c = _chunk_order(i, nc_c, nc, d == 1)
            base = pl.multiple_of(c * CHUNK, CHUNK)
            qd = q_scr[pl.ds(base, CHUNK), :] * qfac[...]
            kt = kt_scr[c]
            for h in range(RET_HEADS):
                p = h // 2
                kt_p = kt[p * LANES:(p + 1) * LANES]
                s_old = s_scr[d, h]
                y = _bdot(jnp.where(head_mask[h % 2], qd[:, p * LANES:(p + 1) * LANES], 0.0), s_old)
                kfac = kfac_ref[d * RET_HEADS + h:d * RET_HEADS + h + 1, :]
                s_scr[d, h] = gtot[h] * s_old + _bdot(kt_p * kfac, v_of(base, h))
                o_scr[d, pl.ds(base, CHUNK), h * RET_V_DIM:(h + 1) * RET_V_DIM] = y

        bodies.append(body)

    def both(i, _):
        for step in range(2):
            bodies[0](2 * i + step)
            bodies[1](2 * i + step)
        return 0

    assert nc % 2 == 0
    lax.fori_loop(0, nc // 2, both, 0)

    def fin(c, _):
        base = pl.multiple_of(c * CHUNK, CHUNK)
        for h in range(RET_HEADS):
            cols = slice(h * RET_V_DIM, (h + 1) * RET_V_DIM)
            o = (o_scr[0, pl.ds(base, CHUNK), cols] + o_scr[1, pl.ds(base, CHUNK), cols]
                 + o_scr[2, pl.ds(base, CHUNK), cols])
            on = o * lax.rsqrt(jnp.mean(o * o, axis=-1, keepdims=True) + EPS)
            gate = u_ref[pl.ds(base, CHUNK), 2 * nqk + RET_WIDTH + h * RET_V_DIM:
                         2 * nqk + RET_WIDTH + (h + 1) * RET_V_DIM]
            o_ref[pl.ds(base, CHUNK), cols] = (_silu(gate) * on).astype(o_ref.dtype)
        return 0

    _loop_chunks(nc, fin)


def _ret(u, *, n_batch, t_ctx, t_lat):
    t_tot = t_ctx + t_lat
    nc_c, nc = t_ctx // CHUNK, t_tot // CHUNK
    dsym, qf, qb, kfac, gtot, cos_t, sin_t = _ret_consts(t_ctx, t_lat)
    nqk = RET_HEADS * RET_QK_DIM
    return pl.pallas_call(
        functools.partial(_ret_kernel, gtot=gtot, nc_c=nc_c, nc=nc),
        grid=(n_batch,),
        in_specs=[pl.BlockSpec((t_tot, RET_COLS), lambda b: (b, 0)),
                  _const_spec((RET_HEADS, CHUNK, CHUNK)), _const_spec((CHUNK, nqk)), _const_spec((CHUNK, nqk)),
                  _const_spec((2 * RET_HEADS, CHUNK)), _const_spec((t_tot, nqk)), _const_spec((t_tot, nqk))],
        out_specs=pl.BlockSpec((t_tot, RET_WIDTH), lambda b: (b, 0)),
        out_shape=jax.ShapeDtypeStruct((n_batch * t_tot, RET_WIDTH), BF16),
        scratch_shapes=[pltpu.VMEM((3, t_tot, RET_WIDTH), F32),
                        pltpu.VMEM((2, RET_HEADS, LANES, RET_V_DIM), F32),
                        pltpu.VMEM((t_tot, nqk), F32), pltpu.VMEM((nc, nqk, CHUNK), F32)],
        compiler_params=_cparams(1),
        name="retention",
    )(u, dsym, qf, qb, kfac, cos_t, sin_t)


def kernel(x, c, ctx, c_ctx, w_mod, b_mod, norm_pre, norm_post, ffn_w1, ffn_w3, ffn_w2, w_in, w_out,
           ssd_conv_w, ssd_conv_b, ssd_dt_bias, ssd_a_log, ssd_d, ssd_norm_w,
           lru_conv_w, lru_conv_b, lru_wa, lru_ba, lru_wx, lru_bx, lru_lambda,
           hgrn_lb_logits, hgrn_norm_w):
    n_batch, seq, d = x.shape
    ctx_len = ctx.shape[1]
    depth = w_mod.shape[0]
    lat_rows = n_batch * seq
    all_rows = lat_rows + n_batch * ctx_len
    geom = dict(lat_rows=lat_rows, seq=seq, ctx_len=ctx_len, n_batch=n_batch)
    mix = dict(n_batch=n_batch, t_ctx=ctx_len, t_lat=seq)

    c_all = jnp.concatenate([c, c_ctx[None, :], jnp.zeros((8 - n_batch - 1, d), F32)], axis=0)
    mods = _modulation(c_all, w_mod, b_mod).reshape(depth, 8, N_MOD, d)
    w2 = ffn_w2
    w_in_parts = jnp.swapaxes(w_in, 1, 2)
    w_out_b = w_out.astype(BF16)
    gpre = norm_pre.reshape(depth * 3, 1, d)
    gpost = norm_post.reshape(depth * 3, 1, d)
    rowvec = lambda v: v.reshape(depth, 1, -1)
    lane_pad = lambda v: jnp.pad(v.reshape(depth, 1, -1), ((0, 0), (0, 0), (0, LANES - v[0].size)))
    ssd_params = (ssd_conv_w, rowvec(ssd_conv_b), lane_pad(ssd_dt_bias), lane_pad(ssd_a_log),
                  rowvec(jnp.repeat(ssd_d, SSD_HEADDIM, axis=-1)), rowvec(ssd_norm_w))
    lru_wg = jnp.concatenate([_block_diag(lru_wa), _block_diag(lru_wx)], axis=-1).astype(BF16)
    lru_bg = jnp.concatenate([lru_ba.reshape(depth, 2, -1), lru_bx.reshape(depth, 2, -1)], axis=-1)
    lru_params = (lru_conv_w, rowvec(lru_conv_b), lru_wg, lru_bg, lru_lambda)
    hgrn_nw = rowvec(hgrn_norm_w)

    rowgeom = dict(lat_rows=lat_rows, seq=seq, n_batch=n_batch)
    xs = (x.reshape(lat_rows, d), ctx.reshape(n_batch * ctx_len, d))
    h = _prenorm(*xs, mods, gpre, **rowgeom)
    for l in range(depth):
        last = l == depth - 1
        g = _ffn_up(h, ffn_w1, ffn_w3, layer=l, k=0, n_rows=all_rows)
        xs, h = _ffn_down(g, xs, mods, gpost, gpre, w2, layer=l, k=0, j=0, nxt=3, nxt_layer=l,
                          n_rows=all_rows, **rowgeom)
        ua, ub, uc, ud = _inproj(h, w_in_parts, layer=l, **geom)
        ya = _ssd(ua, *ssd_params, layer=l, **mix)
        yb = _lru(ub, *lru_params, layer=l, **mix)
        yc = _hgrn(uc, hgrn_lb_logits, hgrn_nw, layer=l, **mix)
        yd = _ret(ud, **mix)
        n_rows = lat_rows if last else all_rows
        xs, h = _outproj(xs, (ya, yb, yc, yd), w_out_b, mods, gpost, gpre, layer=l, n_rows=n_rows, **geom)
        g = _ffn_up(h, ffn_w1, ffn_w3, layer=l, k=1, n_rows=n_rows)
        if last:
            (xs,) = _ffn_down(g, xs, mods, gpost, gpre, w2, layer=l, k=1, j=6, nxt=None, nxt_layer=l,
                              n_rows=n_rows, **rowgeom)
        else:
            xs, h = _ffn_down(g, xs, mods, gpost, gpre, w2, layer=l, k=1, j=6, nxt=0, nxt_layer=l + 1,
                              n_rows=n_rows, **rowgeom)
    return xs.reshape(n_batch, seq, d)
```
